```python
import math
import jax
import jax.numpy as jnp
from jax import lax
import numpy as np

D_MODEL = 1024
BATCH = 32
SEQ = 2048
DEPTH = 2
DEC_BATCH = 128
DEC_SEQ = 1
PAST_LEN = 16384
PAGE_SIZE = 128

N_MIXERS = 4
GROUP_WIDTH = D_MODEL // N_MIXERS
MIX_WIDTH = N_MIXERS * GROUP_WIDTH
HEAD_DIM = 64
ROPE_THETA = 10000.0
NORM_EPS = 1e-6
Q_BLOCK = 128

A_HEADS = GROUP_WIDTH // HEAD_DIM
A_KV_HEADS = 1
IDX_HEADS = 8
IDX_DIM = 32
TOPK_MAX = 256
IDX_SCALE = (IDX_HEADS * IDX_DIM) ** -0.5

SSM_P = HEAD_DIM
SSM_HEADS = GROUP_WIDTH // SSM_P
SSM_INNER = SSM_HEADS * SSM_P
SSM_GROUPS = 2
D_STATE = 128
CONV_W = 4
CONV_CH = SSM_INNER + 2 * SSM_GROUPS * D_STATE
SSD_CHUNK = 128

C_HEADS = GROUP_WIDTH // HEAD_DIM
C_KV_HEADS = 2

D_HEADS = GROUP_WIDTH // HEAD_DIM
D_NOPE = 64
D_ROPE = 32
D_V = GROUP_WIDTH // D_HEADS
Q_RANK = 256
KV_RANK = 128
MLA_SCALE = (D_NOPE + D_ROPE) ** -0.5

MEM_TOKENS = 256
MEM_HEADS = 4
MEM_HD = 64

D_FF = -(-8 * D_MODEL // (3 * 256)) * 256

IN_SIZES = (A_HEADS * HEAD_DIM, A_KV_HEADS * HEAD_DIM, A_KV_HEADS * HEAD_DIM, IDX_HEADS * IDX_DIM, IDX_DIM, IDX_HEADS,
            SSM_INNER, CONV_CH, SSM_HEADS,
            C_HEADS * HEAD_DIM, C_KV_HEADS * HEAD_DIM, C_KV_HEADS * HEAD_DIM,
            Q_RANK, KV_RANK, D_ROPE)
IN_TOTAL = sum(IN_SIZES)

kernel_name = 'hybrid_dsa_ssd_stickbreak_mla_decoder_step'


def _in_offsets():
    return [int(o) for o in np.cumsum(IN_SIZES)[:-1]]


def _rmsnorm(x, g):
    xf = x.astype(jnp.float32)
    y = xf * lax.rsqrt(jnp.mean(xf * xf, axis=-1, keepdims=True) + NORM_EPS)
    return (y * g.astype(jnp.float32)).astype(x.dtype)


def _rope(x, pos):
    half = x.shape[-1] // 2
    inv = 1.0 / (ROPE_THETA ** (jnp.arange(half, dtype=jnp.float32) / half))
    ang = pos.astype(jnp.float32)[:, None] * inv[None, :]
    ang = ang.reshape((ang.shape[0],) + (1,) * (x.ndim - 3) + (half,))
    cos, sin = jnp.cos(ang), jnp.sin(ang)
    xf = x.astype(jnp.float32)
    x1, x2 = xf[..., :half], xf[..., half:]
    return jnp.concatenate([x1 * cos - x2 * sin, x2 * cos + x1 * sin], axis=-1).astype(x.dtype)


def _gather_pages(pool, l, page_table):
    g = pool[l, page_table]
    return g.reshape((g.shape[0], g.shape[1] * g.shape[2]) + g.shape[3:])


def _gather_rows(pool, l, page_table, new, idx, past_len):
    page = jnp.clip(idx // PAGE_SIZE, 0, page_table.shape[1] - 1)
    phys = jnp.take_along_axis(page_table, page, axis=1)
    rows = pool[l, phys, idx % PAGE_SIZE]
    extra = (1,) * (new.ndim - 2)
    j = jnp.clip(idx - past_len, 0, new.shape[1] - 1)
    fresh = jnp.take_along_axis(new, j.reshape(j.shape + extra), axis=1)
    return jnp.where((idx < past_len).reshape(idx.shape + extra), rows.astype(new.dtype), fresh)


def _project(hn, pos, p):
    b, L, _ = hn.shape
    (a_q, a_k, a_v, a_qi, a_ki, a_w, b_z, b_xbc, b_dt, c_q, c_k, c_v, d_cq, d_ckv, d_kr) = jnp.split(
        hn @ p['w_in'], _in_offsets(), axis=-1)
    q_d = (_rmsnorm(d_cq, p['d_q_norm_g']) @ p['d_w_uq']).reshape(b, L, D_HEADS, D_NOPE + D_ROPE)
    return {
        'a_q': _rope(a_q.reshape(b, L, A_KV_HEADS, A_HEADS // A_KV_HEADS, HEAD_DIM), pos),
        'a_k': _rope(a_k.reshape(b, L, A_KV_HEADS, HEAD_DIM), pos),
        'a_v': a_v.reshape(b, L, A_KV_HEADS, HEAD_DIM),
        'a_qi': _rope(a_qi.reshape(b, L, IDX_HEADS, IDX_DIM), pos),
        'a_ki': _rope(a_ki, pos),
        'a_w': a_w,
        'b_z': b_z, 'b_xbc': b_xbc, 'b_dt': b_dt,
        'c_q': c_q.reshape(b, L, C_KV_HEADS, C_HEADS // C_KV_HEADS, HEAD_DIM),
        'c_k': c_k.reshape(b, L, C_KV_HEADS, HEAD_DIM),
        'c_v': c_v.reshape(b, L, C_KV_HEADS, HEAD_DIM),
        'd_qn': q_d[..., :D_NOPE],
        'd_qr': _rope(q_d[..., D_NOPE:], pos),
        'd_ckv': _rmsnorm(d_ckv, p['d_kv_norm_g']),
        'd_kr': _rope(d_kr, pos),
    }


def _indexer_scores(q_idx, w_idx, k_idx, q_pos, k_pos):
    rel = jax.nn.relu(jnp.einsum('bqhe,bse->bqhs', q_idx, k_idx).astype(jnp.float32))
    sc = jnp.einsum('bqh,bqhs->bqs', w_idx.astype(jnp.float32), rel) * IDX_SCALE
    return jnp.where(k_pos[None, None, :] <= q_pos[None, :, None], sc, -jnp.inf)


def _sparse_attend(q, k_sel, v_sel, valid):
    sc = jnp.einsum('btgrd,btkgd->btgrk', q, k_sel).astype(jnp.float32) * HEAD_DIM ** -0.5
    sc = jnp.where(valid[:, :, None, None, :], sc, -jnp.inf)
    pr = jax.nn.softmax(sc, axis=-1).astype(v_sel.dtype)
    return jnp.einsum('btgrk,btkgd->btgrd', pr, v_sel)


def _dsa_prompt(q, k, v, q_idx, k_idx, w_idx):
    b, L = q.shape[:2]
    n_sel = min(TOPK_MAX, L // 4)
    k_pos = jnp.arange(L)

    def block(i):
        t0 = i * Q_BLOCK
        qp = t0 + jnp.arange(Q_BLOCK)
        sl = lambda a: lax.dynamic_slice_in_dim(a, t0, Q_BLOCK, axis=1)
        _, idx = lax.top_k(_indexer_scores(sl(q_idx), sl(w_idx), k_idx, qp, k_pos), n_sel)
        valid = idx <= qp[None, :, None]
        flat = idx.reshape(b, -1, 1, 1)
        ks = jnp.take_along_axis(k, flat, axis=1).reshape(b, Q_BLOCK, n_sel, A_KV_HEADS, HEAD_DIM)
        vs = jnp.take_along_axis(v, flat, axis=1).reshape(b, Q_BLOCK, n_sel, A_KV_HEADS, HEAD_DIM)
        return _sparse_attend(sl(q), ks, vs, valid)

    out = lax.map(block, jnp.arange(L // Q_BLOCK))
    return jnp.moveaxis(out, 0, 1).reshape(b, L, A_HEADS * HEAD_DIM)


def _ssd_scan(x, dt, a, bm, cm, s0):
    b, l, h, p = x.shape
    g, n = bm.shape[2], bm.shape[3]
    r = h // g
    q = SSD_CHUNK if l % SSD_CHUNK == 0 else l
    c = l // q
    f32 = jnp.float32
    xdt = (x.astype(f32) * dt[..., None]).reshape(b, c, q, g, r, p)
    acs = jnp.cumsum((dt * a).reshape(b, c, q, g, r), axis=2)
    bc = bm.astype(f32).reshape(b, c, q, g, n)
    cc = cm.astype(f32).reshape(b, c, q, g, n)
    acs_t = jnp.moveaxis(acs, 2, -1)
    tril = jnp.tril(jnp.ones((q, q), dtype=bool))
    seg = jnp.exp(jnp.where(tril, acs_t[..., :, None] - acs_t[..., None, :], -jnp.inf))
    cb = jnp.einsum('bcign,bcjgn->bcgij', cc, bc)
    y_diag = jnp.einsum('bcgrij,bcjgrp->bcigrp', cb[:, :, :, None] * seg, xdt)
    decay_end = jnp.exp(acs[:, :, -1:] - acs)
    states = jnp.einsum('bcjgn,bcjgrp->bcgrpn', bc, xdt * decay_end[..., None])
    chunk_decay = jnp.exp(acs[:, :, -1])

    def step(s, inp):
        st, dc = inp
        return dc[..., None, None] * s + st, s

    s_fin, s_in = lax.scan(step, s0.astype(f32).reshape(b, g, r, p, n),
                           (jnp.moveaxis(states, 1, 0), jnp.moveaxis(chunk_decay, 1, 0)))
    s_in = jnp.moveaxis(s_in, 0, 1)
    y_off = jnp.einsum('bcign,bcgrpn->bcigrp', cc, s_in) * jnp.exp(acs)[..., None]
    return (y_diag + y_off).reshape(b, l, h, p), s_fin.reshape(b, h, p, n)


def _ssd_mixer(z, xbc, dt_raw, conv_prev, ssm_prev, p):
    b, T = xbc.shape[:2]
    f32 = jnp.float32
    xin = jnp.concatenate([conv_prev.astype(xbc.dtype), xbc], axis=1)
    cw = p['b_conv_w']
    conv = p['b_conv_b'] + xin[:, 0:T] * cw[0]
    for w in range(1, CONV_W):
        conv = conv + xin[:, w:w + T] * cw[w]
    xbc_c = jax.nn.silu(conv)
    xs, bm, cm = jnp.split(xbc_c, [SSM_INNER, SSM_INNER + SSM_GROUPS * D_STATE], axis=-1)
    xs = xs.reshape(b, T, SSM_HEADS, SSM_P)
    bm = bm.reshape(b, T, SSM_GROUPS, D_STATE)
    cm = cm.reshape(b, T, SSM_GROUPS, D_STATE)
    dt = jax.nn.softplus(dt_raw.astype(f32) + p['b_dt_bias'].astype(f32))
    a = -jnp.exp(p['b_a_log'].astype(f32))
    y, s_fin = _ssd_scan(xs, dt, a, bm, cm, ssm_prev)
    y = y + p['b_d'].astype(f32)[:, None] * xs.astype(f32)
    y = y.reshape(b, T, SSM_INNER) * jax.nn.silu(z.astype(f32))
    out = _rmsnorm(y, p['b_norm_g']).astype(z.dtype)
    return out, xin[:, -(CONV_W - 1):], s_fin


def _stick_breaking(q, k, v, q_pos, k_pos):
    z = jnp.einsum('btgrd,bsgd->bgrts', q, k).astype(jnp.float32) * HEAD_DIM ** -0.5
    m = k_pos[None, :] < q_pos[:, None]
    log_1m = jnp.where(m, jax.nn.log_sigmoid(-z), 0.0)
    cum = jnp.cumsum(log_1m, axis=-1)
    log_w = jax.nn.log_sigmoid(z) + cum[..., -1:] - cum
    w = jnp.where(m, jnp.exp(log_w), 0.0).astype(v.dtype)
    return jnp.einsum('bgrts,bsgd->btgrd', w, v)


def _sb_prompt(q, k, v):
    b, L = q.shape[:2]
    k_pos = jnp.arange(L)

    def block(i):
        t0 = i * Q_BLOCK
        return _stick_breaking(lax.dynamic_slice_in_dim(q, t0, Q_BLOCK, axis=1), k, v,
                               t0 + jnp.arange(Q_BLOCK), k_pos)

    out = lax.map(block, jnp.arange(L // Q_BLOCK))
    return jnp.moveaxis(out, 0, 1).reshape(b, L, C_HEADS * HEAD_DIM)


def _mla_prompt(q_nope, q_rope, k_nope, k_rope, v):
    b, L = q_nope.shape[:2]
    k_pos = jnp.arange(L)

    def block(i):
        t0 = i * Q_BLOCK
        qp = t0 + jnp.arange(Q_BLOCK)
        qn = lax.dynamic_slice_in_dim(q_nope, t0, Q_BLOCK, axis=1)
        qr = lax.dynamic_slice_in_dim(q_rope, t0, Q_BLOCK, axis=1)
        sc = (jnp.einsum('bthd,bshd->bhts', qn, k_nope)
              + jnp.einsum('bthr,bsr->bhts', qr, k_rope)).astype(jnp.float32) * MLA_SCALE
        sc = jnp.where(k_pos[None, :] <= qp[:, None], sc, -jnp.inf)
        pr = jax.nn.softmax(sc, axis=-1).astype(v.dtype)
        return jnp.einsum('bhts,bshd->bthd', pr, v)

    out = lax.map(block, jnp.arange(L // Q_BLOCK))
    return jnp.moveaxis(out, 0, 1).reshape(b, L, D_HEADS * D_V)


def _mla_latent(q_nope, q_rope, c_all, r_all, w_uk, w_uv, q_pos, k_pos):
    b, T = q_nope.shape[:2]
    q_lat = jnp.einsum('bthd,chd->bthc', q_nope, w_uk)
    sc = (jnp.einsum('bthc,bsc->bhts', q_lat, c_all)
          + jnp.einsum('bthr,bsr->bhts', q_rope, r_all)).astype(jnp.float32) * MLA_SCALE
    sc = jnp.where(k_pos[None, :] <= q_pos[:, None], sc, -jnp.inf)
    pr = jax.nn.softmax(sc, axis=-1).astype(c_all.dtype)
    o_lat = jnp.einsum('bhts,bsc->bthc', pr, c_all)
    return jnp.einsum('bthc,chd->bthd', o_lat, w_uv).reshape(b, T, D_HEADS * D_V)


def _mixer_prompt(hn, pos, p):
    b = hn.shape[0]
    u = _project(hn, pos, p)
    a_out = _dsa_prompt(u['a_q'], u['a_k'], u['a_v'], u['a_qi'], u['a_ki'], u['a_w'])
    b_out, conv_new, ssm_new = _ssd_mixer(u['b_z'], u['b_xbc'], u['b_dt'],
                                          jnp.zeros((b, CONV_W - 1, CONV_CH), hn.dtype),
                                          jnp.zeros((b, SSM_HEADS, SSM_P, D_STATE), jnp.float32), p)
    c_out = _sb_prompt(u['c_q'], u['c_k'], u['c_v'])
    k_nope = jnp.einsum('blc,chd->blhd', u['d_ckv'], p['d_w_uk'])
    v_d = jnp.einsum('blc,chd->blhd', u['d_ckv'], p['d_w_uv'])
    d_out = _mla_prompt(u['d_qn'], u['d_qr'], k_nope, u['d_kr'], v_d)
    mix = jnp.concatenate([a_out, b_out, c_out, d_out], axis=-1) @ p['w_out']
    new = (u['a_k'], u['a_v'], u['a_ki'], u['c_k'], u['c_v'], u['d_ckv'], u['d_kr'],
           conv_new, ssm_new.astype(hn.dtype))
    return mix, new


def _mixer_sample(hn, pos, p, l, caches, page_table, past_len):
    (ca_k, ca_v, ca_ki, cc_k, cc_v, cd_ckv, cd_kr, sb_conv, sb_ssm) = caches
    b, T = hn.shape[:2]
    L = past_len + T
    k_pos = jnp.arange(L)
    u = _project(hn, pos, p)
    n_sel = min(TOPK_MAX, L // 4)
    ki_all = jnp.concatenate([_gather_pages(ca_ki, l, page_table).astype(hn.dtype), u['a_ki']], axis=1)
    _, idx = lax.top_k(_indexer_scores(u['a_qi'], u['a_w'], ki_all, pos, k_pos), n_sel)
    valid = idx <= pos[None, :, None]
    flat = idx.reshape(b, -1)
    ks = _gather_rows(ca_k, l, page_table, u['a_k'], flat, past_len).reshape(b, T, n_sel, A_KV_HEADS, HEAD_DIM)
    vs = _gather_rows(ca_v, l, page_table, u['a_v'], flat, past_len).reshape(b, T, n_sel, A_KV_HEADS, HEAD_DIM)
    a_out = _sparse_attend(u['a_q'], ks, vs, valid).reshape(b, T, A_HEADS * HEAD_DIM)
    b_out, conv_new, ssm_new = _ssd_mixer(u['b_z'], u['b_xbc'], u['b_dt'], sb_conv[l], sb_ssm[l], p)
    ck_all = jnp.concatenate([_gather_pages(cc_k, l, page_table).astype(hn.dtype), u['c_k']], axis=1)
    cv_all = jnp.concatenate([_gather_pages(cc_v, l, page_table).astype(hn.dtype), u['c_v']], axis=1)
    c_out = _stick_breaking(u['c_q'], ck_all, cv_all, pos, k_pos).reshape(b, T, C_HEADS * HEAD_DIM)
    c_all = jnp.concatenate([_gather_pages(cd_ckv, l, page_table).astype(hn.dtype), u['d_ckv']], axis=1)
    r_all = jnp.concatenate([_gather_pages(cd_kr, l, page_table).astype(hn.dtype), u['d_kr']], axis=1)
    d_out = _mla_latent(u['d_qn'], u['d_qr'], c_all, r_all, p['d_w_uk'], p['d_w_uv'], pos, k_pos)
    mix = jnp.concatenate([a_out, b_out, c_out, d_out], axis=-1) @ p['w_out']
    new = (u['a_k'], u['a_v'], u['a_ki'], u['c_k'], u['c_v'], u['d_ckv'], u['d_kr'],
           conv_new.astype(sb_conv.dtype), ssm_new.astype(sb_ssm.dtype))
    return mix, new


def _mem_kv(mem, p):
    b = mem.shape[0]
    m = _rmsnorm(mem, p['g_mem_kv'])
    k = (m @ p['w_mk']).reshape(b, -1, MEM_HEADS, MEM_HD)
    v = (m @ p['w_mv']).reshape(b, -1, MEM_HEADS, MEM_HD)
    return k, v


def _post_mix(h, mem_k, mem_v, p):
    b, L, _ = h.shape
    hn = _rmsnorm(h, p['g_mem_q'])
    q = (hn @ p['w_mq']).reshape(b, L, MEM_HEADS, MEM_HD)
    sc = jnp.einsum('blhd,bmhd->bhlm', q, mem_k.astype(q.dtype)).astype(jnp.float32) * MEM_HD ** -0.5
    pr = jax.nn.softmax(sc, axis=-1).astype(q.dtype)
    o = jnp.einsum('bhlm,bmhd->blhd', pr, mem_v.astype(q.dtype)).reshape(b, L, MEM_HEADS * MEM_HD)
    h = h + o @ p['w_mo']
    hn = _rmsnorm(h, p['g_ffn'])
    return h + (jax.nn.silu(hn @ p['w_gate']) * (hn @ p['w_up'])) @ p['w_down']


def setup_inputs(seed: int = 0) -> dict:
    key = jax.random.key(seed)
    ks = iter(jax.random.split(key, 64))
    f32 = jnp.float32

    def nrm(shape, scale=1.0):
        return jax.random.normal(next(ks), shape, f32) * scale

    def gain(shape):
        return 1.0 + 0.05 * jax.random.normal(next(ks), shape, f32)

    n_pages = PAST_LEN // PAGE_SIZE
    n_used = DEC_BATCH * n_pages
    n_phys = n_used + max(1, n_used // 4)
    dt0 = jnp.exp(jax.random.uniform(next(ks), (DEPTH, SSM_HEADS), f32) * (math.log(0.1) - math.log(0.001))
                  + math.log(0.001))
    return {
        'x_prompt': nrm((BATCH, SEQ, D_MODEL)),
        'x_sample': nrm((DEC_BATCH, DEC_SEQ, D_MODEL)),
        'mem_prompt': nrm((BATCH, MEM_TOKENS, D_MODEL)),
        'cache_a_k': nrm((DEPTH, n_phys, PAGE_SIZE, A_KV_HEADS, HEAD_DIM)),
        'cache_a_v': nrm((DEPTH, n_phys, PAGE_SIZE, A_KV_HEADS, HEAD_DIM)),
        'cache_a_kidx': nrm((DEPTH, n_phys, PAGE_SIZE, IDX_DIM)),
        'cache_c_k': nrm((DEPTH, n_phys, PAGE_SIZE, C_KV_HEADS, HEAD_DIM)),
        'cache_c_v': nrm((DEPTH, n_phys, PAGE_SIZE, C_KV_HEADS, HEAD_DIM)),
        'cache_d_ckv': nrm((DEPTH, n_phys, PAGE_SIZE, KV_RANK)),
        'cache_d_krope': nrm((DEPTH, n_phys, PAGE_SIZE, D_ROPE)),
        'state_b_conv': nrm((DEPTH, DEC_BATCH, CONV_W - 1, CONV_CH)),
        'state_b_ssm': nrm((DEPTH, DEC_BATCH, SSM_HEADS, SSM_P, D_STATE), 0.1),
        'cache_mem_k': nrm((DEPTH, DEC_BATCH, MEM_TOKENS, MEM_HEADS, MEM_HD)),
        'cache_mem_v': nrm((DEPTH, DEC_BATCH, MEM_TOKENS, MEM_HEADS, MEM_HD)),
        'page_table': jax.random.permutation(next(ks), n_phys)[:n_used].reshape(DEC_BATCH, n_pages).astype(jnp.int32),
        'g_mix': gain((DEPTH, D_MODEL)),
        'w_in': nrm((DEPTH, D_MODEL, IN_TOTAL), D_MODEL ** -0.5),
        'b_conv_w': nrm((DEPTH, CONV_W, CONV_CH), CONV_W ** -0.5),
        'b_conv_b': nrm((DEPTH, CONV_CH), 0.01),
        'b_dt_bias': dt0 + jnp.log(-jnp.expm1(-dt0)),
        'b_a_log': jnp.log(jax.random.uniform(next(ks), (DEPTH, SSM_HEADS), f32, 1.0, 16.0)),
        'b_d': gain((DEPTH, SSM_HEADS)),
        'b_norm_g': gain((DEPTH, SSM_INNER)),
        'd_q_norm_g': gain((DEPTH, Q_RANK)),
        'd_kv_norm_g': gain((DEPTH, KV_RANK)),
        'd_w_uq': nrm((DEPTH, Q_RANK, D_HEADS * (D_NOPE + D_ROPE)), Q_RANK ** -0.5),
        'd_w_uk': nrm((DEPTH, KV_RANK, D_HEADS, D_NOPE), KV_RANK ** -0.5),
        'd_w_uv': nrm((DEPTH, KV_RANK, D_HEADS, D_V), KV_RANK ** -0.5),
        'w_out': nrm((DEPTH, MIX_WIDTH, D_MODEL), MIX_WIDTH ** -0.5),
        'g_mem_q': gain((DEPTH, D_MODEL)),
        'g_mem_kv': gain((DEPTH, D_MODEL)),
        'w_mq': nrm((DEPTH, D_MODEL, MEM_HEADS * MEM_HD), D_MODEL ** -0.5),
        'w_mk': nrm((DEPTH, D_MODEL, MEM_HEADS * MEM_HD), D_MODEL ** -0.5),
        'w_mv': nrm((DEPTH, D_MODEL, MEM_HEADS * MEM_HD), D_MODEL ** -0.5),
        'w_mo': nrm((DEPTH, MEM_HEADS * MEM_HD, D_MODEL), (MEM_HEADS * MEM_HD) ** -0.5),
        'g_ffn': gain((DEPTH, D_MODEL)),
        'w_gate': nrm((DEPTH, D_MODEL, D_FF), D_MODEL ** -0.5),
        'w_up': nrm((DEPTH, D_MODEL, D_FF), D_MODEL ** -0.5),
        'w_down': nrm((DEPTH, D_FF, D_MODEL), D_FF ** -0.5),
        'g_final': gain((D_MODEL,)),
    }


def reference(x_prompt, x_sample, mem_prompt,
              cache_a_k, cache_a_v, cache_a_kidx, cache_c_k, cache_c_v, cache_d_ckv, cache_d_krope,
              state_b_conv, state_b_ssm, cache_mem_k, cache_mem_v, page_table,
              g_mix, w_in, b_conv_w, b_conv_b, b_dt_bias, b_a_log, b_d, b_norm_g,
              d_q_norm_g, d_kv_norm_g, d_w_uq, d_w_uk, d_w_uv, w_out,
              g_mem_q, g_mem_kv, w_mq, w_mk, w_mv, w_mo,
              g_ffn, w_gate, w_up, w_down, g_final):
    s_prompt = x_prompt.shape[1]
    t_new = x_sample.shape[1]
    past_len = page_table.shape[1] * PAGE_SIZE
    pos_p = jnp.arange(s_prompt)
    pos_s = past_len + jnp.arange(t_new)
    caches = (cache_a_k, cache_a_v, cache_a_kidx, cache_c_k, cache_c_v, cache_d_ckv, cache_d_krope,
              state_b_conv, state_b_ssm)
    hp, hs = x_prompt, x_sample
    new_p, new_s, mem_ks, mem_vs = [], [], [], []
    for l in range(DEPTH):
        p = {'w_in': w_in[l], 'b_conv_w': b_conv_w[l], 'b_conv_b': b_conv_b[l], 'b_dt_bias': b_dt_bias[l],
             'b_a_log': b_a_log[l], 'b_d': b_d[l], 'b_norm_g': b_norm_g[l],
             'd_q_norm_g': d_q_norm_g[l], 'd_kv_norm_g': d_kv_norm_g[l], 'd_w_uq': d_w_uq[l],
             'd_w_uk': d_w_uk[l], 'd_w_uv': d_w_uv[l], 'w_out': w_out[l],
             'g_mem_q': g_mem_q[l], 'g_mem_kv': g_mem_kv[l], 'w_mq': w_mq[l], 'w_mk': w_mk[l],
             'w_mv': w_mv[l], 'w_mo': w_mo[l], 'g_ffn': g_ffn[l], 'w_gate': w_gate[l],
             'w_up': w_up[l], 'w_down': w_down[l]}
        mix_p, st_p = _mixer_prompt(_rmsnorm(hp, g_mix[l]), pos_p, p)
        mk, mv = _mem_kv(mem_prompt, p)
        hp = _post_mix(hp + mix_p, mk, mv, p)
        new_p.append(st_p)
        mem_ks.append(mk)
        mem_vs.append(mv)
        mix_s, st_s = _mixer_sample(_rmsnorm(hs, g_mix[l]), pos_s, p, l, caches, page_table, past_len)
        hs = _post_mix(hs + mix_s, cache_mem_k[l], cache_mem_v[l], p)
        new_s.append(st_s)
    y_prompt = _rmsnorm(hp, g_final)
    y_sample = _rmsnorm(hs, g_final)
    (p_a_k, p_a_v, p_a_kidx, p_c_k, p_c_v, p_d_ckv, p_d_krope, p_b_conv, p_b_ssm) = [jnp.stack(t) for t in zip(*new_p)]
    (s_a_k, s_a_v, s_a_kidx, s_c_k, s_c_v, s_d_ckv, s_d_krope, s_b_conv, s_b_ssm) = [jnp.stack(t) for t in zip(*new_s)]
    p_mem_k = jnp.stack(mem_ks)
    p_mem_v = jnp.stack(mem_vs)
    return (y_prompt, y_sample,
            p_a_k, p_a_v, p_a_kidx, p_c_k, p_c_v, p_d_ckv, p_d_krope, p_b_conv, p_b_ssm, p_mem_k, p_mem_v,
            s_a_k, s_a_v, s_a_kidx, s_c_k, s_c_v, s_d_ckv, s_d_krope, s_b_conv, s_b_ssm)
```

```python
import functools
import math
import jax
import jax.numpy as jnp
from jax import lax
import numpy as np
from jax.experimental import pallas as pl
from jax.experimental.pallas import tpu as pltpu


D_MODEL = 1024
BATCH = 32
SEQ = 2048
DEPTH = 2
DEC_BATCH = 128
DEC_SEQ = 1
PAST_LEN = 16384
PAGE_SIZE = 128

N_MIXERS = 4
GROUP_WIDTH = D_MODEL // N_MIXERS
MIX_WIDTH = N_MIXERS * GROUP_WIDTH
HEAD_DIM = 64
ROPE_THETA = 10000.0
NORM_EPS = 1e-6
Q_BLOCK = 128

A_HEADS = GROUP_WIDTH // HEAD_DIM
A_KV_HEADS = 1
IDX_HEADS = 8
IDX_DIM = 32
TOPK_MAX = 256
IDX_SCALE = (IDX_HEADS * IDX_DIM) ** -0.5

SSM_P = HEAD_DIM
SSM_HEADS = GROUP_WIDTH // SSM_P
SSM_INNER = SSM_HEADS * SSM_P
SSM_GROUPS = 2
D_STATE = 128
CONV_W = 4
CONV_CH = SSM_INNER + 2 * SSM_GROUPS * D_STATE
SSD_CHUNK = 128

C_HEADS = GROUP_WIDTH // HEAD_DIM
C_KV_HEADS = 2

D_HEADS = GROUP_WIDTH // HEAD_DIM
D_NOPE = 64
D_ROPE = 32
D_V = GROUP_WIDTH // D_HEADS
Q_RANK = 256
KV_RANK = 128
MLA_SCALE = (D_NOPE + D_ROPE) ** -0.5

MEM_TOKENS = 256
MEM_HEADS = 4
MEM_HD = 64

D_FF = -(-8 * D_MODEL // (3 * 256)) * 256

IN_SIZES = (A_HEADS * HEAD_DIM, A_KV_HEADS * HEAD_DIM, A_KV_HEADS * HEAD_DIM, IDX_HEADS * IDX_DIM, IDX_DIM, IDX_HEADS,
            SSM_INNER, CONV_CH, SSM_HEADS,
            C_HEADS * HEAD_DIM, C_KV_HEADS * HEAD_DIM, C_KV_HEADS * HEAD_DIM,
            Q_RANK, KV_RANK, D_ROPE)
IN_TOTAL = sum(IN_SIZES)


def _in_offsets():
    return [int(o) for o in np.cumsum(IN_SIZES)[:-1]]


def _rmsnorm(x, g):
    xf = x.astype(jnp.float32)
    y = xf * lax.rsqrt(jnp.mean(xf * xf, axis=-1, keepdims=True) + NORM_EPS)
    return (y * g.astype(jnp.float32)).astype(x.dtype)


def _rope(x, pos):
    half = x.shape[-1] // 2
    inv = 1.0 / (ROPE_THETA ** (jnp.arange(half, dtype=jnp.float32) / half))
    ang = pos.astype(jnp.float32)[:, None] * inv[None, :]
    ang = ang.reshape((ang.shape[0],) + (1,) * (x.ndim - 3) + (half,))
    cos, sin = jnp.cos(ang), jnp.sin(ang)
    xf = x.astype(jnp.float32)
    x1, x2 = xf[..., :half], xf[..., half:]
    return jnp.concatenate([x1 * cos - x2 * sin, x2 * cos + x1 * sin], axis=-1).astype(x.dtype)


def _gather_pages(pool, l, page_table):
    g = pool[l, page_table]
    return g.reshape((g.shape[0], g.shape[1] * g.shape[2]) + g.shape[3:])


def _gather_rows(pool, l, page_table, new, idx, past_len):
    page = jnp.clip(idx // PAGE_SIZE, 0, page_table.shape[1] - 1)
    phys = jnp.take_along_axis(page_table, page, axis=1)
    rows = pool[l, phys, idx % PAGE_SIZE]
    extra = (1,) * (new.ndim - 2)
    j = jnp.clip(idx - past_len, 0, new.shape[1] - 1)
    fresh = jnp.take_along_axis(new, j.reshape(j.shape + extra), axis=1)
    return jnp.where((idx < past_len).reshape(idx.shape + extra), rows.astype(new.dtype), fresh)


def _project(hn, pos, p):
    b, L, _ = hn.shape
    (a_q, a_k, a_v, a_qi, a_ki, a_w, b_z, b_xbc, b_dt, c_q, c_k, c_v, d_cq, d_ckv, d_kr) = jnp.split(
        hn @ p['w_in'], _in_offsets(), axis=-1)
    q_d = (_rmsnorm(d_cq, p['d_q_norm_g']) @ p['d_w_uq']).reshape(b, L, D_HEADS, D_NOPE + D_ROPE)
    return {
        'a_q': _rope(a_q.reshape(b, L, A_KV_HEADS, A_HEADS // A_KV_HEADS, HEAD_DIM), pos),
        'a_k': _rope(a_k.reshape(b, L, A_KV_HEADS, HEAD_DIM), pos),
        'a_v': a_v.reshape(b, L, A_KV_HEADS, HEAD_DIM),
        'a_qi': _rope(a_qi.reshape(b, L, IDX_HEADS, IDX_DIM), pos),
        'a_ki': _rope(a_ki, pos),
        'a_w': a_w,
        'b_z': b_z, 'b_xbc': b_xbc, 'b_dt': b_dt,
        'c_q': c_q.reshape(b, L, C_KV_HEADS, C_HEADS // C_KV_HEADS, HEAD_DIM),
        'c_k': c_k.reshape(b, L, C_KV_HEADS, HEAD_DIM),
        'c_v': c_v.reshape(b, L, C_KV_HEADS, HEAD_DIM),
        'd_qn': q_d[..., :D_NOPE],
        'd_qr': _rope(q_d[..., D_NOPE:], pos),
        'd_ckv': _rmsnorm(d_ckv, p['d_kv_norm_g']),
        'd_kr': _rope(d_kr, pos),
    }


def _indexer_scores(q_idx, w_idx, k_idx, q_pos, k_pos):
    rel = jax.nn.relu(jnp.einsum('bqhe,bse->bqhs', q_idx, k_idx).astype(jnp.float32))
    sc = jnp.einsum('bqh,bqhs->bqs', w_idx.astype(jnp.float32), rel) * IDX_SCALE
    return jnp.where(k_pos[None, None, :] <= q_pos[None, :, None], sc, -jnp.inf)


def _sparse_attend(q, k_sel, v_sel, valid):
    sc = jnp.einsum('btgrd,btkgd->btgrk', q, k_sel).astype(jnp.float32) * HEAD_DIM ** -0.5
    sc = jnp.where(valid[:, :, None, None, :], sc, -jnp.inf)
    pr = jax.nn.softmax(sc, axis=-1).astype(v_sel.dtype)
    return jnp.einsum('btgrk,btkgd->btgrd', pr, v_sel)


_BF16 = jnp.bfloat16
_F32 = jnp.float32
_INT_MIN = -2 ** 31
_NEG_BIG = -1e30
_TQ = 128
_TK = 128


def _sortable_key(x):
    u = lax.bitcast_convert_type(x, jnp.int32)
    return jnp.where(u < 0, -(u & 0x7FFFFFFF), u)


def _kth_largest_key(key_sc, nchunks, n_sel):
    def count_ge(cand):
        def body(c, cnt):
            kk = key_sc[pl.ds(pl.multiple_of(c * _TK, _TK), _TK), :]
            return cnt + jnp.where(kk >= cand, 1, 0)
        cnt = lax.fori_loop(0, nchunks, body, jnp.zeros((_TK, _TQ), jnp.int32))
        return jnp.sum(cnt, axis=0, keepdims=True)

    def bit_step(i, carry):
        t, cnt_t = carry
        cand = t ^ lax.shift_left(jnp.int32(1), 31 - i)
        cnt = count_ge(cand)
        ok = cnt >= n_sel
        return jnp.where(ok, cand, t), jnp.where(ok, cnt, cnt_t)

    t0 = jnp.full((1, _TQ), _INT_MIN, jnp.int32)
    c0 = jnp.zeros((1, _TQ), jnp.int32) + nchunks * _TK
    return lax.fori_loop(0, 32, bit_step, (t0, c0))


def _drop_late_ties(key_sc, nchunks, thr, n_sel):
    def count_gt(c, cnt):
        kk = key_sc[pl.ds(pl.multiple_of(c * _TK, _TK), _TK), :]
        return cnt + jnp.where(kk > thr, 1, 0)
    n_gt = jnp.sum(lax.fori_loop(0, nchunks, count_gt, jnp.zeros((_TK, _TQ), jnp.int32)), axis=0, keepdims=True)
    room = (n_sel - n_gt).astype(_F32)
    row = lax.broadcasted_iota(jnp.int32, (_TK, _TK), 0)
    col = lax.broadcasted_iota(jnp.int32, (_TK, _TK), 1)
    before = jnp.where(col < row, 1.0, 0.0).astype(_BF16)

    def body(c, seen):
        sl = pl.ds(pl.multiple_of(c * _TK, _TK), _TK)
        kk = key_sc[sl, :]
        eq = kk == thr
        eqf = jnp.where(eq, 1.0, 0.0)
        rank = jnp.dot(before, eqf.astype(_BF16), preferred_element_type=_F32) + seen
        key_sc[sl, :] = jnp.where(eq & (rank >= room), _INT_MIN, kk)
        return seen + jnp.sum(eqf, axis=0, keepdims=True)

    lax.fori_loop(0, nchunks, body, jnp.zeros((1, _TQ), _F32))


def _dsa_prompt_kernel(q_ref, qi_ref, wt_ref, ki_ref, k_ref, vt_ref, o_ref, key_sc, *, n_sel):
    j = pl.program_id(1)
    nchunks = j + 1
    heads, d = q_ref.shape[1], q_ref.shape[3]
    ih, e = qi_ref.shape[1], qi_ref.shape[3]
    qidx = qi_ref[0].reshape(ih * _TQ, e)
    wt = wt_ref[0]
    t_pos = j * _TQ + lax.broadcasted_iota(jnp.int32, (_TK, _TQ), 1)
    s_loc = lax.broadcasted_iota(jnp.int32, (_TK, _TQ), 0)

    def score_chunk(c, _):
        sl = pl.ds(pl.multiple_of(c * _TK, _TK), _TK)
        dots = lax.dot_general(ki_ref[0, sl, :], qidx, (((1,), (1,)), ((), ())), preferred_element_type=_F32)
        acc = jnp.zeros((_TK, _TQ), _F32)
        for h in range(ih):
            acc = acc + wt[h:h + 1, :] * jnp.maximum(dots[:, h * _TQ:(h + 1) * _TQ], 0.0)
        sc = jnp.where(c * _TK + s_loc <= t_pos, acc * IDX_SCALE, -jnp.inf)
        key_sc[sl, :] = _sortable_key(sc)
        return 0

    lax.fori_loop(0, nchunks, score_chunk, 0)
    thr, cnt_thr = _kth_largest_key(key_sc, nchunks, n_sel)

    @pl.when(jnp.max(cnt_thr) > n_sel)
    def _():
        _drop_late_ties(key_sc, nchunks, thr, n_sel)

    q_all = q_ref[0].reshape(heads * _TQ, d)
    t_pos_w = j * _TQ + lax.broadcasted_iota(jnp.int32, (_TK, _TQ), 1)

    def att_chunk(c, carry):
        m, l, acc = carry
        sl = pl.ds(pl.multiple_of(c * _TK, _TK), _TK)
        sel = (key_sc[sl, :] >= thr) & (c * _TK + s_loc <= t_pos_w)
        bias1 = jnp.where(sel, 0.0, _NEG_BIG)
        bias = jnp.concatenate([bias1] * heads, axis=1)
        logit = lax.dot_general(k_ref[0, sl, :], q_all, (((1,), (1,)), ((), ())),
                                preferred_element_type=_F32) * (d ** -0.5) + bias
        m_new = jnp.maximum(m, jnp.max(logit, axis=0, keepdims=True))
        p = jnp.exp(logit - m_new)
        alpha = jnp.exp(m - m_new)
        l = alpha * l + jnp.sum(p, axis=0, keepdims=True)
        acc = alpha * acc + jnp.dot(vt_ref[0, :, sl], p.astype(_BF16), preferred_element_type=_F32)
        return m_new, l, acc

    m0 = jnp.full((1, heads * _TQ), _NEG_BIG, _F32)
    l0 = jnp.zeros((1, heads * _TQ), _F32)
    a0 = jnp.zeros((d, heads * _TQ), _F32)
    _, l, acc = lax.fori_loop(0, nchunks, att_chunk, (m0, l0, a0))
    out_t = acc / l
    stacked = jnp.concatenate([out_t[:, h * _TQ:(h + 1) * _TQ] for h in range(heads)], axis=0)
    o_ref[0] = stacked.T


def _dsa_prompt(q, k, v, q_idx, k_idx, w_idx):
    b, L = q.shape[:2]
    heads, d = q.shape[3], q.shape[4]
    ih, e = q_idx.shape[2], q_idx.shape[3]
    n_sel = min(TOPK_MAX, L // 4)
    qh = jnp.transpose(q.reshape(b, L, heads, d), (0, 2, 1, 3)).astype(_BF16)
    qih = jnp.transpose(q_idx, (0, 2, 1, 3)).astype(_BF16)
    wt = jnp.transpose(w_idx, (0, 2, 1)).astype(_F32)
    kk = k.reshape(b, L, d).astype(_BF16)
    vt = jnp.transpose(v.reshape(b, L, d), (0, 2, 1)).astype(_BF16)
    return pl.pallas_call(
        functools.partial(_dsa_prompt_kernel, n_sel=n_sel),
        grid=(b, L // _TQ),
        in_specs=[
            pl.BlockSpec((1, heads, _TQ, d), lambda i, j: (i, 0, j, 0)),
            pl.BlockSpec((1, ih, _TQ, e), lambda i, j: (i, 0, j, 0)),
            pl.BlockSpec((1, ih, _TQ), lambda i, j: (i, 0, j)),
            pl.BlockSpec((1, L, e), lambda i, j: (i, 0, 0)),
            pl.BlockSpec((1, L, d), lambda i, j: (i, 0, 0)),
            pl.BlockSpec((1, d, L), lambda i, j: (i, 0, 0)),
        ],
        out_specs=pl.BlockSpec((1, _TQ, heads * d), lambda i, j: (i, j, 0)),
        out_shape=jax.ShapeDtypeStruct((b, L, heads * d), _F32),
        scratch_shapes=[pltpu.VMEM((L, _TQ), jnp.int32)],
        compiler_params=pltpu.CompilerParams(dimension_semantics=("parallel", "arbitrary")),
        name="dsa_prompt",
    )(qh, qih, wt, k_idx.astype(_BF16), kk, vt)


def _ssd_scan(x, dt, a, bm, cm, s0):
    b, l, h, p = x.shape
    g, n = bm.shape[2], bm.shape[3]
    r = h // g
    q = SSD_CHUNK if l % SSD_CHUNK == 0 else l
    c = l // q
    f32 = jnp.float32
    xdt = (x.astype(f32) * dt[..., None]).reshape(b, c, q, g, r, p)
    acs = jnp.cumsum((dt * a).reshape(b, c, q, g, r), axis=2)
    bc = bm.astype(f32).reshape(b, c, q, g, n)
    cc = cm.astype(f32).reshape(b, c, q, g, n)
    acs_t = jnp.moveaxis(acs, 2, -1)
    tril = jnp.tril(jnp.ones((q, q), dtype=bool))
    seg = jnp.exp(jnp.where(tril, acs_t[..., :, None] - acs_t[..., None, :], -jnp.inf))
    cb = jnp.einsum('bcign,bcjgn->bcgij', cc, bc)
    y_diag = jnp.einsum('bcgrij,bcjgrp->bcigrp', cb[:, :, :, None] * seg, xdt)
    decay_end = jnp.exp(acs[:, :, -1:] - acs)
    states = jnp.einsum('bcjgn,bcjgrp->bcgrpn', bc, xdt * decay_end[..., None])
    chunk_decay = jnp.exp(acs[:, :, -1])

    def step(s, inp):
        st, dc = inp
        return dc[..., None, None] * s + st, s

    s_fin, s_in = lax.scan(step, s0.astype(f32).reshape(b, g, r, p, n),
                           (jnp.moveaxis(states, 1, 0), jnp.moveaxis(chunk_decay, 1, 0)))
    s_in = jnp.moveaxis(s_in, 0, 1)
    y_off = jnp.einsum('bcign,bcgrpn->bcigrp', cc, s_in) * jnp.exp(acs)[..., None]
    return (y_diag + y_off).reshape(b, l, h, p), s_fin.reshape(b, h, p, n)


def _ssd_mixer(z, xbc, dt_raw, conv_prev, ssm_prev, p):
    b, T = xbc.shape[:2]
    f32 = jnp.float32
    xin = jnp.concatenate([conv_prev.astype(xbc.dtype), xbc], axis=1)
    cw = p['b_conv_w']
    conv = p['b_conv_b'] + xin[:, 0:T] * cw[0]
    for w in range(1, CONV_W):
        conv = conv + xin[:, w:w + T] * cw[w]
    xbc_c = jax.nn.silu(conv)
    xs, bm, cm = jnp.split(xbc_c, [SSM_INNER, SSM_INNER + SSM_GROUPS * D_STATE], axis=-1)
    xs = xs.reshape(b, T, SSM_HEADS, SSM_P)
    bm = bm.reshape(b, T, SSM_GROUPS, D_STATE)
    cm = cm.reshape(b, T, SSM_GROUPS, D_STATE)
    dt = jax.nn.softplus(dt_raw.astype(f32) + p['b_dt_bias'].astype(f32))
    a = -jnp.exp(p['b_a_log'].astype(f32))
    y, s_fin = _ssd_scan(xs, dt, a, bm, cm, ssm_prev)
    y = y + p['b_d'].astype(f32)[:, None] * xs.astype(f32)
    y = y.reshape(b, T, SSM_INNER) * jax.nn.silu(z.astype(f32))
    out = _rmsnorm(y, p['b_norm_g']).astype(z.dtype)
    return out, xin[:, -(CONV_W - 1):], s_fin


def _stick_breaking(q, k, v, q_pos, k_pos):
    z = jnp.einsum('btgrd,bsgd->bgrts', q, k).astype(jnp.float32) * HEAD_DIM ** -0.5
    m = k_pos[None, :] < q_pos[:, None]
    log_1m = jnp.where(m, jax.nn.log_sigmoid(-z), 0.0)
    cum = jnp.cumsum(log_1m, axis=-1)
    log_w = jax.nn.log_sigmoid(z) + cum[..., -1:] - cum
    w = jnp.where(m, jnp.exp(log_w), 0.0).astype(v.dtype)
    return jnp.einsum('bgrts,bsgd->btgrd', w, v)


def _sb_prompt(q, k, v):
    b, L = q.shape[:2]
    k_pos = jnp.arange(L)

    def block(i):
        t0 = i * Q_BLOCK
        return _stick_breaking(lax.dynamic_slice_in_dim(q, t0, Q_BLOCK, axis=1), k, v,
                               t0 + jnp.arange(Q_BLOCK), k_pos)

    out = lax.map(block, jnp.arange(L // Q_BLOCK))
    return jnp.moveaxis(out, 0, 1).reshape(b, L, C_HEADS * HEAD_DIM)


def _mla_prompt(q_nope, q_rope, k_nope, k_rope, v):
    b, L = q_nope.shape[:2]
    k_pos = jnp.arange(L)

    def block(i):
        t0 = i * Q_BLOCK
        qp = t0 + jnp.arange(Q_BLOCK)
        qn = lax.dynamic_slice_in_dim(q_nope, t0, Q_BLOCK, axis=1)
        qr = lax.dynamic_slice_in_dim(q_rope, t0, Q_BLOCK, axis=1)
        sc = (jnp.einsum('bthd,bshd->bhts', qn, k_nope)
              + jnp.einsum('bthr,bsr->bhts', qr, k_rope)).astype(jnp.float32) * MLA_SCALE
        sc = jnp.where(k_pos[None, :] <= qp[:, None], sc, -jnp.inf)
        pr = jax.nn.softmax(sc, axis=-1).astype(v.dtype)
        return jnp.einsum('bhts,bshd->bthd', pr, v)

    out = lax.map(block, jnp.arange(L // Q_BLOCK))
    return jnp.moveaxis(out, 0, 1).reshape(b, L, D_HEADS * D_V)


def _mla_latent(q_nope, q_rope, c_all, r_all, w_uk, w_uv, q_pos, k_pos):
    b, T = q_nope.shape[:2]
    q_lat = jnp.einsum('bthd,chd->bthc', q_nope, w_uk)
    sc = (jnp.einsum('bthc,bsc->bhts', q_lat, c_all)
          + jnp.einsum('bthr,bsr->bhts', q_rope, r_all)).astype(jnp.float32) * MLA_SCALE
    sc = jnp.where(k_pos[None, :] <= q_pos[:, None], sc, -jnp.inf)
    pr = jax.nn.softmax(sc, axis=-1).astype(c_all.dtype)
    o_lat = jnp.einsum('bhts,bsc->bthc', pr, c_all)
    return jnp.einsum('bthc,chd->bthd', o_lat, w_uv).reshape(b, T, D_HEADS * D_V)


def _mixer_prompt(hn, pos, p):
    b = hn.shape[0]
    u = _project(hn, pos, p)
    a_out = _dsa_prompt(u['a_q'], u['a_k'], u['a_v'], u['a_qi'], u['a_ki'], u['a_w'])
    b_out, conv_new, ssm_new = _ssd_mixer(u['b_z'], u['b_xbc'], u['b_dt'],
                                          jnp.zeros((b, CONV_W - 1, CONV_CH), hn.dtype),
                                          jnp.zeros((b, SSM_HEADS, SSM_P, D_STATE), jnp.float32), p)
    c_out = _sb_prompt(u['c_q'], u['c_k'], u['c_v'])
    k_nope = jnp.einsum('blc,chd->blhd', u['d_ckv'], p['d_w_uk'])
    v_d = jnp.einsum('blc,chd->blhd', u['d_ckv'], p['d_w_uv'])
    d_out = _mla_prompt(u['d_qn'], u['d_qr'], k_nope, u['d_kr'], v_d)
    mix = jnp.concatenate([a_out, b_out, c_out, d_out], axis=-1) @ p['w_out']
    new = (u['a_k'], u['a_v'], u['a_ki'], u['c_k'], u['c_v'], u['d_ckv'], u['d_kr'],
           conv_new, ssm_new.astype(hn.dtype))
    return mix, new


def _mixer_sample(hn, pos, p, l, caches, page_table, past_len):
    (ca_k, ca_v, ca_ki, cc_k, cc_v, cd_ckv, cd_kr, sb_conv, sb_ssm) = caches
    b, T = hn.shape[:2]
    L = past_len + T
    k_pos = jnp.arange(L)
    u = _project(hn, pos, p)
    n_sel = min(TOPK_MAX, L // 4)
    ki_all = jnp.concatenate([_gather_pages(ca_ki, l, page_table).astype(hn.dtype), u['a_ki']], axis=1)
    _, idx = lax.top_k(_indexer_scores(u['a_qi'], u['a_w'], ki_all, pos, k_pos), n_sel)
    valid = idx <= pos[None, :, None]
    flat = idx.reshape(b, -1)
    ks = _gather_rows(ca_k, l, page_table, u['a_k'], flat, past_len).reshape(b, T, n_sel, A_KV_HEADS, HEAD_DIM)
    vs = _gather_rows(ca_v, l, page_table, u['a_v'], flat, past_len).reshape(b, T, n_sel, A_KV_HEADS, HEAD_DIM)
    a_out = _sparse_attend(u['a_q'], ks, vs, valid).reshape(b, T, A_HEADS * HEAD_DIM)
    b_out, conv_new, ssm_new = _ssd_mixer(u['b_z'], u['b_xbc'], u['b_dt'], sb_conv[l], sb_ssm[l], p)
    ck_all = jnp.concatenate([_gather_pages(cc_k, l, page_table).astype(hn.dtype), u['c_k']], axis=1)
    cv_all = jnp.concatenate([_gather_pages(cc_v, l, page_table).astype(hn.dtype), u['c_v']], axis=1)
    c_out = _stick_breaking(u['c_q'], ck_all, cv_all, pos, k_pos).reshape(b, T, C_HEADS * HEAD_DIM)
    c_all = jnp.concatenate([_gather_pages(cd_ckv, l, page_table).astype(hn.dtype), u['d_ckv']], axis=1)
    r_all = jnp.concatenate([_gather_pages(cd_kr, l, page_table).astype(hn.dtype), u['d_kr']], axis=1)
    d_out = _mla_latent(u['d_qn'], u['d_qr'], c_all, r_all, p['d_w_uk'], p['d_w_uv'], pos, k_pos)
    mix = jnp.concatenate([a_out, b_out, c_out, d_out], axis=-1) @ p['w_out']
    new = (u['a_k'], u['a_v'], u['a_ki'], u['c_k'], u['c_v'], u['d_ckv'], u['d_kr'],
           conv_new.astype(sb_conv.dtype), ssm_new.astype(sb_ssm.dtype))
    return mix, new


def _mem_kv(mem, p):
    b = mem.shape[0]
    m = _rmsnorm(mem, p['g_mem_kv'])
    k = (m @ p['w_mk']).reshape(b, -1, MEM_HEADS, MEM_HD)
    v = (m @ p['w_mv']).reshape(b, -1, MEM_HEADS, MEM_HD)
    return k, v


def _post_mix(h, mem_k, mem_v, p):
    b, L, _ = h.shape
    hn = _rmsnorm(h, p['g_mem_q'])
    q = (hn @ p['w_mq']).reshape(b, L, MEM_HEADS, MEM_HD)
    sc = jnp.einsum('blhd,bmhd->bhlm', q, mem_k.astype(q.dtype)).astype(jnp.float32) * MEM_HD ** -0.5
    pr = jax.nn.softmax(sc, axis=-1).astype(q.dtype)
    o = jnp.einsum('bhlm,bmhd->blhd', pr, mem_v.astype(q.dtype)).reshape(b, L, MEM_HEADS * MEM_HD)
    h = h + o @ p['w_mo']
    hn = _rmsnorm(h, p['g_ffn'])
    return h + (jax.nn.silu(hn @ p['w_gate']) * (hn @ p['w_up'])) @ p['w_down']


def _final_norm_kernel(x_ref, g_ref, o_ref):
    x = x_ref[...]
    o_ref[...] = x * lax.rsqrt(jnp.mean(x * x, axis=-1, keepdims=True) + NORM_EPS) * g_ref[...]


def _final_norm(x, g):
    shp = x.shape
    x2 = x.reshape(-1, shp[-1])
    n = x2.shape[0]
    tm = min(n, 1024)
    out = pl.pallas_call(
        _final_norm_kernel,
        grid=(n // tm,),
        in_specs=[pl.BlockSpec((tm, shp[-1]), lambda i: (i, 0)), pl.BlockSpec((1, shp[-1]), lambda i: (0, 0))],
        out_specs=pl.BlockSpec((tm, shp[-1]), lambda i: (i, 0)),
        out_shape=jax.ShapeDtypeStruct(x2.shape, x2.dtype),
    )(x2, g.reshape(1, -1))
    return out.reshape(shp)


def kernel(x_prompt, x_sample, mem_prompt, cache_a_k, cache_a_v, cache_a_kidx, cache_c_k, cache_c_v, cache_d_ckv, cache_d_krope, state_b_conv, state_b_ssm, cache_mem_k, cache_mem_v, page_table, g_mix, w_in, b_conv_w, b_conv_b, b_dt_bias, b_a_log, b_d, b_norm_g, d_q_norm_g, d_kv_norm_g, d_w_uq, d_w_uk, d_w_uv, w_out, g_mem_q, g_mem_kv, w_mq, w_mk, w_mv, w_mo, g_ffn, w_gate, w_up, w_down, g_final):
    s_prompt = x_prompt.shape[1]
    t_new = x_sample.shape[1]
    past_len = page_table.shape[1] * PAGE_SIZE
    pos_p = jnp.arange(s_prompt)
    pos_s = past_len + jnp.arange(t_new)
    caches = (cache_a_k, cache_a_v, cache_a_kidx, cache_c_k, cache_c_v, cache_d_ckv, cache_d_krope,
              state_b_conv, state_b_ssm)
    hp, hs = x_prompt, x_sample
    new_p, new_s, mem_ks, mem_vs = [], [], [], []
    for l in range(DEPTH):
        p = {'w_in': w_in[l], 'b_conv_w': b_conv_w[l], 'b_conv_b': b_conv_b[l], 'b_dt_bias': b_dt_bias[l],
             'b_a_log': b_a_log[l], 'b_d': b_d[l], 'b_norm_g': b_norm_g[l],
             'd_q_norm_g': d_q_norm_g[l], 'd_kv_norm_g': d_kv_norm_g[l], 'd_w_uq': d_w_uq[l],
             'd_w_uk': d_w_uk[l], 'd_w_uv': d_w_uv[l], 'w_out': w_out[l],
             'g_mem_q': g_mem_q[l], 'g_mem_kv': g_mem_kv[l], 'w_mq': w_mq[l], 'w_mk': w_mk[l],
             'w_mv': w_mv[l], 'w_mo': w_mo[l], 'g_ffn': g_ffn[l], 'w_gate': w_gate[l],
             'w_up': w_up[l], 'w_down': w_down[l]}
        mix_p, st_p = _mixer_prompt(_rmsnorm(hp, g_mix[l]), pos_p, p)
        mk, mv = _mem_kv(mem_prompt, p)
        hp = _post_mix(hp + mix_p, mk, mv, p)
        new_p.append(st_p)
        mem_ks.append(mk)
        mem_vs.append(mv)
        mix_s, st_s = _mixer_sample(_rmsnorm(hs, g_mix[l]), pos_s, p, l, caches, page_table, past_len)
        hs = _post_mix(hs + mix_s, cache_mem_k[l], cache_mem_v[l], p)
        new_s.append(st_s)
    y_prompt = _final_norm(hp, g_final)
    y_sample = _final_norm(hs, g_final)
    (p_a_k, p_a_v, p_a_kidx, p_c_k, p_c_v, p_d_ckv, p_d_krope, p_b_conv, p_b_ssm) = [jnp.stack(t) for t in zip(*new_p)]
    (s_a_k, s_a_v, s_a_kidx, s_c_k, s_c_v, s_d_ckv, s_d_krope, s_b_conv, s_b_ssm) = [jnp.stack(t) for t in zip(*new_s)]
    p_mem_k = jnp.stack(mem_ks)
    p_mem_v = jnp.stack(mem_vs)
    return (y_prompt, y_sample,
            p_a_k, p_a_v, p_a_kidx, p_c_k, p_c_v, p_d_ckv, p_d_krope, p_b_conv, p_b_ssm, p_mem_k, p_mem_v,
            s_a_k, s_a_v, s_a_kidx, s_c_k, s_c_v, s_d_ckv, s_d_krope, s_b_conv, s_b_ssm)
```

```python
import functools
import math
import jax
import jax.numpy as jnp
from jax import lax
import numpy as np
from jax.experimental import pallas as pl
from jax.experimental.pallas import tpu as pltpu


D_MODEL = 1024
BATCH = 32
SEQ = 2048
DEPTH = 2
DEC_BATCH = 128
DEC_SEQ = 1
PAST_LEN = 16384
PAGE_SIZE = 128

N_MIXERS = 4
GROUP_WIDTH = D_MODEL // N_MIXERS
MIX_WIDTH = N_MIXERS * GROUP_WIDTH
HEAD_DIM = 64
ROPE_THETA = 10000.0
NORM_EPS = 1e-6
Q_BLOCK = 128

A_HEADS = GROUP_WIDTH // HEAD_DIM
A_KV_HEADS = 1
IDX_HEADS = 8
IDX_DIM = 32
TOPK_MAX = 256
IDX_SCALE = (IDX_HEADS * IDX_DIM) ** -0.5

SSM_P = HEAD_DIM
SSM_HEADS = GROUP_WIDTH // SSM_P
SSM_INNER = SSM_HEADS * SSM_P
SSM_GROUPS = 2
D_STATE = 128
CONV_W = 4
CONV_CH = SSM_INNER + 2 * SSM_GROUPS * D_STATE
SSD_CHUNK = 128

C_HEADS = GROUP_WIDTH // HEAD_DIM
C_KV_HEADS = 2

D_HEADS = GROUP_WIDTH // HEAD_DIM
D_NOPE = 64
D_ROPE = 32
D_V = GROUP_WIDTH // D_HEADS
Q_RANK = 256
KV_RANK = 128
MLA_SCALE = (D_NOPE + D_ROPE) ** -0.5

MEM_TOKENS = 256
MEM_HEADS = 4
MEM_HD = 64

D_FF = -(-8 * D_MODEL // (3 * 256)) * 256

IN_SIZES = (A_HEADS * HEAD_DIM, A_KV_HEADS * HEAD_DIM, A_KV_HEADS * HEAD_DIM, IDX_HEADS * IDX_DIM, IDX_DIM, IDX_HEADS,
            SSM_INNER, CONV_CH, SSM_HEADS,
            C_HEADS * HEAD_DIM, C_KV_HEADS * HEAD_DIM, C_KV_HEADS * HEAD_DIM,
            Q_RANK, KV_RANK, D_ROPE)
IN_TOTAL = sum(IN_SIZES)


def _in_offsets():
    return [int(o) for o in np.cumsum(IN_SIZES)[:-1]]


def _rmsnorm(x, g):
    xf = x.astype(jnp.float32)
    y = xf * lax.rsqrt(jnp.mean(xf * xf, axis=-1, keepdims=True) + NORM_EPS)
    return (y * g.astype(jnp.float32)).astype(x.dtype)


def _rope(x, pos):
    half = x.shape[-1] // 2
    inv = 1.0 / (ROPE_THETA ** (jnp.arange(half, dtype=jnp.float32) / half))
    ang = pos.astype(jnp.float32)[:, None] * inv[None, :]
    ang = ang.reshape((ang.shape[0],) + (1,) * (x.ndim - 3) + (half,))
    cos, sin = jnp.cos(ang), jnp.sin(ang)
    xf = x.astype(jnp.float32)
    x1, x2 = xf[..., :half], xf[..., half:]
    return jnp.concatenate([x1 * cos - x2 * sin, x2 * cos + x1 * sin], axis=-1).astype(x.dtype)


_LANES = 128
_VMEM_LIMIT = 48 * 1024 * 1024


def _row_tile(n, cap):
    tm = min(n, cap)
    assert n % tm == 0
    return tm


def _rms(x, g):
    return x * lax.rsqrt(jnp.mean(x * x, axis=-1, keepdims=True) + NORM_EPS) * g


def _norm_matmul_kernel(x_ref, g_ref, w_ref, o_ref):
    o_ref[...] = jnp.dot(_rms(x_ref[...], g_ref[...]).astype(_BF16), w_ref[...], preferred_element_type=_F32)


def _norm_matmul(x, g, w, tm_cap=256):
    n, k = x.shape
    m = w.shape[1]
    mp = -(-m // _LANES) * _LANES
    wb = jnp.pad(w.astype(_BF16), ((0, 0), (0, mp - m)))
    tm = _row_tile(n, tm_cap)
    out = pl.pallas_call(
        _norm_matmul_kernel,
        grid=(n // tm,),
        in_specs=[pl.BlockSpec((tm, k), lambda i: (i, 0)), pl.BlockSpec((1, k), lambda i: (0, 0)),
                  pl.BlockSpec((k, mp), lambda i: (0, 0))],
        out_specs=pl.BlockSpec((tm, mp), lambda i: (i, 0)),
        out_shape=jax.ShapeDtypeStruct((n, mp), _F32),
        compiler_params=pltpu.CompilerParams(dimension_semantics=("parallel",), vmem_limit_bytes=_VMEM_LIMIT),
        name="norm_matmul",
    )(x, g.reshape(1, k).astype(_F32), wb)
    return out[:, :m] if mp != m else out


def _matmul_residual_kernel(x_ref, w_ref, r_ref, o_ref):
    o_ref[...] = r_ref[...] + jnp.dot(x_ref[...].astype(_BF16), w_ref[...], preferred_element_type=_F32)


def _matmul_residual(x, w, res, tm_cap=256):
    n, k = x.shape
    m = w.shape[1]
    tm = _row_tile(n, tm_cap)
    return pl.pallas_call(
        _matmul_residual_kernel,
        grid=(n // tm,),
        in_specs=[pl.BlockSpec((tm, k), lambda i: (i, 0)), pl.BlockSpec((k, m), lambda i: (0, 0)),
                  pl.BlockSpec((tm, m), lambda i: (i, 0))],
        out_specs=pl.BlockSpec((tm, m), lambda i: (i, 0)),
        out_shape=jax.ShapeDtypeStruct((n, m), _F32),
        compiler_params=pltpu.CompilerParams(dimension_semantics=("parallel",), vmem_limit_bytes=_VMEM_LIMIT),
        name="matmul_residual",
    )(x, w.astype(_BF16), res)


_FF_SPLIT = 2


def _ffn_kernel(h_ref, g_ref, wg_ref, wu_ref, wd_ref, o_ref, hn_sc, acc_sc):
    k = pl.program_id(1)

    @pl.when(k == 0)
    def _():
        h = h_ref[...]
        hn_sc[...] = _rms(h, g_ref[...]).astype(_BF16)
        acc_sc[...] = h

    hn = hn_sc[...]
    gate = jnp.dot(hn, wg_ref[...], preferred_element_type=_F32)
    up = jnp.dot(hn, wu_ref[...], preferred_element_type=_F32)
    act = (gate * jax.nn.sigmoid(gate) * up).astype(_BF16)
    acc_sc[...] += jnp.dot(act, wd_ref[...], preferred_element_type=_F32)

    @pl.when(k == pl.num_programs(1) - 1)
    def _():
        o_ref[...] = acc_sc[...]


def _ffn(h, g, w_gate, w_up, w_down, tm_cap=512):
    n, dm = h.shape
    ff = w_gate.shape[1]
    tf = ff // _FF_SPLIT
    assert tf * _FF_SPLIT == ff and tf % _LANES == 0
    tm = _row_tile(n, tm_cap)
    return pl.pallas_call(
        _ffn_kernel,
        grid=(n // tm, _FF_SPLIT),
        in_specs=[pl.BlockSpec((tm, dm), lambda i, k: (i, 0)), pl.BlockSpec((1, dm), lambda i, k: (0, 0)),
                  pl.BlockSpec((dm, tf), lambda i, k: (0, k)), pl.BlockSpec((dm, tf), lambda i, k: (0, k)),
                  pl.BlockSpec((tf, dm), lambda i, k: (k, 0))],
        out_specs=pl.BlockSpec((tm, dm), lambda i, k: (i, 0)),
        out_shape=jax.ShapeDtypeStruct((n, dm), _F32),
        scratch_shapes=[pltpu.VMEM((tm, dm), _BF16), pltpu.VMEM((tm, dm), _F32)],
        compiler_params=pltpu.CompilerParams(dimension_semantics=("parallel", "arbitrary"),
                                             vmem_limit_bytes=_VMEM_LIMIT),
        name="ffn",
    )(h, g.reshape(1, dm).astype(_F32), w_gate.astype(_BF16), w_up.astype(_BF16), w_down.astype(_BF16))


def _mix_mem_kernel(h_ref, mix_ref, wout_ref, gq_ref, wmq_ref, mk_ref, mvt_ref, wmo_ref, o_ref):
    heads, hd = mk_ref.shape[1], mk_ref.shape[3]
    h1 = h_ref[...] + jnp.dot(mix_ref[...].astype(_BF16), wout_ref[...], preferred_element_type=_F32)
    q = jnp.dot(_rms(h1, gq_ref[...]).astype(_BF16), wmq_ref[...], preferred_element_type=_F32)
    outs = []
    for hh in range(heads):
        qh = q[:, hh * hd:(hh + 1) * hd].astype(_BF16)
        sc = lax.dot_general(mk_ref[0, hh], qh, (((1,), (1,)), ((), ())), preferred_element_type=_F32) * (hd ** -0.5)
        p = jnp.exp(sc - jnp.max(sc, axis=0, keepdims=True))
        o_t = jnp.dot(mvt_ref[0, hh], p.astype(_BF16), preferred_element_type=_F32)
        outs.append(o_t / jnp.sum(p, axis=0, keepdims=True))
    o = jnp.concatenate(outs, axis=0).T
    o_ref[...] = h1 + jnp.dot(o.astype(_BF16), wmo_ref[...], preferred_element_type=_F32)


def _mix_mem(h, mix, mem_k, mem_v, p, rows_per_batch, tm_cap=256):
    n, dm = h.shape
    b, mt, heads, hd = mem_k.shape
    tm = _row_tile(rows_per_batch, tm_cap)
    per = rows_per_batch // tm
    mk = jnp.transpose(mem_k, (0, 2, 1, 3)).astype(_BF16)
    mvt = jnp.transpose(mem_v, (0, 2, 3, 1)).astype(_BF16)
    const = lambda i: (0, 0)
    return pl.pallas_call(
        _mix_mem_kernel,
        grid=(n // tm,),
        in_specs=[pl.BlockSpec((tm, dm), lambda i: (i, 0)), pl.BlockSpec((tm, dm), lambda i: (i, 0)),
                  pl.BlockSpec((dm, dm), const), pl.BlockSpec((1, dm), const),
                  pl.BlockSpec((dm, heads * hd), const),
                  pl.BlockSpec((1, heads, mt, hd), lambda i: (i // per, 0, 0, 0)),
                  pl.BlockSpec((1, heads, hd, mt), lambda i: (i // per, 0, 0, 0)),
                  pl.BlockSpec((heads * hd, dm), const)],
        out_specs=pl.BlockSpec((tm, dm), lambda i: (i, 0)),
        out_shape=jax.ShapeDtypeStruct((n, dm), _F32),
        compiler_params=pltpu.CompilerParams(dimension_semantics=("parallel",), vmem_limit_bytes=_VMEM_LIMIT),
        name="mix_mem",
    )(h, mix, p['w_out'].astype(_BF16), p['g_mem_q'].reshape(1, dm).astype(_F32), p['w_mq'].astype(_BF16),
      mk, mvt, p['w_mo'].astype(_BF16))


def _gather_pages(pool, l, page_table):
    g = pool[l, page_table]
    return g.reshape((g.shape[0], g.shape[1] * g.shape[2]) + g.shape[3:])


def _gather_rows(pool, l, page_table, new, idx, past_len):
    page = jnp.clip(idx // PAGE_SIZE, 0, page_table.shape[1] - 1)
    phys = jnp.take_along_axis(page_table, page, axis=1)
    rows = pool[l, phys, idx % PAGE_SIZE]
    extra = (1,) * (new.ndim - 2)
    j = jnp.clip(idx - past_len, 0, new.shape[1] - 1)
    fresh = jnp.take_along_axis(new, j.reshape(j.shape + extra), axis=1)
    return jnp.where((idx < past_len).reshape(idx.shape + extra), rows.astype(new.dtype), fresh)


def _project(h, g_mix, pos, p):
    b, L, dm = h.shape
    u_all = _norm_matmul(h.reshape(b * L, dm), g_mix, p['w_in']).reshape(b, L, IN_TOTAL)
    (a_q, a_k, a_v, a_qi, a_ki, a_w, b_z, b_xbc, b_dt, c_q, c_k, c_v, d_cq, d_ckv, d_kr) = jnp.split(
        u_all, _in_offsets(), axis=-1)
    q_d = _norm_matmul(d_cq.reshape(b * L, Q_RANK), p['d_q_norm_g'], p['d_w_uq']).reshape(
        b, L, D_HEADS, D_NOPE + D_ROPE)
    return {
        'd_ckv_raw': d_ckv,
        'a_q': _rope(a_q.reshape(b, L, A_KV_HEADS, A_HEADS // A_KV_HEADS, HEAD_DIM), pos),
        'a_k': _rope(a_k.reshape(b, L, A_KV_HEADS, HEAD_DIM), pos),
        'a_v': a_v.reshape(b, L, A_KV_HEADS, HEAD_DIM),
        'a_qi': _rope(a_qi.reshape(b, L, IDX_HEADS, IDX_DIM), pos),
        'a_ki': _rope(a_ki, pos),
        'a_w': a_w,
        'b_z': b_z, 'b_xbc': b_xbc, 'b_dt': b_dt,
        'c_q': c_q.reshape(b, L, C_KV_HEADS, C_HEADS // C_KV_HEADS, HEAD_DIM),
        'c_k': c_k.reshape(b, L, C_KV_HEADS, HEAD_DIM),
        'c_v': c_v.reshape(b, L, C_KV_HEADS, HEAD_DIM),
        'd_qn': q_d[..., :D_NOPE],
        'd_qr': _rope(q_d[..., D_NOPE:], pos),
        'd_ckv': _rmsnorm(d_ckv, p['d_kv_norm_g']),
        'd_kr': _rope(d_kr, pos),
    }


def _indexer_scores(q_idx, w_idx, k_idx, q_pos, k_pos):
    rel = jax.nn.relu(jnp.einsum('bqhe,bse->bqhs', q_idx, k_idx).astype(jnp.float32))
    sc = jnp.einsum('bqh,bqhs->bqs', w_idx.astype(jnp.float32), rel) * IDX_SCALE
    return jnp.where(k_pos[None, None, :] <= q_pos[None, :, None], sc, -jnp.inf)


def _sparse_attend(q, k_sel, v_sel, valid):
    sc = jnp.einsum('btgrd,btkgd->btgrk', q, k_sel).astype(jnp.float32) * HEAD_DIM ** -0.5
    sc = jnp.where(valid[:, :, None, None, :], sc, -jnp.inf)
    pr = jax.nn.softmax(sc, axis=-1).astype(v_sel.dtype)
    return jnp.einsum('btgrk,btkgd->btgrd', pr, v_sel)


_BF16 = jnp.bfloat16
_F32 = jnp.float32
_INT_MIN = -2 ** 31
_NEG_BIG = -1e30
_TQ = 256
_TK = 128


def _sortable_key(x):
    u = lax.bitcast_convert_type(x, jnp.int32)
    return jnp.where(u < 0, -(u & 0x7FFFFFFF), u)


def _kth_largest_key(key_sc, nchunks, n_sel):
    def count_ge(cand):
        def body(c, cnt):
            kk = key_sc[pl.ds(pl.multiple_of(c * _TK, _TK), _TK), :]
            return cnt + jnp.where(kk >= cand, 1, 0)
        cnt = lax.fori_loop(0, nchunks, body, jnp.zeros((_TK, _TQ), jnp.int32))
        return jnp.sum(cnt, axis=0, keepdims=True)

    def bit_step(i, carry):
        t, cnt_t = carry
        cand = t ^ lax.shift_left(jnp.int32(1), 31 - i)
        cnt = count_ge(cand)
        ok = cnt >= n_sel
        return jnp.where(ok, cand, t), jnp.where(ok, cnt, cnt_t)

    t0 = jnp.full((1, _TQ), _INT_MIN, jnp.int32)
    c0 = jnp.zeros((1, _TQ), jnp.int32) + nchunks * _TK
    return lax.fori_loop(0, 32, bit_step, (t0, c0))


def _drop_late_ties(key_sc, nchunks, thr, n_sel):
    def count_gt(c, cnt):
        kk = key_sc[pl.ds(pl.multiple_of(c * _TK, _TK), _TK), :]
        return cnt + jnp.where(kk > thr, 1, 0)
    n_gt = jnp.sum(lax.fori_loop(0, nchunks, count_gt, jnp.zeros((_TK, _TQ), jnp.int32)), axis=0, keepdims=True)
    room = (n_sel - n_gt).astype(_F32)
    row = lax.broadcasted_iota(jnp.int32, (_TK, _TK), 0)
    col = lax.broadcasted_iota(jnp.int32, (_TK, _TK), 1)
    before = jnp.where(col < row, 1.0, 0.0).astype(_BF16)

    def body(c, seen):
        sl = pl.ds(pl.multiple_of(c * _TK, _TK), _TK)
        kk = key_sc[sl, :]
        eq = kk == thr
        eqf = jnp.where(eq, 1.0, 0.0)
        rank = jnp.dot(before, eqf.astype(_BF16), preferred_element_type=_F32) + seen
        key_sc[sl, :] = jnp.where(eq & (rank >= room), _INT_MIN, kk)
        return seen + jnp.sum(eqf, axis=0, keepdims=True)

    lax.fori_loop(0, nchunks, body, jnp.zeros((1, _TQ), _F32))


def _dsa_prompt_kernel(q_ref, qi_ref, wt_ref, ki_ref, k_ref, vt_ref, o_ref, key_sc, *, n_sel):
    j = pl.program_id(1)
    nchunks = (j + 1) * (_TQ // _TK)
    heads, d = q_ref.shape[1], q_ref.shape[3]
    ih, e = qi_ref.shape[1], qi_ref.shape[3]
    qidx = qi_ref[0].reshape(ih * _TQ, e)
    wt = wt_ref[0]
    t_pos = j * _TQ + lax.broadcasted_iota(jnp.int32, (_TK, _TQ), 1)
    s_loc = lax.broadcasted_iota(jnp.int32, (_TK, _TQ), 0)

    def score_chunk(c, _):
        sl = pl.ds(pl.multiple_of(c * _TK, _TK), _TK)
        dots = lax.dot_general(ki_ref[0, sl, :], qidx, (((1,), (1,)), ((), ())), preferred_element_type=_F32)
        acc = jnp.zeros((_TK, _TQ), _F32)
        for h in range(ih):
            acc = acc + wt[h:h + 1, :] * jnp.maximum(dots[:, h * _TQ:(h + 1) * _TQ], 0.0)
        sc = jnp.where(c * _TK + s_loc <= t_pos, acc * IDX_SCALE, -jnp.inf)
        key_sc[sl, :] = _sortable_key(sc)
        return 0

    lax.fori_loop(0, nchunks, score_chunk, 0)
    thr, cnt_thr = _kth_largest_key(key_sc, nchunks, n_sel)

    @pl.when(jnp.max(cnt_thr) > n_sel)
    def _():
        _drop_late_ties(key_sc, nchunks, thr, n_sel)

    q_all = q_ref[0].reshape(heads * _TQ, d)
    t_pos_w = j * _TQ + lax.broadcasted_iota(jnp.int32, (_TK, _TQ), 1)

    def att_chunk(c, carry):
        m, l, acc = carry
        sl = pl.ds(pl.multiple_of(c * _TK, _TK), _TK)
        sel = (key_sc[sl, :] >= thr) & (c * _TK + s_loc <= t_pos_w)
        bias1 = jnp.where(sel, 0.0, _NEG_BIG)
        bias = jnp.concatenate([bias1] * heads, axis=1)
        logit = lax.dot_general(k_ref[0, sl, :], q_all, (((1,), (1,)), ((), ())),
                                preferred_element_type=_F32) * (d ** -0.5) + bias
        m_new = jnp.maximum(m, jnp.max(logit, axis=0, keepdims=True))
        p = jnp.exp(logit - m_new)
        alpha = jnp.exp(m - m_new)
        l = alpha * l + jnp.sum(p, axis=0, keepdims=True)
        acc = alpha * acc + jnp.dot(vt_ref[0, :, sl], p.astype(_BF16), preferred_element_type=_F32)
        return m_new, l, acc

    m0 = jnp.full((1, heads * _TQ), _NEG_BIG, _F32)
    l0 = jnp.zeros((1, heads * _TQ), _F32)
    a0 = jnp.zeros((d, heads * _TQ), _F32)
    _, l, acc = lax.fori_loop(0, nchunks, att_chunk, (m0, l0, a0))
    out_t = acc / l
    stacked = jnp.concatenate([out_t[:, h * _TQ:(h + 1) * _TQ] for h in range(heads)], axis=0)
    o_ref[0] = stacked.T


def _dsa_prompt(q, k, v, q_idx, k_idx, w_idx):
    b, L = q.shape[:2]
    heads, d = q.shape[3], q.shape[4]
    ih, e = q_idx.shape[2], q_idx.shape[3]
    n_sel = min(TOPK_MAX, L // 4)
    qh = jnp.transpose(q.reshape(b, L, heads, d), (0, 2, 1, 3)).astype(_BF16)
    qih = jnp.transpose(q_idx, (0, 2, 1, 3)).astype(_BF16)
    wt = jnp.transpose(w_idx, (0, 2, 1)).astype(_F32)
    kk = k.reshape(b, L, d).astype(_BF16)
    vt = jnp.transpose(v.reshape(b, L, d), (0, 2, 1)).astype(_BF16)
    return pl.pallas_call(
        functools.partial(_dsa_prompt_kernel, n_sel=n_sel),
        grid=(b, L // _TQ),
        in_specs=[
            pl.BlockSpec((1, heads, _TQ, d), lambda i, j: (i, 0, j, 0)),
            pl.BlockSpec((1, ih, _TQ, e), lambda i, j: (i, 0, j, 0)),
            pl.BlockSpec((1, ih, _TQ), lambda i, j: (i, 0, j)),
            pl.BlockSpec((1, L, e), lambda i, j: (i, 0, 0)),
            pl.BlockSpec((1, L, d), lambda i, j: (i, 0, 0)),
            pl.BlockSpec((1, d, L), lambda i, j: (i, 0, 0)),
        ],
        out_specs=pl.BlockSpec((1, _TQ, heads * d), lambda i, j: (i, j, 0)),
        out_shape=jax.ShapeDtypeStruct((b, L, heads * d), _F32),
        scratch_shapes=[pltpu.VMEM((L, _TQ), jnp.int32)],
        compiler_params=pltpu.CompilerParams(dimension_semantics=("parallel", "arbitrary")),
        name="dsa_prompt",
    )(qh, qih, wt, k_idx.astype(_BF16), kk, vt)


def _ssd_scan(x, dt, a, bm, cm, s0):
    b, l, h, p = x.shape
    g, n = bm.shape[2], bm.shape[3]
    r = h // g
    q = SSD_CHUNK if l % SSD_CHUNK == 0 else l
    c = l // q
    f32 = jnp.float32
    xdt = (x.astype(f32) * dt[..., None]).reshape(b, c, q, g, r, p)
    acs = jnp.cumsum((dt * a).reshape(b, c, q, g, r), axis=2)
    bc = bm.astype(f32).reshape(b, c, q, g, n)
    cc = cm.astype(f32).reshape(b, c, q, g, n)
    acs_t = jnp.moveaxis(acs, 2, -1)
    tril = jnp.tril(jnp.ones((q, q), dtype=bool))
    seg = jnp.exp(jnp.where(tril, acs_t[..., :, None] - acs_t[..., None, :], -jnp.inf))
    cb = jnp.einsum('bcign,bcjgn->bcgij', cc, bc)
    y_diag = jnp.einsum('bcgrij,bcjgrp->bcigrp', cb[:, :, :, None] * seg, xdt)
    decay_end = jnp.exp(acs[:, :, -1:] - acs)
    states = jnp.einsum('bcjgn,bcjgrp->bcgrpn', bc, xdt * decay_end[..., None])
    chunk_decay = jnp.exp(acs[:, :, -1])

    def step(s, inp):
        st, dc = inp
        return dc[..., None, None] * s + st, s

    s_fin, s_in = lax.scan(step, s0.astype(f32).reshape(b, g, r, p, n),
                           (jnp.moveaxis(states, 1, 0), jnp.moveaxis(chunk_decay, 1, 0)))
    s_in = jnp.moveaxis(s_in, 0, 1)
    y_off = jnp.einsum('bcign,bcgrpn->bcigrp', cc, s_in) * jnp.exp(acs)[..., None]
    return (y_diag + y_off).reshape(b, l, h, p), s_fin.reshape(b, h, p, n)


def _ssd_mixer(z, xbc, dt_raw, conv_prev, ssm_prev, p):
    b, T = xbc.shape[:2]
    f32 = jnp.float32
    xin = jnp.concatenate([conv_prev.astype(xbc.dtype), xbc], axis=1)
    cw = p['b_conv_w']
    conv = p['b_conv_b'] + xin[:, 0:T] * cw[0]
    for w in range(1, CONV_W):
        conv = conv + xin[:, w:w + T] * cw[w]
    xbc_c = jax.nn.silu(conv)
    xs, bm, cm = jnp.split(xbc_c, [SSM_INNER, SSM_INNER + SSM_GROUPS * D_STATE], axis=-1)
    xs = xs.reshape(b, T, SSM_HEADS, SSM_P)
    bm = bm.reshape(b, T, SSM_GROUPS, D_STATE)
    cm = cm.reshape(b, T, SSM_GROUPS, D_STATE)
    dt = jax.nn.softplus(dt_raw.astype(f32) + p['b_dt_bias'].astype(f32))
    a = -jnp.exp(p['b_a_log'].astype(f32))
    y, s_fin = _ssd_scan(xs, dt, a, bm, cm, ssm_prev)
    y = y + p['b_d'].astype(f32)[:, None] * xs.astype(f32)
    y = y.reshape(b, T, SSM_INNER) * jax.nn.silu(z.astype(f32))
    out = _rmsnorm(y, p['b_norm_g']).astype(z.dtype)
    return out, xin[:, -(CONV_W - 1):], s_fin


def _stick_breaking(q, k, v, q_pos, k_pos):
    z = jnp.einsum('btgrd,bsgd->bgrts', q, k).astype(jnp.float32) * HEAD_DIM ** -0.5
    m = k_pos[None, :] < q_pos[:, None]
    log_1m = jnp.where(m, jax.nn.log_sigmoid(-z), 0.0)
    cum = jnp.cumsum(log_1m, axis=-1)
    log_w = jax.nn.log_sigmoid(z) + cum[..., -1:] - cum
    w = jnp.where(m, jnp.exp(log_w), 0.0).astype(v.dtype)
    return jnp.einsum('bgrts,bsgd->btgrd', w, v)


def _heads_to_rows(out_t, heads):
    stacked = jnp.concatenate([out_t[:, h * _TQ:(h + 1) * _TQ] for h in range(heads)], axis=0)
    return stacked.T


def _softplus(z):
    return jnp.maximum(z, 0.0) + jnp.log(1.0 + jnp.exp(-jnp.abs(z)))


def _sb_prompt_kernel(q_ref, k_ref, vt_ref, o_ref, *, rep):
    j = pl.program_id(1)
    heads, d = q_ref.shape[1], q_ref.shape[3]
    groups = k_ref.shape[1]
    gw = rep * _TQ
    width = heads * _TQ
    qg = [q_ref[0, g * rep:(g + 1) * rep].reshape(gw, d) for g in range(groups)]
    row = lax.broadcasted_iota(jnp.int32, (_TK, _TK), 0)
    col = lax.broadcasted_iota(jnp.int32, (_TK, _TK), 1)
    after = jnp.where(col > row, 1.0, 0.0).astype(_BF16)
    s_loc = lax.broadcasted_iota(jnp.int32, (_TK, width), 0)
    t_loc = lax.broadcasted_iota(jnp.int32, (_TK, width), 1) & (_TQ - 1)
    diag = _TQ // _TK

    def chunk(c, carry, diag_off=None):
        masked = diag_off is not None
        if masked:
            strictly_before = s_loc + diag_off < t_loc
        tail, accs = carry
        sl = pl.ds(pl.multiple_of(c * _TK, _TK), _TK)
        z = jnp.concatenate(
            [lax.dot_general(k_ref[0, g, sl, :], qg[g], (((1,), (1,)), ((), ())), preferred_element_type=_F32)
             for g in range(groups)], axis=1) * (d ** -0.5)
        sp = _softplus(z)
        if masked:
            sp = jnp.where(strictly_before, sp, 0.0)
        hi = sp.astype(_BF16)
        lo = (sp - hi.astype(_F32)).astype(_BF16)
        later = (jnp.dot(after, hi, preferred_element_type=_F32)
                 + jnp.dot(after, lo, preferred_element_type=_F32))
        w = jnp.exp(z - sp - later - tail)
        if masked:
            w = jnp.where(strictly_before, w, 0.0)
        wb = w.astype(_BF16)
        accs = tuple(accs[g] + jnp.dot(vt_ref[0, g, :, sl], wb[:, g * gw:(g + 1) * gw],
                                       preferred_element_type=_F32) for g in range(groups))
        return tail + jnp.sum(sp, axis=0, keepdims=True), accs

    init = (jnp.zeros((1, width), _F32), tuple(jnp.zeros((d, gw), _F32) for _ in range(groups)))
    carry = init
    for i in reversed(range(diag)):
        carry = chunk(j * diag + i, carry, diag_off=i * _TK)
    _, accs = lax.fori_loop(0, j * diag, lambda i, cr: chunk(j * diag - 1 - i, cr), carry)
    o_ref[0] = _heads_to_rows(jnp.concatenate(accs, axis=1), heads)


def _sb_prompt(q, k, v):
    b, L, groups, rep, d = q.shape
    heads = groups * rep
    qh = jnp.transpose(q.reshape(b, L, heads, d), (0, 2, 1, 3)).astype(_BF16)
    kg = jnp.transpose(k, (0, 2, 1, 3)).astype(_BF16)
    vt = jnp.transpose(v, (0, 2, 3, 1)).astype(_BF16)
    return pl.pallas_call(
        functools.partial(_sb_prompt_kernel, rep=rep),
        grid=(b, L // _TQ),
        in_specs=[
            pl.BlockSpec((1, heads, _TQ, d), lambda i, j: (i, 0, j, 0)),
            pl.BlockSpec((1, groups, L, d), lambda i, j: (i, 0, 0, 0)),
            pl.BlockSpec((1, groups, d, L), lambda i, j: (i, 0, 0, 0)),
        ],
        out_specs=pl.BlockSpec((1, _TQ, heads * d), lambda i, j: (i, j, 0)),
        out_shape=jax.ShapeDtypeStruct((b, L, heads * d), _F32),
        compiler_params=pltpu.CompilerParams(dimension_semantics=("parallel", "arbitrary")),
        name="sb_prompt",
    )(qh, kg, vt)


def _causal_mha_kernel(q_ref, k_ref, vt_ref, o_ref, *, scale):
    j = pl.program_id(1)
    heads = q_ref.shape[1]
    dv = vt_ref.shape[2]
    width = heads * _TQ
    s_loc = lax.broadcasted_iota(jnp.int32, (_TK, width), 0)
    t_loc = lax.broadcasted_iota(jnp.int32, (_TK, width), 1) & (_TQ - 1)
    diag = _TQ // _TK

    def chunk(c, carry, diag_off=None):
        masked = diag_off is not None
        if masked:
            visible = s_loc + diag_off <= t_loc
        m, l, accs = carry
        sl = pl.ds(pl.multiple_of(c * _TK, _TK), _TK)
        logit = jnp.concatenate(
            [lax.dot_general(k_ref[0, h, sl, :], q_ref[0, h], (((1,), (1,)), ((), ())), preferred_element_type=_F32)
             for h in range(heads)], axis=1) * scale
        if masked:
            logit = jnp.where(visible, logit, _NEG_BIG)
        m_new = jnp.maximum(m, jnp.max(logit, axis=0, keepdims=True))
        p = jnp.exp(logit - m_new)
        alpha = jnp.exp(m - m_new)
        l = alpha * l + jnp.sum(p, axis=0, keepdims=True)
        pb = p.astype(_BF16)
        accs = tuple(alpha[:, h * _TQ:(h + 1) * _TQ] * accs[h]
                     + jnp.dot(vt_ref[0, h, :, sl], pb[:, h * _TQ:(h + 1) * _TQ], preferred_element_type=_F32)
                     for h in range(heads))
        return m_new, l, accs

    init = (jnp.full((1, width), _NEG_BIG, _F32), jnp.zeros((1, width), _F32),
            tuple(jnp.zeros((dv, _TQ), _F32) for _ in range(heads)))
    carry = init
    for i in range(diag):
        carry = chunk(j * diag + i, carry, diag_off=i * _TK)
    _, l, accs = lax.fori_loop(0, j * diag, lambda i, cr: chunk(i, cr), carry)
    o_ref[0] = _heads_to_rows(jnp.concatenate(accs, axis=1) / l, heads)


def _mla_prompt(q_nope, q_rope, k_nope, k_rope, v):
    b, L, heads, _ = q_nope.shape
    dv = v.shape[-1]
    q = jnp.transpose(jnp.concatenate([q_nope, q_rope], axis=-1), (0, 2, 1, 3)).astype(_BF16)
    kr = jnp.broadcast_to(k_rope[:, :, None, :], (b, L, heads, k_rope.shape[-1]))
    k = jnp.transpose(jnp.concatenate([k_nope, kr], axis=-1), (0, 2, 1, 3)).astype(_BF16)
    vt = jnp.transpose(v, (0, 2, 3, 1)).astype(_BF16)
    dq = q.shape[-1]
    return pl.pallas_call(
        functools.partial(_causal_mha_kernel, scale=MLA_SCALE),
        grid=(b, L // _TQ),
        in_specs=[
            pl.BlockSpec((1, heads, _TQ, dq), lambda i, j: (i, 0, j, 0)),
            pl.BlockSpec((1, heads, L, dq), lambda i, j: (i, 0, 0, 0)),
            pl.BlockSpec((1, heads, dv, L), lambda i, j: (i, 0, 0, 0)),
        ],
        out_specs=pl.BlockSpec((1, _TQ, heads * dv), lambda i, j: (i, j, 0)),
        out_shape=jax.ShapeDtypeStruct((b, L, heads * dv), _F32),
        compiler_params=pltpu.CompilerParams(dimension_semantics=("parallel", "arbitrary")),
        name="mla_prompt",
    )(q, k, vt)


def _mla_latent(q_nope, q_rope, c_all, r_all, w_uk, w_uv, q_pos, k_pos):
    b, T = q_nope.shape[:2]
    q_lat = jnp.einsum('bthd,chd->bthc', q_nope, w_uk)
    sc = (jnp.einsum('bthc,bsc->bhts', q_lat, c_all)
          + jnp.einsum('bthr,bsr->bhts', q_rope, r_all)).astype(jnp.float32) * MLA_SCALE
    sc = jnp.where(k_pos[None, :] <= q_pos[:, None], sc, -jnp.inf)
    pr = jax.nn.softmax(sc, axis=-1).astype(c_all.dtype)
    o_lat = jnp.einsum('bhts,bsc->bthc', pr, c_all)
    return jnp.einsum('bthc,chd->bthd', o_lat, w_uv).reshape(b, T, D_HEADS * D_V)


def _mixer_prompt(h, g_mix, pos, p):
    b, L = h.shape[:2]
    u = _project(h, g_mix, pos, p)
    a_out = _dsa_prompt(u['a_q'], u['a_k'], u['a_v'], u['a_qi'], u['a_ki'], u['a_w'])
    b_out, conv_new, ssm_new = _ssd_mixer(u['b_z'], u['b_xbc'], u['b_dt'],
                                          jnp.zeros((b, CONV_W - 1, CONV_CH), h.dtype),
                                          jnp.zeros((b, SSM_HEADS, SSM_P, D_STATE), jnp.float32), p)
    c_out = _sb_prompt(u['c_q'], u['c_k'], u['c_v'])
    w_ukv = jnp.concatenate([p['d_w_uk'].reshape(KV_RANK, D_HEADS * D_NOPE),
                             p['d_w_uv'].reshape(KV_RANK, D_HEADS * D_V)], axis=1)
    kv_up = _norm_matmul(u['d_ckv_raw'].reshape(b * L, KV_RANK), p['d_kv_norm_g'], w_ukv)
    k_nope = kv_up[:, :D_HEADS * D_NOPE].reshape(b, L, D_HEADS, D_NOPE)
    v_d = kv_up[:, D_HEADS * D_NOPE:].reshape(b, L, D_HEADS, D_V)
    d_out = _mla_prompt(u['d_qn'], u['d_qr'], k_nope, u['d_kr'], v_d)
    mix = jnp.concatenate([a_out, b_out, c_out, d_out], axis=-1)
    new = (u['a_k'], u['a_v'], u['a_ki'], u['c_k'], u['c_v'], u['d_ckv'], u['d_kr'],
           conv_new, ssm_new.astype(h.dtype))
    return mix, new


def _mixer_sample(h, g_mix, pos, p, l, caches, page_table, past_len):
    (ca_k, ca_v, ca_ki, cc_k, cc_v, cd_ckv, cd_kr, sb_conv, sb_ssm) = caches
    hn = h
    b, T = h.shape[:2]
    L = past_len + T
    k_pos = jnp.arange(L)
    u = _project(h, g_mix, pos, p)
    n_sel = min(TOPK_MAX, L // 4)
    ki_all = jnp.concatenate([_gather_pages(ca_ki, l, page_table).astype(hn.dtype), u['a_ki']], axis=1)
    _, idx = lax.top_k(_indexer_scores(u['a_qi'], u['a_w'], ki_all, pos, k_pos), n_sel)
    valid = idx <= pos[None, :, None]
    flat = idx.reshape(b, -1)
    ks = _gather_rows(ca_k, l, page_table, u['a_k'], flat, past_len).reshape(b, T, n_sel, A_KV_HEADS, HEAD_DIM)
    vs = _gather_rows(ca_v, l, page_table, u['a_v'], flat, past_len).reshape(b, T, n_sel, A_KV_HEADS, HEAD_DIM)
    a_out = _sparse_attend(u['a_q'], ks, vs, valid).reshape(b, T, A_HEADS * HEAD_DIM)
    b_out, conv_new, ssm_new = _ssd_mixer(u['b_z'], u['b_xbc'], u['b_dt'], sb_conv[l], sb_ssm[l], p)
    ck_all = jnp.concatenate([_gather_pages(cc_k, l, page_table).astype(hn.dtype), u['c_k']], axis=1)
    cv_all = jnp.concatenate([_gather_pages(cc_v, l, page_table).astype(hn.dtype), u['c_v']], axis=1)
    c_out = _stick_breaking(u['c_q'], ck_all, cv_all, pos, k_pos).reshape(b, T, C_HEADS * HEAD_DIM)
    c_all = jnp.concatenate([_gather_pages(cd_ckv, l, page_table).astype(hn.dtype), u['d_ckv']], axis=1)
    r_all = jnp.concatenate([_gather_pages(cd_kr, l, page_table).astype(hn.dtype), u['d_kr']], axis=1)
    d_out = _mla_latent(u['d_qn'], u['d_qr'], c_all, r_all, p['d_w_uk'], p['d_w_uv'], pos, k_pos)
    mix = jnp.concatenate([a_out, b_out, c_out, d_out], axis=-1)
    new = (u['a_k'], u['a_v'], u['a_ki'], u['c_k'], u['c_v'], u['d_ckv'], u['d_kr'],
           conv_new.astype(sb_conv.dtype), ssm_new.astype(sb_ssm.dtype))
    return mix, new


def _mem_kv(mem, p):
    b, mt, dm = mem.shape
    kv = _norm_matmul(mem.reshape(b * mt, dm), p['g_mem_kv'], jnp.concatenate([p['w_mk'], p['w_mv']], axis=1))
    width = MEM_HEADS * MEM_HD
    return kv[:, :width].reshape(b, mt, MEM_HEADS, MEM_HD), kv[:, width:].reshape(b, mt, MEM_HEADS, MEM_HD)


def _post_mix_prompt(h, mix, mem_k, mem_v, p):
    b, L, dm = h.shape
    h2 = _mix_mem(h.reshape(b * L, dm), mix.reshape(b * L, dm), mem_k, mem_v, p, rows_per_batch=L)
    return _ffn(h2, p['g_ffn'], p['w_gate'], p['w_up'], p['w_down']).reshape(b, L, dm)


def _post_mix_sample(h, mix, mem_k, mem_v, p):
    b, L, dm = h.shape
    h1 = _matmul_residual(mix.reshape(b * L, dm), p['w_out'], h.reshape(b * L, dm))
    q = _norm_matmul(h1, p['g_mem_q'], p['w_mq']).reshape(b, L, MEM_HEADS, MEM_HD)
    sc = jnp.einsum('blhd,bmhd->bhlm', q, mem_k.astype(q.dtype)).astype(jnp.float32) * MEM_HD ** -0.5
    pr = jax.nn.softmax(sc, axis=-1).astype(q.dtype)
    o = jnp.einsum('bhlm,bmhd->blhd', pr, mem_v.astype(q.dtype)).reshape(b * L, MEM_HEADS * MEM_HD)
    h2 = _matmul_residual(o, p['w_mo'], h1)
    return _ffn(h2, p['g_ffn'], p['w_gate'], p['w_up'], p['w_down']).reshape(b, L, dm)


def _final_norm_kernel(x_ref, g_ref, o_ref):
    x = x_ref[...]
    o_ref[...] = x * lax.rsqrt(jnp.mean(x * x, axis=-1, keepdims=True) + NORM_EPS) * g_ref[...]


def _final_norm(x, g):
    shp = x.shape
    x2 = x.reshape(-1, shp[-1])
    n = x2.shape[0]
    tm = min(n, 1024)
    out = pl.pallas_call(
        _final_norm_kernel,
        grid=(n // tm,),
        in_specs=[pl.BlockSpec((tm, shp[-1]), lambda i: (i, 0)), pl.BlockSpec((1, shp[-1]), lambda i: (0, 0))],
        out_specs=pl.BlockSpec((tm, shp[-1]), lambda i: (i, 0)),
        out_shape=jax.ShapeDtypeStruct(x2.shape, x2.dtype),
    )(x2, g.reshape(1, -1))
    return out.reshape(shp)


def kernel(x_prompt, x_sample, mem_prompt, cache_a_k, cache_a_v, cache_a_kidx, cache_c_k, cache_c_v, cache_d_ckv, cache_d_krope, state_b_conv, state_b_ssm, cache_mem_k, cache_mem_v, page_table, g_mix, w_in, b_conv_w, b_conv_b, b_dt_bias, b_a_log, b_d, b_norm_g, d_q_norm_g, d_kv_norm_g, d_w_uq, d_w_uk, d_w_uv, w_out, g_mem_q, g_mem_kv, w_mq, w_mk, w_mv, w_mo, g_ffn, w_gate, w_up, w_down, g_final):
    s_prompt = x_prompt.shape[1]
    t_new = x_sample.shape[1]
    past_len = page_table.shape[1] * PAGE_SIZE
    pos_p = jnp.arange(s_prompt)
    pos_s = past_len + jnp.arange(t_new)
    caches = (cache_a_k, cache_a_v, cache_a_kidx, cache_c_k, cache_c_v, cache_d_ckv, cache_d_krope,
              state_b_conv, state_b_ssm)
    hp, hs = x_prompt, x_sample
    new_p, new_s, mem_ks, mem_vs = [], [], [], []
    for l in range(DEPTH):
        p = {'w_in': w_in[l], 'b_conv_w': b_conv_w[l], 'b_conv_b': b_conv_b[l], 'b_dt_bias': b_dt_bias[l],
             'b_a_log': b_a_log[l], 'b_d': b_d[l], 'b_norm_g': b_norm_g[l],
             'd_q_norm_g': d_q_norm_g[l], 'd_kv_norm_g': d_kv_norm_g[l], 'd_w_uq': d_w_uq[l],
             'd_w_uk': d_w_uk[l], 'd_w_uv': d_w_uv[l], 'w_out': w_out[l],
             'g_mem_q': g_mem_q[l], 'g_mem_kv': g_mem_kv[l], 'w_mq': w_mq[l], 'w_mk': w_mk[l],
             'w_mv': w_mv[l], 'w_mo': w_mo[l], 'g_ffn': g_ffn[l], 'w_gate': w_gate[l],
             'w_up': w_up[l], 'w_down': w_down[l]}
        mix_p, st_p = _mixer_prompt(hp, g_mix[l], pos_p, p)
        mk, mv = _mem_kv(mem_prompt, p)
        hp = _post_mix_prompt(hp, mix_p, mk, mv, p)
        new_p.append(st_p)
        mem_ks.append(mk)
        mem_vs.append(mv)
        mix_s, st_s = _mixer_sample(hs, g_mix[l], pos_s, p, l, caches, page_table, past_len)
        hs = _post_mix_sample(hs, mix_s, cache_mem_k[l], cache_mem_v[l], p)
        new_s.append(st_s)
    y_prompt = _final_norm(hp, g_final)
    y_sample = _final_norm(hs, g_final)
    (p_a_k, p_a_v, p_a_kidx, p_c_k, p_c_v, p_d_ckv, p_d_krope, p_b_conv, p_b_ssm) = [jnp.stack(t) for t in zip(*new_p)]
    (s_a_k, s_a_v, s_a_kidx, s_c_k, s_c_v, s_d_ckv, s_d_krope, s_b_conv, s_b_ssm) = [jnp.stack(t) for t in zip(*new_s)]
    p_mem_k = jnp.stack(mem_ks)
    p_mem_v = jnp.stack(mem_vs)
    return (y_prompt, y_sample,
            p_a_k, p_a_v, p_a_kidx, p_c_k, p_c_v, p_d_ckv, p_d_krope, p_b_conv, p_b_ssm, p_mem_k, p_mem_v,
            s_a_k, s_a_v, s_a_kidx, s_c_k, s_c_v, s_d_ckv, s_d_krope, s_b_conv, s_b_ssm)
```

```python
import functools
import math
import jax
import jax.numpy as jnp
from jax import lax
import numpy as np
from jax.experimental import pallas as pl
from jax.experimental.pallas import tpu as pltpu


D_MODEL = 1024
BATCH = 32
SEQ = 2048
DEPTH = 2
DEC_BATCH = 128
DEC_SEQ = 1
PAST_LEN = 16384
PAGE_SIZE = 128

N_MIXERS = 4
GROUP_WIDTH = D_MODEL // N_MIXERS
MIX_WIDTH = N_MIXERS * GROUP_WIDTH
HEAD_DIM = 64
ROPE_THETA = 10000.0
NORM_EPS = 1e-6
Q_BLOCK = 128

A_HEADS = GROUP_WIDTH // HEAD_DIM
A_KV_HEADS = 1
IDX_HEADS = 8
IDX_DIM = 32
TOPK_MAX = 256
IDX_SCALE = (IDX_HEADS * IDX_DIM) ** -0.5

SSM_P = HEAD_DIM
SSM_HEADS = GROUP_WIDTH // SSM_P
SSM_INNER = SSM_HEADS * SSM_P
SSM_GROUPS = 2
D_STATE = 128
CONV_W = 4
CONV_CH = SSM_INNER + 2 * SSM_GROUPS * D_STATE
SSD_CHUNK = 128

C_HEADS = GROUP_WIDTH // HEAD_DIM
C_KV_HEADS = 2

D_HEADS = GROUP_WIDTH // HEAD_DIM
D_NOPE = 64
D_ROPE = 32
D_V = GROUP_WIDTH // D_HEADS
Q_RANK = 256
KV_RANK = 128
MLA_SCALE = (D_NOPE + D_ROPE) ** -0.5

MEM_TOKENS = 256
MEM_HEADS = 4
MEM_HD = 64

D_FF = -(-8 * D_MODEL // (3 * 256)) * 256

IN_SIZES = (A_HEADS * HEAD_DIM, A_KV_HEADS * HEAD_DIM, A_KV_HEADS * HEAD_DIM, IDX_HEADS * IDX_DIM, IDX_DIM, IDX_HEADS,
            SSM_INNER, CONV_CH, SSM_HEADS,
            C_HEADS * HEAD_DIM, C_KV_HEADS * HEAD_DIM, C_KV_HEADS * HEAD_DIM,
            Q_RANK, KV_RANK, D_ROPE)
IN_TOTAL = sum(IN_SIZES)


def _in_offsets():
    return [int(o) for o in np.cumsum(IN_SIZES)[:-1]]


def _rmsnorm(x, g):
    xf = x.astype(jnp.float32)
    y = xf * lax.rsqrt(jnp.mean(xf * xf, axis=-1, keepdims=True) + NORM_EPS)
    return (y * g.astype(jnp.float32)).astype(x.dtype)


def _rope(x, pos):
    half = x.shape[-1] // 2
    inv = 1.0 / (ROPE_THETA ** (jnp.arange(half, dtype=jnp.float32) / half))
    ang = pos.astype(jnp.float32)[:, None] * inv[None, :]
    ang = ang.reshape((ang.shape[0],) + (1,) * (x.ndim - 3) + (half,))
    cos, sin = jnp.cos(ang), jnp.sin(ang)
    xf = x.astype(jnp.float32)
    x1, x2 = xf[..., :half], xf[..., half:]
    return jnp.concatenate([x1 * cos - x2 * sin, x2 * cos + x1 * sin], axis=-1).astype(x.dtype)


_LANES = 128
_VMEM_LIMIT = 48 * 1024 * 1024


def _row_tile(n, cap):
    tm = min(n, cap)
    assert n % tm == 0
    return tm


def _rms(x, g):
    return x * lax.rsqrt(jnp.mean(x * x, axis=-1, keepdims=True) + NORM_EPS) * g


def _norm_matmul_kernel(x_ref, g_ref, w_ref, o_ref):
    o_ref[...] = jnp.dot(_rms(x_ref[...], g_ref[...]).astype(_BF16), w_ref[...], preferred_element_type=_F32)


def _norm_matmul(x, g, w, tm_cap=256):
    n, k = x.shape
    m = w.shape[1]
    mp = -(-m // _LANES) * _LANES
    wb = jnp.pad(w.astype(_BF16), ((0, 0), (0, mp - m)))
    tm = _row_tile(n, tm_cap)
    out = pl.pallas_call(
        _norm_matmul_kernel,
        grid=(n // tm,),
        in_specs=[pl.BlockSpec((tm, k), lambda i: (i, 0)), pl.BlockSpec((1, k), lambda i: (0, 0)),
                  pl.BlockSpec((k, mp), lambda i: (0, 0))],
        out_specs=pl.BlockSpec((tm, mp), lambda i: (i, 0)),
        out_shape=jax.ShapeDtypeStruct((n, mp), _F32),
        compiler_params=pltpu.CompilerParams(dimension_semantics=("parallel",), vmem_limit_bytes=_VMEM_LIMIT),
        name="norm_matmul",
    )(x, g.reshape(1, k).astype(_F32), wb)
    return out[:, :m] if mp != m else out


def _matmul_residual_kernel(x_ref, w_ref, r_ref, o_ref):
    o_ref[...] = r_ref[...] + jnp.dot(x_ref[...].astype(_BF16), w_ref[...], preferred_element_type=_F32)


def _matmul_residual(x, w, res, tm_cap=256):
    n, k = x.shape
    m = w.shape[1]
    tm = _row_tile(n, tm_cap)
    return pl.pallas_call(
        _matmul_residual_kernel,
        grid=(n // tm,),
        in_specs=[pl.BlockSpec((tm, k), lambda i: (i, 0)), pl.BlockSpec((k, m), lambda i: (0, 0)),
                  pl.BlockSpec((tm, m), lambda i: (i, 0))],
        out_specs=pl.BlockSpec((tm, m), lambda i: (i, 0)),
        out_shape=jax.ShapeDtypeStruct((n, m), _F32),
        compiler_params=pltpu.CompilerParams(dimension_semantics=("parallel",), vmem_limit_bytes=_VMEM_LIMIT),
        name="matmul_residual",
    )(x, w.astype(_BF16), res)


_FF_SPLIT = 2


def _ffn_kernel(h_ref, g_ref, wg_ref, wu_ref, wd_ref, o_ref, hn_sc, acc_sc):
    k = pl.program_id(1)

    @pl.when(k == 0)
    def _():
        h = h_ref[...]
        hn_sc[...] = _rms(h, g_ref[...]).astype(_BF16)
        acc_sc[...] = h

    hn = hn_sc[...]
    gate = jnp.dot(hn, wg_ref[...], preferred_element_type=_F32)
    up = jnp.dot(hn, wu_ref[...], preferred_element_type=_F32)
    act = (gate * jax.nn.sigmoid(gate) * up).astype(_BF16)
    acc_sc[...] += jnp.dot(act, wd_ref[...], preferred_element_type=_F32)

    @pl.when(k == pl.num_programs(1) - 1)
    def _():
        o_ref[...] = acc_sc[...]


def _ffn(h, g, w_gate, w_up, w_down, tm_cap=512):
    n, dm = h.shape
    ff = w_gate.shape[1]
    tf = ff // _FF_SPLIT
    assert tf * _FF_SPLIT == ff and tf % _LANES == 0
    tm = _row_tile(n, tm_cap)
    return pl.pallas_call(
        _ffn_kernel,
        grid=(n // tm, _FF_SPLIT),
        in_specs=[pl.BlockSpec((tm, dm), lambda i, k: (i, 0)), pl.BlockSpec((1, dm), lambda i, k: (0, 0)),
                  pl.BlockSpec((dm, tf), lambda i, k: (0, k)), pl.BlockSpec((dm, tf), lambda i, k: (0, k)),
                  pl.BlockSpec((tf, dm), lambda i, k: (k, 0))],
        out_specs=pl.BlockSpec((tm, dm), lambda i, k: (i, 0)),
        out_shape=jax.ShapeDtypeStruct((n, dm), _F32),
        scratch_shapes=[pltpu.VMEM((tm, dm), _BF16), pltpu.VMEM((tm, dm), _F32)],
        compiler_params=pltpu.CompilerParams(dimension_semantics=("parallel", "arbitrary"),
                                             vmem_limit_bytes=_VMEM_LIMIT),
        name="ffn",
    )(h, g.reshape(1, dm).astype(_F32), w_gate.astype(_BF16), w_up.astype(_BF16), w_down.astype(_BF16))


def _mix_mem_kernel(h_ref, mix_ref, wout_ref, gq_ref, wmq_ref, mk_ref, mvt_ref, wmo_ref, o_ref):
    heads, hd = mk_ref.shape[1], mk_ref.shape[3]
    h1 = h_ref[...] + jnp.dot(mix_ref[...].astype(_BF16), wout_ref[...], preferred_element_type=_F32)
    q = jnp.dot(_rms(h1, gq_ref[...]).astype(_BF16), wmq_ref[...], preferred_element_type=_F32)
    outs = []
    for hh in range(heads):
        qh = q[:, hh * hd:(hh + 1) * hd].astype(_BF16)
        sc = lax.dot_general(mk_ref[0, hh], qh, (((1,), (1,)), ((), ())), preferred_element_type=_F32) * (hd ** -0.5)
        p = jnp.exp(sc - jnp.max(sc, axis=0, keepdims=True))
        o_t = jnp.dot(mvt_ref[0, hh], p.astype(_BF16), preferred_element_type=_F32)
        outs.append(o_t / jnp.sum(p, axis=0, keepdims=True))
    o = jnp.concatenate(outs, axis=0).T
    o_ref[...] = h1 + jnp.dot(o.astype(_BF16), wmo_ref[...], preferred_element_type=_F32)


def _mix_mem(h, mix, mem_k, mem_v, p, rows_per_batch, tm_cap=256):
    n, dm = h.shape
    b, mt, heads, hd = mem_k.shape
    tm = _row_tile(rows_per_batch, tm_cap)
    per = rows_per_batch // tm
    mk = jnp.transpose(mem_k, (0, 2, 1, 3)).astype(_BF16)
    mvt = jnp.transpose(mem_v, (0, 2, 3, 1)).astype(_BF16)
    const = lambda i: (0, 0)
    return pl.pallas_call(
        _mix_mem_kernel,
        grid=(n // tm,),
        in_specs=[pl.BlockSpec((tm, dm), lambda i: (i, 0)), pl.BlockSpec((tm, dm), lambda i: (i, 0)),
                  pl.BlockSpec((dm, dm), const), pl.BlockSpec((1, dm), const),
                  pl.BlockSpec((dm, heads * hd), const),
                  pl.BlockSpec((1, heads, mt, hd), lambda i: (i // per, 0, 0, 0)),
                  pl.BlockSpec((1, heads, hd, mt), lambda i: (i // per, 0, 0, 0)),
                  pl.BlockSpec((heads * hd, dm), const)],
        out_specs=pl.BlockSpec((tm, dm), lambda i: (i, 0)),
        out_shape=jax.ShapeDtypeStruct((n, dm), _F32),
        compiler_params=pltpu.CompilerParams(dimension_semantics=("parallel",), vmem_limit_bytes=_VMEM_LIMIT),
        name="mix_mem",
    )(h, mix, p['w_out'].astype(_BF16), p['g_mem_q'].reshape(1, dm).astype(_F32), p['w_mq'].astype(_BF16),
      mk, mvt, p['w_mo'].astype(_BF16))


def _gather_pages(pool, l, page_table):
    g = pool[l, page_table]
    return g.reshape((g.shape[0], g.shape[1] * g.shape[2]) + g.shape[3:])


def _gather_rows(pool, l, page_table, new, idx, past_len):
    page = jnp.clip(idx // PAGE_SIZE, 0, page_table.shape[1] - 1)
    phys = jnp.take_along_axis(page_table, page, axis=1)
    rows = pool[l, phys, idx % PAGE_SIZE]
    extra = (1,) * (new.ndim - 2)
    j = jnp.clip(idx - past_len, 0, new.shape[1] - 1)
    fresh = jnp.take_along_axis(new, j.reshape(j.shape + extra), axis=1)
    return jnp.where((idx < past_len).reshape(idx.shape + extra), rows.astype(new.dtype), fresh)


def _norm_matmul_multi_kernel(x_ref, g_ref, w_ref, *o_refs):
    y = jnp.dot(_rms(x_ref[...], g_ref[...]).astype(_BF16), w_ref[...], preferred_element_type=_F32)
    off = 0
    for o_ref in o_refs:
        o_ref[...] = y[:, off:off + o_ref.shape[1]]
        off += o_ref.shape[1]


def _norm_matmul_multi(x, g, w_groups, tm_cap=256):
    n, k = x.shape
    pads = [-(-w.shape[1] // _LANES) * _LANES for w in w_groups]
    wb = jnp.concatenate([jnp.pad(w.astype(_BF16), ((0, 0), (0, mp - w.shape[1]))) for w, mp in zip(w_groups, pads)],
                         axis=1)
    tm = _row_tile(n, tm_cap)
    return pl.pallas_call(
        _norm_matmul_multi_kernel,
        grid=(n // tm,),
        in_specs=[pl.BlockSpec((tm, k), lambda i: (i, 0)), pl.BlockSpec((1, k), lambda i: (0, 0)),
                  pl.BlockSpec((k, sum(pads)), lambda i: (0, 0))],
        out_specs=[pl.BlockSpec((tm, mp), lambda i: (i, 0)) for mp in pads],
        out_shape=[jax.ShapeDtypeStruct((n, mp), _F32) for mp in pads],
        compiler_params=pltpu.CompilerParams(dimension_semantics=("parallel",), vmem_limit_bytes=_VMEM_LIMIT),
        name="norm_matmul_multi",
    )(x, g.reshape(1, k).astype(_F32), wb)


def _project(h, g_mix, pos, p):
    b, L, dm = h.shape
    cols = jnp.split(p['w_in'], _in_offsets(), axis=1)
    cat = lambda ids: jnp.concatenate([cols[i] for i in ids], axis=1)
    outs = _norm_matmul_multi(h.reshape(b * L, dm), g_mix,
                              [cols[0], cols[3], cols[6], cols[7], cols[9], cols[10], cols[11], cols[12], cols[13],
                               cat([1, 2]), cat([4, 14, 5, 8])])
    a_q, a_qi, b_z, b_xbc, c_q, c_k, c_v, d_cq, d_ckv, kv_a, small = [o.reshape(b, L, -1) for o in outs]
    a_k, a_v = kv_a[..., :HEAD_DIM], kv_a[..., HEAD_DIM:2 * HEAD_DIM]
    o1, o2, o3 = IDX_DIM, IDX_DIM + D_ROPE, IDX_DIM + D_ROPE + IDX_HEADS
    a_ki, d_kr, a_w, b_dt = small[..., :o1], small[..., o1:o2], small[..., o2:o3], small[..., o3:o3 + SSM_HEADS]
    q_d = _norm_matmul(d_cq.reshape(b * L, Q_RANK), p['d_q_norm_g'], p['d_w_uq']).reshape(
        b, L, D_HEADS, D_NOPE + D_ROPE)
    return {
        'd_ckv_raw': d_ckv,
        'a_q': _rope(a_q.reshape(b, L, A_KV_HEADS, A_HEADS // A_KV_HEADS, HEAD_DIM), pos),
        'a_k': _rope(a_k.reshape(b, L, A_KV_HEADS, HEAD_DIM), pos),
        'a_v': a_v.reshape(b, L, A_KV_HEADS, HEAD_DIM),
        'a_qi': _rope(a_qi.reshape(b, L, IDX_HEADS, IDX_DIM), pos),
        'a_ki': _rope(a_ki, pos),
        'a_w': a_w,
        'b_z': b_z, 'b_xbc': b_xbc, 'b_dt': b_dt,
        'c_q': c_q.reshape(b, L, C_KV_HEADS, C_HEADS // C_KV_HEADS, HEAD_DIM),
        'c_k': c_k.reshape(b, L, C_KV_HEADS, HEAD_DIM),
        'c_v': c_v.reshape(b, L, C_KV_HEADS, HEAD_DIM),
        'd_qn': q_d[..., :D_NOPE],
        'd_qr': _rope(q_d[..., D_NOPE:], pos),
        'd_ckv': _rmsnorm(d_ckv, p['d_kv_norm_g']),
        'd_kr': _rope(d_kr, pos),
    }


def _indexer_scores(q_idx, w_idx, k_idx, q_pos, k_pos):
    rel = jax.nn.relu(jnp.einsum('bqhe,bse->bqhs', q_idx, k_idx).astype(jnp.float32))
    sc = jnp.einsum('bqh,bqhs->bqs', w_idx.astype(jnp.float32), rel) * IDX_SCALE
    return jnp.where(k_pos[None, None, :] <= q_pos[None, :, None], sc, -jnp.inf)


def _sparse_attend(q, k_sel, v_sel, valid):
    sc = jnp.einsum('btgrd,btkgd->btgrk', q, k_sel).astype(jnp.float32) * HEAD_DIM ** -0.5
    sc = jnp.where(valid[:, :, None, None, :], sc, -jnp.inf)
    pr = jax.nn.softmax(sc, axis=-1).astype(v_sel.dtype)
    return jnp.einsum('btgrk,btkgd->btgrd', pr, v_sel)


_BF16 = jnp.bfloat16
_F32 = jnp.float32
_INT_MIN = -2 ** 31
_NEG_BIG = -1e30
_TQ = 256
_TK = 128


def _sortable_key(x):
    u = lax.bitcast_convert_type(x, jnp.int32)
    return jnp.where(u < 0, -(u & 0x7FFFFFFF), u)


def _kth_largest_key(key_sc, nchunks, n_sel):
    def count_ge(cand):
        def body(c, cnt):
            kk = key_sc[pl.ds(pl.multiple_of(c * _TK, _TK), _TK), :]
            return cnt + jnp.where(kk >= cand, 1, 0)
        cnt = lax.fori_loop(0, nchunks, body, jnp.zeros((_TK, _TQ), jnp.int32))
        return jnp.sum(cnt, axis=0, keepdims=True)

    def bit_step(i, carry):
        t, cnt_t = carry
        cand = t ^ lax.shift_left(jnp.int32(1), 31 - i)
        cnt = count_ge(cand)
        ok = cnt >= n_sel
        return jnp.where(ok, cand, t), jnp.where(ok, cnt, cnt_t)

    t0 = jnp.full((1, _TQ), _INT_MIN, jnp.int32)
    c0 = jnp.zeros((1, _TQ), jnp.int32) + nchunks * _TK
    return lax.fori_loop(0, 32, bit_step, (t0, c0))


def _drop_late_ties(key_sc, nchunks, thr, n_sel):
    def count_gt(c, cnt):
        kk = key_sc[pl.ds(pl.multiple_of(c * _TK, _TK), _TK), :]
        return cnt + jnp.where(kk > thr, 1, 0)
    n_gt = jnp.sum(lax.fori_loop(0, nchunks, count_gt, jnp.zeros((_TK, _TQ), jnp.int32)), axis=0, keepdims=True)
    room = (n_sel - n_gt).astype(_F32)
    row = lax.broadcasted_iota(jnp.int32, (_TK, _TK), 0)
    col = lax.broadcasted_iota(jnp.int32, (_TK, _TK), 1)
    before = jnp.where(col < row, 1.0, 0.0).astype(_BF16)

    def body(c, seen):
        sl = pl.ds(pl.multiple_of(c * _TK, _TK), _TK)
        kk = key_sc[sl, :]
        eq = kk == thr
        eqf = jnp.where(eq, 1.0, 0.0)
        rank = jnp.dot(before, eqf.astype(_BF16), preferred_element_type=_F32) + seen
        key_sc[sl, :] = jnp.where(eq & (rank >= room), _INT_MIN, kk)
        return seen + jnp.sum(eqf, axis=0, keepdims=True)

    lax.fori_loop(0, nchunks, body, jnp.zeros((1, _TQ), _F32))


def _dsa_prompt_kernel(q_ref, qi_ref, wt_ref, ki_ref, k_ref, vt_ref, o_ref, key_sc, *, n_sel):
    j = pl.program_id(1)
    nchunks = (j + 1) * (_TQ // _TK)
    heads, d = q_ref.shape[1], q_ref.shape[3]
    ih, e = qi_ref.shape[1], qi_ref.shape[3]
    qidx = qi_ref[0].reshape(ih * _TQ, e)
    wt = wt_ref[0]
    t_pos = j * _TQ + lax.broadcasted_iota(jnp.int32, (_TK, _TQ), 1)
    s_loc = lax.broadcasted_iota(jnp.int32, (_TK, _TQ), 0)

    def score_chunk(c, _):
        sl = pl.ds(pl.multiple_of(c * _TK, _TK), _TK)
        dots = lax.dot_general(ki_ref[0, sl, :], qidx, (((1,), (1,)), ((), ())), preferred_element_type=_F32)
        acc = jnp.zeros((_TK, _TQ), _F32)
        for h in range(ih):
            acc = acc + wt[h:h + 1, :] * jnp.maximum(dots[:, h * _TQ:(h + 1) * _TQ], 0.0)
        sc = jnp.where(c * _TK + s_loc <= t_pos, acc * IDX_SCALE, -jnp.inf)
        key_sc[sl, :] = _sortable_key(sc)
        return 0

    lax.fori_loop(0, nchunks, score_chunk, 0)
    thr, cnt_thr = _kth_largest_key(key_sc, nchunks, n_sel)

    @pl.when(jnp.max(cnt_thr) > n_sel)
    def _():
        _drop_late_ties(key_sc, nchunks, thr, n_sel)

    q_all = q_ref[0].reshape(heads * _TQ, d)
    t_pos_w = j * _TQ + lax.broadcasted_iota(jnp.int32, (_TK, _TQ), 1)

    def att_chunk(c, carry):
        m, l, acc = carry
        sl = pl.ds(pl.multiple_of(c * _TK, _TK), _TK)
        sel = (key_sc[sl, :] >= thr) & (c * _TK + s_loc <= t_pos_w)
        bias1 = jnp.where(sel, 0.0, _NEG_BIG)
        bias = jnp.concatenate([bias1] * heads, axis=1)
        logit = lax.dot_general(k_ref[0, sl, :], q_all, (((1,), (1,)), ((), ())),
                                preferred_element_type=_F32) * (d ** -0.5) + bias
        m_new = jnp.maximum(m, jnp.max(logit, axis=0, keepdims=True))
        p = jnp.exp(logit - m_new)
        alpha = jnp.exp(m - m_new)
        l = alpha * l + jnp.sum(p, axis=0, keepdims=True)
        acc = alpha * acc + jnp.dot(vt_ref[0, :, sl], p.astype(_BF16), preferred_element_type=_F32)
        return m_new, l, acc

    m0 = jnp.full((1, heads * _TQ), _NEG_BIG, _F32)
    l0 = jnp.zeros((1, heads * _TQ), _F32)
    a0 = jnp.zeros((d, heads * _TQ), _F32)
    _, l, acc = lax.fori_loop(0, nchunks, att_chunk, (m0, l0, a0))
    out_t = acc / l
    stacked = jnp.concatenate([out_t[:, h * _TQ:(h + 1) * _TQ] for h in range(heads)], axis=0)
    o_ref[0] = stacked.T


def _dsa_prompt(q, k, v, q_idx, k_idx, w_idx):
    b, L = q.shape[:2]
    heads, d = q.shape[3], q.shape[4]
    ih, e = q_idx.shape[2], q_idx.shape[3]
    n_sel = min(TOPK_MAX, L // 4)
    qh = jnp.transpose(q.reshape(b, L, heads, d), (0, 2, 1, 3)).astype(_BF16)
    qih = jnp.transpose(q_idx, (0, 2, 1, 3)).astype(_BF16)
    wt = jnp.transpose(w_idx, (0, 2, 1)).astype(_F32)
    kk = k.reshape(b, L, d).astype(_BF16)
    vt = jnp.transpose(v.reshape(b, L, d), (0, 2, 1)).astype(_BF16)
    return pl.pallas_call(
        functools.partial(_dsa_prompt_kernel, n_sel=n_sel),
        grid=(b, L // _TQ),
        in_specs=[
            pl.BlockSpec((1, heads, _TQ, d), lambda i, j: (i, 0, j, 0)),
            pl.BlockSpec((1, ih, _TQ, e), lambda i, j: (i, 0, j, 0)),
            pl.BlockSpec((1, ih, _TQ), lambda i, j: (i, 0, j)),
            pl.BlockSpec((1, L, e), lambda i, j: (i, 0, 0)),
            pl.BlockSpec((1, L, d), lambda i, j: (i, 0, 0)),
            pl.BlockSpec((1, d, L), lambda i, j: (i, 0, 0)),
        ],
        out_specs=pl.BlockSpec((1, _TQ, heads * d), lambda i, j: (i, j, 0)),
        out_shape=jax.ShapeDtypeStruct((b, L, heads * d), _F32),
        scratch_shapes=[pltpu.VMEM((L, _TQ), jnp.int32)],
        compiler_params=pltpu.CompilerParams(dimension_semantics=("parallel", "arbitrary")),
        name="dsa_prompt",
    )(qh, qih, wt, k_idx.astype(_BF16), kk, vt)


def _ssd_scan(x, dt, a, bm, cm, s0):
    b, l, h, p = x.shape
    g, n = bm.shape[2], bm.shape[3]
    r = h // g
    q = SSD_CHUNK if l % SSD_CHUNK == 0 else l
    c = l // q
    f32 = jnp.float32
    xdt = (x.astype(f32) * dt[..., None]).reshape(b, c, q, g, r, p)
    acs = jnp.cumsum((dt * a).reshape(b, c, q, g, r), axis=2)
    bc = bm.astype(f32).reshape(b, c, q, g, n)
    cc = cm.astype(f32).reshape(b, c, q, g, n)
    acs_t = jnp.moveaxis(acs, 2, -1)
    tril = jnp.tril(jnp.ones((q, q), dtype=bool))
    seg = jnp.exp(jnp.where(tril, acs_t[..., :, None] - acs_t[..., None, :], -jnp.inf))
    cb = jnp.einsum('bcign,bcjgn->bcgij', cc, bc)
    y_diag = jnp.einsum('bcgrij,bcjgrp->bcigrp', cb[:, :, :, None] * seg, xdt)
    decay_end = jnp.exp(acs[:, :, -1:] - acs)
    states = jnp.einsum('bcjgn,bcjgrp->bcgrpn', bc, xdt * decay_end[..., None])
    chunk_decay = jnp.exp(acs[:, :, -1])

    def step(s, inp):
        st, dc = inp
        return dc[..., None, None] * s + st, s

    s_fin, s_in = lax.scan(step, s0.astype(f32).reshape(b, g, r, p, n),
                           (jnp.moveaxis(states, 1, 0), jnp.moveaxis(chunk_decay, 1, 0)))
    s_in = jnp.moveaxis(s_in, 0, 1)
    y_off = jnp.einsum('bcign,bcgrpn->bcigrp', cc, s_in) * jnp.exp(acs)[..., None]
    return (y_diag + y_off).reshape(b, l, h, p), s_fin.reshape(b, h, p, n)


def _ssd_mixer(z, xbc, dt_raw, conv_prev, ssm_prev, p):
    b, T = xbc.shape[:2]
    f32 = jnp.float32
    xin = jnp.concatenate([conv_prev.astype(xbc.dtype), xbc], axis=1)
    cw = p['b_conv_w']
    conv = p['b_conv_b'] + xin[:, 0:T] * cw[0]
    for w in range(1, CONV_W):
        conv = conv + xin[:, w:w + T] * cw[w]
    xbc_c = jax.nn.silu(conv)
    xs, bm, cm = jnp.split(xbc_c, [SSM_INNER, SSM_INNER + SSM_GROUPS * D_STATE], axis=-1)
    xs = xs.reshape(b, T, SSM_HEADS, SSM_P)
    bm = bm.reshape(b, T, SSM_GROUPS, D_STATE)
    cm = cm.reshape(b, T, SSM_GROUPS, D_STATE)
    dt = jax.nn.softplus(dt_raw.astype(f32) + p['b_dt_bias'].astype(f32))
    a = -jnp.exp(p['b_a_log'].astype(f32))
    y, s_fin = _ssd_scan(xs, dt, a, bm, cm, ssm_prev)
    y = y + p['b_d'].astype(f32)[:, None] * xs.astype(f32)
    y = y.reshape(b, T, SSM_INNER) * jax.nn.silu(z.astype(f32))
    out = _rmsnorm(y, p['b_norm_g']).astype(z.dtype)
    return out, xin[:, -(CONV_W - 1):], s_fin


def _split_dot(a, b_mat, a_is_exact):
    x = b_mat if a_is_exact else a
    hi = x.astype(_BF16)
    lo = (x - hi.astype(_F32)).astype(_BF16)
    if a_is_exact:
        return jnp.dot(a, hi, preferred_element_type=_F32) + jnp.dot(a, lo, preferred_element_type=_F32)
    return jnp.dot(hi, b_mat, preferred_element_type=_F32) + jnp.dot(lo, b_mat, preferred_element_type=_F32)


def _ssd_prompt_kernel(xbc_ref, z_ref, dtc_ref, dtr_ref, cw_ref, cb_ref, hc_ref, hr_ref, ng_ref,
                       y_ref, conv_ref, st_ref, buf_sc, st_sc):
    c = pl.program_id(1)
    q = xbc_ref.shape[1]
    heads = SSM_HEADS
    inner = z_ref.shape[2]
    hp = inner // heads
    n = (xbc_ref.shape[2] - inner) // (2 * SSM_GROUPS)
    rep = heads // SSM_GROUPS
    lead = _SUB - (CONV_W - 1)

    @pl.when(c == 0)
    def _():
        buf_sc[0:_SUB, :] = jnp.zeros((_SUB, buf_sc.shape[1]), _F32)
        st_sc[...] = jnp.zeros_like(st_sc)

    xbc = xbc_ref[0]
    buf_sc[_SUB:_SUB + q, :] = xbc
    conv = cb_ref[...] + buf_sc[lead:lead + q, :] * cw_ref[0:1, :]
    for w in range(1, CONV_W):
        conv = conv + buf_sc[lead + w:lead + w + q, :] * cw_ref[w:w + 1, :]
    act = conv * jax.nn.sigmoid(conv)
    tail = buf_sc[q + lead:q + _SUB, :]
    buf_sc[lead:_SUB, :] = tail

    dt_c = _softplus(dtc_ref[0] + hc_ref[0:1, :])
    dt_r = _softplus(dtr_ref[0] + hr_ref[:, 0:1])
    da_c = dt_c * -jnp.exp(hc_ref[1:2, :])
    da_r = dt_r * -jnp.exp(hr_ref[:, 1:2])
    ri = lax.broadcasted_iota(jnp.int32, (q, q), 0)
    ci = lax.broadcasted_iota(jnp.int32, (q, q), 1)
    causal = ci <= ri
    acs_c = _split_dot(jnp.where(causal, 1.0, 0.0).astype(_BF16), da_c, True)
    acs_r = _split_dot(da_r, jnp.where(ri <= ci, 1.0, 0.0).astype(_BF16), False)
    acs_end = acs_c[q - 1:q, :]
    decay_end = jnp.exp(acs_end - acs_c)
    grow = jnp.exp(acs_c)

    ys = []
    for hh in range(heads):
        g = hh // rep
        xs = act[:, hh * hp:(hh + 1) * hp]
        bm = act[:, inner + g * n:inner + (g + 1) * n]
        cm = act[:, inner + SSM_GROUPS * n + g * n:inner + SSM_GROUPS * n + (g + 1) * n]
        xdt = xs * dt_c[:, hh:hh + 1]
        cbm = lax.dot_general(cm.astype(_BF16), bm.astype(_BF16), _NT, preferred_element_type=_F32)
        seg = jnp.exp(jnp.where(causal, acs_c[:, hh:hh + 1] - acs_r[hh:hh + 1, :], -jnp.inf))
        y = jnp.dot((cbm * seg).astype(_BF16), xdt.astype(_BF16), preferred_element_type=_F32)
        s_in = st_sc[hh]
        y = y + jnp.dot(cm.astype(_BF16), s_in.astype(_BF16), preferred_element_type=_F32) * grow[:, hh:hh + 1]
        new = jnp.dot(bm.T.astype(_BF16), (xdt * decay_end[:, hh:hh + 1]).astype(_BF16), preferred_element_type=_F32)
        st_sc[hh] = jnp.exp(acs_end[:, hh:hh + 1]) * s_in + new
        ys.append(y + hr_ref[hh:hh + 1, 2:3] * xs)
    y = jnp.concatenate(ys, axis=1)
    zz = z_ref[0]
    y_ref[0] = _rms(y * (zz * jax.nn.sigmoid(zz)), ng_ref[...])

    @pl.when(c == pl.num_programs(1) - 1)
    def _():
        conv_ref[0] = tail
        st_ref[0] = st_sc[...]


def _ssd_prompt(z, xbc, dt_raw, p):
    b, L, inner = z.shape
    heads = dt_raw.shape[-1]
    ch = xbc.shape[-1]
    q = SSD_CHUNK
    assert L % q == 0
    assert heads == SSM_HEADS and heads <= _SUB
    hc = jnp.stack([p['b_dt_bias'], p['b_a_log'], p['b_d']]).astype(_F32)
    hc_l = jnp.pad(hc, ((0, 0), (0, _LANES - heads)))
    hr_s = jnp.pad(hc.T, ((0, _SUB - heads), (0, _LANES - 3)))
    dt_l = jnp.pad(dt_raw, ((0, 0), (0, 0), (0, _LANES - heads)))
    dt_s = jnp.pad(jnp.swapaxes(dt_raw, 1, 2), ((0, 0), (0, _SUB - heads), (0, 0)))
    y, conv_new, st = pl.pallas_call(
        _ssd_prompt_kernel,
        grid=(b, L // q),
        in_specs=[pl.BlockSpec((1, q, ch), lambda i, c: (i, c, 0)), pl.BlockSpec((1, q, inner), lambda i, c: (i, c, 0)),
                  pl.BlockSpec((1, q, _LANES), lambda i, c: (i, c, 0)), pl.BlockSpec((1, _SUB, q), lambda i, c: (i, 0, c)),
                  pl.BlockSpec((CONV_W, ch), lambda i, c: (0, 0)), pl.BlockSpec((1, ch), lambda i, c: (0, 0)),
                  pl.BlockSpec((3, _LANES), lambda i, c: (0, 0)), pl.BlockSpec((_SUB, _LANES), lambda i, c: (0, 0)),
                  pl.BlockSpec((1, inner), lambda i, c: (0, 0))],
        out_specs=[pl.BlockSpec((1, q, inner), lambda i, c: (i, c, 0)),
                   pl.BlockSpec((1, CONV_W - 1, ch), lambda i, c: (i, 0, 0)),
                   pl.BlockSpec((1, heads, D_STATE, inner // heads), lambda i, c: (i, 0, 0, 0))],
        out_shape=[jax.ShapeDtypeStruct((b, L, inner), _F32), jax.ShapeDtypeStruct((b, CONV_W - 1, ch), _F32),
                   jax.ShapeDtypeStruct((b, heads, D_STATE, inner // heads), _F32)],
        scratch_shapes=[pltpu.VMEM((q + _SUB, ch), _F32), pltpu.VMEM((heads, D_STATE, inner // heads), _F32)],
        compiler_params=pltpu.CompilerParams(dimension_semantics=("parallel", "arbitrary")),
        name="ssd_prompt",
    )(xbc, z, dt_l, dt_s, p['b_conv_w'].astype(_F32), p['b_conv_b'].reshape(1, ch).astype(_F32),
      hc_l, hr_s, p['b_norm_g'].reshape(1, inner).astype(_F32))
    return y, conv_new, jnp.swapaxes(st, 2, 3)


def _stick_breaking(q, k, v, q_pos, k_pos):
    z = jnp.einsum('btgrd,bsgd->bgrts', q, k).astype(jnp.float32) * HEAD_DIM ** -0.5
    m = k_pos[None, :] < q_pos[:, None]
    log_1m = jnp.where(m, jax.nn.log_sigmoid(-z), 0.0)
    cum = jnp.cumsum(log_1m, axis=-1)
    log_w = jax.nn.log_sigmoid(z) + cum[..., -1:] - cum
    w = jnp.where(m, jnp.exp(log_w), 0.0).astype(v.dtype)
    return jnp.einsum('bgrts,bsgd->btgrd', w, v)


def _heads_to_rows(out_t, heads):
    stacked = jnp.concatenate([out_t[:, h * _TQ:(h + 1) * _TQ] for h in range(heads)], axis=0)
    return stacked.T


def _softplus(z):
    return jnp.maximum(z, 0.0) + jnp.log(1.0 + jnp.exp(-jnp.abs(z)))


def _sb_prompt_kernel(q_ref, k_ref, vt_ref, o_ref, *, rep):
    j = pl.program_id(1)
    heads, d = q_ref.shape[1], q_ref.shape[3]
    groups = k_ref.shape[1]
    gw = rep * _TQ
    width = heads * _TQ
    qg = [q_ref[0, g * rep:(g + 1) * rep].reshape(gw, d) for g in range(groups)]
    row = lax.broadcasted_iota(jnp.int32, (_TK, _TK), 0)
    col = lax.broadcasted_iota(jnp.int32, (_TK, _TK), 1)
    after = jnp.where(col > row, 1.0, 0.0).astype(_BF16)
    s_loc = lax.broadcasted_iota(jnp.int32, (_TK, width), 0)
    t_loc = lax.broadcasted_iota(jnp.int32, (_TK, width), 1) & (_TQ - 1)
    diag = _TQ // _TK

    def chunk(c, carry, diag_off=None):
        masked = diag_off is not None
        if masked:
            strictly_before = s_loc + diag_off < t_loc
        tail, accs = carry
        sl = pl.ds(pl.multiple_of(c * _TK, _TK), _TK)
        z = jnp.concatenate(
            [lax.dot_general(k_ref[0, g, sl, :], qg[g], (((1,), (1,)), ((), ())), preferred_element_type=_F32)
             for g in range(groups)], axis=1) * (d ** -0.5)
        sp = _softplus(z)
        if masked:
            sp = jnp.where(strictly_before, sp, 0.0)
        hi = sp.astype(_BF16)
        lo = (sp - hi.astype(_F32)).astype(_BF16)
        later = (jnp.dot(after, hi, preferred_element_type=_F32)
                 + jnp.dot(after, lo, preferred_element_type=_F32))
        w = jnp.exp(z - sp - later - tail)
        if masked:
            w = jnp.where(strictly_before, w, 0.0)
        wb = w.astype(_BF16)
        accs = tuple(accs[g] + jnp.dot(vt_ref[0, g, :, sl], wb[:, g * gw:(g + 1) * gw],
                                       preferred_element_type=_F32) for g in range(groups))
        return tail + jnp.sum(sp, axis=0, keepdims=True), accs

    init = (jnp.zeros((1, width), _F32), tuple(jnp.zeros((d, gw), _F32) for _ in range(groups)))
    carry = init
    for i in reversed(range(diag)):
        carry = chunk(j * diag + i, carry, diag_off=i * _TK)
    _, accs = lax.fori_loop(0, j * diag, lambda i, cr: chunk(j * diag - 1 - i, cr), carry)
    o_ref[0] = _heads_to_rows(jnp.concatenate(accs, axis=1), heads)


def _sb_prompt(q, k, v):
    b, L, groups, rep, d = q.shape
    heads = groups * rep
    qh = jnp.transpose(q.reshape(b, L, heads, d), (0, 2, 1, 3)).astype(_BF16)
    kg = jnp.transpose(k, (0, 2, 1, 3)).astype(_BF16)
    vt = jnp.transpose(v, (0, 2, 3, 1)).astype(_BF16)
    return pl.pallas_call(
        functools.partial(_sb_prompt_kernel, rep=rep),
        grid=(b, L // _TQ),
        in_specs=[
            pl.BlockSpec((1, heads, _TQ, d), lambda i, j: (i, 0, j, 0)),
            pl.BlockSpec((1, groups, L, d), lambda i, j: (i, 0, 0, 0)),
            pl.BlockSpec((1, groups, d, L), lambda i, j: (i, 0, 0, 0)),
        ],
        out_specs=pl.BlockSpec((1, _TQ, heads * d), lambda i, j: (i, j, 0)),
        out_shape=jax.ShapeDtypeStruct((b, L, heads * d), _F32),
        compiler_params=pltpu.CompilerParams(dimension_semantics=("parallel", "arbitrary")),
        name="sb_prompt",
    )(qh, kg, vt)


def _causal_mha_kernel(q_ref, k_ref, vt_ref, o_ref, *, scale):
    j = pl.program_id(1)
    heads = q_ref.shape[1]
    dv = vt_ref.shape[2]
    width = heads * _TQ
    s_loc = lax.broadcasted_iota(jnp.int32, (_TK, width), 0)
    t_loc = lax.broadcasted_iota(jnp.int32, (_TK, width), 1) & (_TQ - 1)
    diag = _TQ // _TK

    def chunk(c, carry, diag_off=None):
        masked = diag_off is not None
        if masked:
            visible = s_loc + diag_off <= t_loc
        m, l, accs = carry
        sl = pl.ds(pl.multiple_of(c * _TK, _TK), _TK)
        logit = jnp.concatenate(
            [lax.dot_general(k_ref[0, h, sl, :], q_ref[0, h], (((1,), (1,)), ((), ())), preferred_element_type=_F32)
             for h in range(heads)], axis=1) * scale
        if masked:
            logit = jnp.where(visible, logit, _NEG_BIG)
        m_new = jnp.maximum(m, jnp.max(logit, axis=0, keepdims=True))
        p = jnp.exp(logit - m_new)
        alpha = jnp.exp(m - m_new)
        l = alpha * l + jnp.sum(p, axis=0, keepdims=True)
        pb = p.astype(_BF16)
        accs = tuple(alpha[:, h * _TQ:(h + 1) * _TQ] * accs[h]
                     + jnp.dot(vt_ref[0, h, :, sl], pb[:, h * _TQ:(h + 1) * _TQ], preferred_element_type=_F32)
                     for h in range(heads))
        return m_new, l, accs

    init = (jnp.full((1, width), _NEG_BIG, _F32), jnp.zeros((1, width), _F32),
            tuple(jnp.zeros((dv, _TQ), _F32) for _ in range(heads)))
    carry = init
    for i in range(diag):
        carry = chunk(j * diag + i, carry, diag_off=i * _TK)
    _, l, accs = lax.fori_loop(0, j * diag, lambda i, cr: chunk(i, cr), carry)
    o_ref[0] = _heads_to_rows(jnp.concatenate(accs, axis=1) / l, heads)


def _mla_prompt(q_nope, q_rope, k_nope, k_rope, v):
    b, L, heads, _ = q_nope.shape
    dv = v.shape[-1]
    q = jnp.transpose(jnp.concatenate([q_nope, q_rope], axis=-1), (0, 2, 1, 3)).astype(_BF16)
    kr = jnp.broadcast_to(k_rope[:, :, None, :], (b, L, heads, k_rope.shape[-1]))
    k = jnp.transpose(jnp.concatenate([k_nope, kr], axis=-1), (0, 2, 1, 3)).astype(_BF16)
    vt = jnp.transpose(v, (0, 2, 3, 1)).astype(_BF16)
    dq = q.shape[-1]
    return pl.pallas_call(
        functools.partial(_causal_mha_kernel, scale=MLA_SCALE),
        grid=(b, L // _TQ),
        in_specs=[
            pl.BlockSpec((1, heads, _TQ, dq), lambda i, j: (i, 0, j, 0)),
            pl.BlockSpec((1, heads, L, dq), lambda i, j: (i, 0, 0, 0)),
            pl.BlockSpec((1, heads, dv, L), lambda i, j: (i, 0, 0, 0)),
        ],
        out_specs=pl.BlockSpec((1, _TQ, heads * dv), lambda i, j: (i, j, 0)),
        out_shape=jax.ShapeDtypeStruct((b, L, heads * dv), _F32),
        compiler_params=pltpu.CompilerParams(dimension_semantics=("parallel", "arbitrary")),
        name="mla_prompt",
    )(q, k, vt)


def _mla_latent(q_nope, q_rope, c_all, r_all, w_uk, w_uv, q_pos, k_pos):
    b, T = q_nope.shape[:2]
    q_lat = jnp.einsum('bthd,chd->bthc', q_nope, w_uk)
    sc = (jnp.einsum('bthc,bsc->bhts', q_lat, c_all)
          + jnp.einsum('bthr,bsr->bhts', q_rope, r_all)).astype(jnp.float32) * MLA_SCALE
    sc = jnp.where(k_pos[None, :] <= q_pos[:, None], sc, -jnp.inf)
    pr = jax.nn.softmax(sc, axis=-1).astype(c_all.dtype)
    o_lat = jnp.einsum('bhts,bsc->bthc', pr, c_all)
    return jnp.einsum('bthc,chd->bthd', o_lat, w_uv).reshape(b, T, D_HEADS * D_V)


def _mixer_prompt(h, g_mix, pos, p):
    b, L = h.shape[:2]
    u = _project(h, g_mix, pos, p)
    a_out = _dsa_prompt(u['a_q'], u['a_k'], u['a_v'], u['a_qi'], u['a_ki'], u['a_w'])
    b_out, conv_new, ssm_new = _ssd_prompt(u['b_z'], u['b_xbc'], u['b_dt'], p)
    c_out = _sb_prompt(u['c_q'], u['c_k'], u['c_v'])
    w_ukv = jnp.concatenate([p['d_w_uk'].reshape(KV_RANK, D_HEADS * D_NOPE),
                             p['d_w_uv'].reshape(KV_RANK, D_HEADS * D_V)], axis=1)
    kv_up = _norm_matmul(u['d_ckv_raw'].reshape(b * L, KV_RANK), p['d_kv_norm_g'], w_ukv)
    k_nope = kv_up[:, :D_HEADS * D_NOPE].reshape(b, L, D_HEADS, D_NOPE)
    v_d = kv_up[:, D_HEADS * D_NOPE:].reshape(b, L, D_HEADS, D_V)
    d_out = _mla_prompt(u['d_qn'], u['d_qr'], k_nope, u['d_kr'], v_d)
    mix = jnp.concatenate([a_out, b_out, c_out, d_out], axis=-1)
    new = (u['a_k'], u['a_v'], u['a_ki'], u['c_k'], u['c_v'], u['d_ckv'], u['d_kr'],
           conv_new, ssm_new.astype(h.dtype))
    return mix, new


_PPS = 16
_SUB = 8


def _page_specs(rows, width, layer, n_pages, descending):
    specs = []
    for slot in range(_PPS):
        if descending:
            idx = lambda b, c, pt, slot=slot: (layer, pt[b, n_pages - 1 - (c * _PPS + slot)], 0, 0)
        else:
            idx = lambda b, c, pt, slot=slot: (layer, pt[b, c * _PPS + slot], 0, 0)
        specs.append(pl.BlockSpec((None, None, rows, width), idx))
    return specs


def _seq_spec(rows, width):
    return pl.BlockSpec((1, rows, width), lambda b, c, pt: (b, 0, 0))


def _pad_rows(x):
    return jnp.pad(x, ((0, 0), (0, _SUB - x.shape[1]), (0, 0)))


_NT = (((1,), (1,)), ((), ()))


def _paged_call(kernel_fn, page_table, seq_inputs, pools, layer, descending, out_shape, out_spec, scratch, name):
    b, n_pages = page_table.shape
    assert n_pages % _PPS == 0
    in_specs = [_seq_spec(x.shape[1], x.shape[2]) for x in seq_inputs]
    args = list(seq_inputs)
    for pool in pools:
        in_specs += _page_specs(pool.shape[2], pool.shape[3], layer, n_pages, descending)
        args += [pool] * _PPS
    return pl.pallas_call(
        kernel_fn,
        grid_spec=pltpu.PrefetchScalarGridSpec(
            num_scalar_prefetch=1, grid=(b, n_pages // _PPS),
            in_specs=in_specs, out_specs=out_spec, scratch_shapes=scratch),
        out_shape=out_shape,
        compiler_params=pltpu.CompilerParams(dimension_semantics=("parallel", "arbitrary")),
        name=name,
    )(page_table, *args)


def _sb_decode_kernel(pt_ref, q_ref, *refs, scale):
    k_refs, v_refs = refs[:_PPS], refs[_PPS:2 * _PPS]
    o_ref, tail_sc, acc_sc = refs[2 * _PPS:]
    c = pl.program_id(1)

    @pl.when(c == 0)
    def _():
        tail_sc[...] = jnp.zeros_like(tail_sc)
        acc_sc[...] = jnp.zeros_like(acc_sc)

    q8 = q_ref[0]
    row = lax.broadcasted_iota(jnp.int32, (PAGE_SIZE, PAGE_SIZE), 0)
    col = lax.broadcasted_iota(jnp.int32, (PAGE_SIZE, PAGE_SIZE), 1)
    after = jnp.where(row > col, 1.0, 0.0).astype(_BF16)
    tail, acc = tail_sc[...], acc_sc[...]
    for slot in range(_PPS):
        z = lax.dot_general(q8, k_refs[slot][...].astype(_BF16), _NT, preferred_element_type=_F32) * scale
        sp = _softplus(z)
        hi = sp.astype(_BF16)
        lo = (sp - hi.astype(_F32)).astype(_BF16)
        later = jnp.dot(hi, after, preferred_element_type=_F32) + jnp.dot(lo, after, preferred_element_type=_F32)
        w = jnp.exp(z - sp - later - tail)
        acc = acc + jnp.dot(w.astype(_BF16), v_refs[slot][...].astype(_BF16), preferred_element_type=_F32)
        tail = tail + jnp.sum(sp, axis=1, keepdims=True)
    tail_sc[...] = tail
    acc_sc[...] = acc

    @pl.when(c == pl.num_programs(1) - 1)
    def _():
        o_ref[0] = acc


def _sb_decode(q, pool_k, pool_v, layer, page_table):
    b, t, groups, rep, d = q.shape
    assert t == 1
    heads, gd = groups * rep, groups * d
    qh = q.reshape(b, heads, d)
    q8 = jnp.zeros((b, heads, gd), _F32)
    for hh in range(heads):
        g = hh // rep
        q8 = q8.at[:, hh, g * d:(g + 1) * d].set(qh[:, hh])
    q8 = _pad_rows(q8).astype(_BF16)
    view = lambda pool: pool.reshape(pool.shape[0], pool.shape[1], PAGE_SIZE, gd)
    out = _paged_call(
        functools.partial(_sb_decode_kernel, scale=d ** -0.5), page_table, [q8], [view(pool_k), view(pool_v)],
        layer, True, jax.ShapeDtypeStruct((b, _SUB, gd), _F32),
        pl.BlockSpec((1, _SUB, gd), lambda i, c, pt: (i, 0, 0)),
        [pltpu.VMEM((_SUB, PAGE_SIZE), _F32), pltpu.VMEM((_SUB, gd), _F32)], "sb_decode")
    return jnp.concatenate([out[:, hh, (hh // rep) * d:(hh // rep + 1) * d] for hh in range(heads)], axis=-1)


def _softmax_pages(logits, values, m, l, acc):
    mx = logits[0].max(axis=1, keepdims=True)
    for lg in logits[1:]:
        mx = jnp.maximum(mx, lg.max(axis=1, keepdims=True))
    m_new = jnp.maximum(m, mx)
    alpha = jnp.exp(m - m_new)
    l = alpha * l
    acc = alpha[:, :acc.shape[1]] * acc
    for lg, val in zip(logits, values):
        pr = jnp.exp(lg - m_new)
        l = l + jnp.sum(pr, axis=1, keepdims=True)
        acc = acc + jnp.dot(pr.astype(_BF16), val, preferred_element_type=_F32)
    return m_new, l, acc


def _mla_decode_kernel(pt_ref, ql_ref, qr_ref, lnew_ref, cnew_ref, *refs, scale):
    c_refs, r_refs = refs[:_PPS], refs[_PPS:2 * _PPS]
    o_ref, m_sc, l_sc, acc_sc = refs[2 * _PPS:]
    c = pl.program_id(1)

    @pl.when(c == 0)
    def _():
        m_sc[...] = lnew_ref[0]
        l_sc[...] = jnp.ones_like(l_sc)
        acc_sc[...] = jnp.broadcast_to(cnew_ref[0], acc_sc.shape)

    ql, qr = ql_ref[0], qr_ref[0]
    lat = [c_refs[s][...].astype(_BF16) for s in range(_PPS)]
    logits = [(lax.dot_general(ql, lat[s], _NT, preferred_element_type=_F32)
               + lax.dot_general(qr, r_refs[s][...].astype(_BF16), _NT, preferred_element_type=_F32)) * scale
              for s in range(_PPS)]
    m, l, acc = _softmax_pages(logits, lat, m_sc[...], l_sc[...], acc_sc[...])
    m_sc[...], l_sc[...], acc_sc[...] = m, l, acc

    @pl.when(c == pl.num_programs(1) - 1)
    def _():
        o_ref[0] = acc / l


def _mla_decode(q_nope, q_rope, c_new, r_new, pool_c, pool_r, w_uk, w_uv, layer, page_table):
    b, t, heads, _ = q_nope.shape
    assert t == 1
    q_lat = jnp.einsum('bthd,chd->bhc', q_nope, w_uk)
    qr = q_rope[:, 0]
    l_new = (jnp.einsum('bhc,bc->bh', q_lat, c_new[:, 0]) + jnp.einsum('bhr,br->bh', qr, r_new[:, 0])) * MLA_SCALE
    l_new = jnp.broadcast_to(_pad_rows(l_new[:, :, None]), (b, _SUB, PAGE_SIZE)).astype(_F32)
    rank = pool_c.shape[-1]
    o_lat = _paged_call(
        functools.partial(_mla_decode_kernel, scale=MLA_SCALE), page_table,
        [_pad_rows(q_lat).astype(_BF16), _pad_rows(qr).astype(_BF16), l_new, c_new.astype(_F32)],
        [pool_c, pool_r], layer, False, jax.ShapeDtypeStruct((b, _SUB, rank), _F32),
        pl.BlockSpec((1, _SUB, rank), lambda i, c, pt: (i, 0, 0)),
        [pltpu.VMEM((_SUB, PAGE_SIZE), _F32), pltpu.VMEM((_SUB, PAGE_SIZE), _F32), pltpu.VMEM((_SUB, rank), _F32)],
        "mla_decode")
    return jnp.einsum('bhc,chd->bhd', o_lat[:, :heads], w_uv).reshape(b, t, heads * w_uv.shape[-1])


def _idx_decode_kernel(pt_ref, qi_ref, w_ref, *refs):
    ki_refs, o_ref = refs[:_PPS], refs[_PPS]
    qi, w = qi_ref[0], w_ref[0]
    for s in range(_PPS):
        dots = lax.dot_general(qi, ki_refs[s][...].astype(_BF16), _NT, preferred_element_type=_F32)
        o_ref[0, s:s + 1, :] = jnp.sum(w * jnp.maximum(dots, 0.0), axis=0, keepdims=True) * IDX_SCALE


def _dsa_decode_kernel(pt_ref, sc_ref, scnew_ref, q_ref, lnew_ref, vnew_ref, *refs, n_sel):
    k_refs, v_refs = refs[:_PPS], refs[_PPS:2 * _PPS]
    o_ref, bias_sc, m_sc, l_sc, acc_sc = refs[2 * _PPS:]
    c = pl.program_id(1)
    n_rows = sc_ref.shape[1]

    @pl.when(c == 0)
    def _():
        key = _sortable_key(sc_ref[0])
        key_new = _sortable_key(scnew_ref[0])

        def count(pred_tile, pred_new):
            return jnp.sum(jnp.where(pred_tile, 1, 0)) + jnp.max(jnp.where(pred_new, 1, 0))

        def bit_step(i, carry):
            t, cnt_t = carry
            cand = t ^ lax.shift_left(jnp.int32(1), 31 - i)
            cnt = count(key >= cand, key_new >= cand)
            ok = cnt >= n_sel
            return jnp.where(ok, cand, t), jnp.where(ok, cnt, cnt_t)

        thr, cnt_thr = lax.fori_loop(0, 32, bit_step, (jnp.int32(_INT_MIN), jnp.int32(n_rows * PAGE_SIZE + 1)))
        bias_sc[...] = jnp.where(key >= thr, 0.0, _NEG_BIG)
        sel_new = key_new >= thr

        def init(sel_new):
            m_sc[...] = jnp.where(sel_new, lnew_ref[0], _NEG_BIG)
            l_sc[...] = jnp.where(sel_new, 1.0, 0.0) + jnp.zeros_like(l_sc)
            acc_sc[...] = jnp.where(sel_new[:, :acc_sc.shape[1]], vnew_ref[0], 0.0) + jnp.zeros_like(acc_sc)

        init(sel_new)

        @pl.when(cnt_thr > n_sel)
        def _():
            room = (n_sel - count(key > thr, key_new > thr)).astype(_F32)
            eq = key == thr
            eqf = jnp.where(eq, 1.0, 0.0)
            r_i = lax.broadcasted_iota(jnp.int32, (PAGE_SIZE, PAGE_SIZE), 0)
            c_i = lax.broadcasted_iota(jnp.int32, (PAGE_SIZE, PAGE_SIZE), 1)
            in_row = jnp.dot(eqf.astype(_BF16), jnp.where(r_i < c_i, 1.0, 0.0).astype(_BF16),
                             preferred_element_type=_F32)
            row_tot = jnp.broadcast_to(jnp.sum(eqf, axis=1, keepdims=True), eqf.shape).astype(_BF16)
            rr = lax.broadcasted_iota(jnp.int32, (n_rows, n_rows), 0)
            rc = lax.broadcasted_iota(jnp.int32, (n_rows, n_rows), 1)
            rows_before = jnp.dot(jnp.where(rc < rr, 1.0, 0.0).astype(_BF16), row_tot, preferred_element_type=_F32)
            keep = (key > thr) | (eq & (in_row + rows_before < room))
            bias_sc[...] = jnp.where(keep, 0.0, _NEG_BIG)
            init((key_new > thr) | ((key_new == thr) & (jnp.sum(eqf) < room)))

    q8 = q_ref[0]
    kb = [k_refs[s][...].astype(_BF16) for s in range(_PPS)]
    vb = [v_refs[s][...].astype(_BF16) for s in range(_PPS)]
    d = q8.shape[1]
    logits = [lax.dot_general(q8, kb[s], _NT, preferred_element_type=_F32) * (d ** -0.5)
              + bias_sc[pl.ds(c * _PPS + s, 1), :] for s in range(_PPS)]
    m, l, acc = _softmax_pages(logits, vb, m_sc[...], l_sc[...], acc_sc[...])
    m_sc[...], l_sc[...], acc_sc[...] = m, l, acc

    @pl.when(c == pl.num_programs(1) - 1)
    def _():
        o_ref[0] = acc / l[:, :acc.shape[1]]


def _dsa_decode(q, k_new, v_new, q_idx, ki_new, w_idx, pool_k, pool_v, pool_ki, layer, page_table):
    b, t = q.shape[:2]
    assert t == 1
    heads, d = q.shape[3], q.shape[4]
    ih, e = q_idx.shape[2], q_idx.shape[3]
    n_pages = page_table.shape[1]
    n_sel = min(TOPK_MAX, (n_pages * PAGE_SIZE + t) // 4)
    assert ih == _SUB
    w8 = jnp.broadcast_to(w_idx[:, 0, :, None], (b, ih, PAGE_SIZE)).astype(_F32)
    scores = _paged_call(
        _idx_decode_kernel, page_table, [q_idx[:, 0].astype(_BF16), w8], [pool_ki], layer, False,
        jax.ShapeDtypeStruct((b, n_pages, PAGE_SIZE), _F32),
        pl.BlockSpec((1, _PPS, PAGE_SIZE), lambda i, c, pt: (i, c, 0)), [], "idx_decode")
    rel_new = jax.nn.relu(jnp.einsum('bhe,be->bh', q_idx[:, 0], ki_new[:, 0]))
    sc_new = jnp.einsum('bh,bh->b', w_idx[:, 0], rel_new) * IDX_SCALE
    sc_new = jnp.broadcast_to(sc_new[:, None, None], (b, 1, PAGE_SIZE)).astype(_F32)
    qh = q.reshape(b, heads, d)
    l_new = jnp.einsum('bhd,bd->bh', qh, k_new.reshape(b, d)) * d ** -0.5
    l_new = jnp.broadcast_to(_pad_rows(l_new[:, :, None]), (b, _SUB, PAGE_SIZE)).astype(_F32)
    view = lambda pool: pool.reshape(pool.shape[0], pool.shape[1], PAGE_SIZE, d)
    out = _paged_call(
        functools.partial(_dsa_decode_kernel, n_sel=n_sel), page_table,
        [scores, sc_new, _pad_rows(qh).astype(_BF16), l_new, v_new.reshape(b, 1, d).astype(_F32)],
        [view(pool_k), view(pool_v)], layer, False, jax.ShapeDtypeStruct((b, _SUB, d), _F32),
        pl.BlockSpec((1, _SUB, d), lambda i, c, pt: (i, 0, 0)),
        [pltpu.VMEM((n_pages, PAGE_SIZE), _F32), pltpu.VMEM((_SUB, PAGE_SIZE), _F32),
         pltpu.VMEM((_SUB, PAGE_SIZE), _F32), pltpu.VMEM((_SUB, d), _F32)], "dsa_decode")
    return out[:, :heads].reshape(b, t, heads * d)


def _mixer_sample(h, g_mix, pos, p, l, caches, page_table, past_len):
    (ca_k, ca_v, ca_ki, cc_k, cc_v, cd_ckv, cd_kr, sb_conv, sb_ssm) = caches
    hn = h
    b, T = h.shape[:2]
    L = past_len + T
    k_pos = jnp.arange(L)
    u = _project(h, g_mix, pos, p)
    a_out = _dsa_decode(u['a_q'], u['a_k'], u['a_v'], u['a_qi'], u['a_ki'], u['a_w'], ca_k, ca_v, ca_ki, l, page_table)
    b_out, conv_new, ssm_new = _ssd_mixer(u['b_z'], u['b_xbc'], u['b_dt'], sb_conv[l], sb_ssm[l], p)
    c_out = _sb_decode(u['c_q'], cc_k, cc_v, l, page_table).reshape(b, T, C_HEADS * HEAD_DIM)
    d_out = _mla_decode(u['d_qn'], u['d_qr'], u['d_ckv'], u['d_kr'], cd_ckv, cd_kr, p['d_w_uk'], p['d_w_uv'],
                        l, page_table)
    mix = jnp.concatenate([a_out, b_out, c_out, d_out], axis=-1)
    new = (u['a_k'], u['a_v'], u['a_ki'], u['c_k'], u['c_v'], u['d_ckv'], u['d_kr'],
           conv_new.astype(sb_conv.dtype), ssm_new.astype(sb_ssm.dtype))
    return mix, new


def _mem_kv(mem, p):
    b, mt, dm = mem.shape
    kv = _norm_matmul(mem.reshape(b * mt, dm), p['g_mem_kv'], jnp.concatenate([p['w_mk'], p['w_mv']], axis=1))
    width = MEM_HEADS * MEM_HD
    return kv[:, :width].reshape(b, mt, MEM_HEADS, MEM_HD), kv[:, width:].reshape(b, mt, MEM_HEADS, MEM_HD)


def _post_mix_prompt(h, mix, mem_k, mem_v, p):
    b, L, dm = h.shape
    h2 = _mix_mem(h.reshape(b * L, dm), mix.reshape(b * L, dm), mem_k, mem_v, p, rows_per_batch=L)
    return _ffn(h2, p['g_ffn'], p['w_gate'], p['w_up'], p['w_down']).reshape(b, L, dm)


def _post_mix_sample(h, mix, mem_k, mem_v, p):
    b, L, dm = h.shape
    h1 = _matmul_residual(mix.reshape(b * L, dm), p['w_out'], h.reshape(b * L, dm))
    q = _norm_matmul(h1, p['g_mem_q'], p['w_mq']).reshape(b, L, MEM_HEADS, MEM_HD)
    sc = jnp.einsum('blhd,bmhd->bhlm', q, mem_k.astype(q.dtype)).astype(jnp.float32) * MEM_HD ** -0.5
    pr = jax.nn.softmax(sc, axis=-1).astype(q.dtype)
    o = jnp.einsum('bhlm,bmhd->blhd', pr, mem_v.astype(q.dtype)).reshape(b * L, MEM_HEADS * MEM_HD)
    h2 = _matmul_residual(o, p['w_mo'], h1)
    return _ffn(h2, p['g_ffn'], p['w_gate'], p['w_up'], p['w_down']).reshape(b, L, dm)


def _final_norm_kernel(x_ref, g_ref, o_ref):
    x = x_ref[...]
    o_ref[...] = x * lax.rsqrt(jnp.mean(x * x, axis=-1, keepdims=True) + NORM_EPS) * g_ref[...]


def _final_norm(x, g):
    shp = x.shape
    x2 = x.reshape(-1, shp[-1])
    n = x2.shape[0]
    tm = min(n, 1024)
    out = pl.pallas_call(
        _final_norm_kernel,
        grid=(n // tm,),
        in_specs=[pl.BlockSpec((tm, shp[-1]), lambda i: (i, 0)), pl.BlockSpec((1, shp[-1]), lambda i: (0, 0))],
        out_specs=pl.BlockSpec((tm, shp[-1]), lambda i: (i, 0)),
        out_shape=jax.ShapeDtypeStruct(x2.shape, x2.dtype),
    )(x2, g.reshape(1, -1))
    return out.reshape(shp)


def kernel(x_prompt, x_sample, mem_prompt, cache_a_k, cache_a_v, cache_a_kidx, cache_c_k, cache_c_v, cache_d_ckv, cache_d_krope, state_b_conv, state_b_ssm, cache_mem_k, cache_mem_v, page_table, g_mix, w_in, b_conv_w, b_conv_b, b_dt_bias, b_a_log, b_d, b_norm_g, d_q_norm_g, d_kv_norm_g, d_w_uq, d_w_uk, d_w_uv, w_out, g_mem_q, g_mem_kv, w_mq, w_mk, w_mv, w_mo, g_ffn, w_gate, w_up, w_down, g_final):
    s_prompt = x_prompt.shape[1]
    t_new = x_sample.shape[1]
    past_len = page_table.shape[1] * PAGE_SIZE
    pos_p = jnp.arange(s_prompt)
    pos_s = past_len + jnp.arange(t_new)
    caches = (cache_a_k, cache_a_v, cache_a_kidx, cache_c_k, cache_c_v, cache_d_ckv, cache_d_krope,
              state_b_conv, state_b_ssm)
    hp, hs = x_prompt, x_sample
    new_p, new_s, mem_ks, mem_vs = [], [], [], []
    for l in range(DEPTH):
        p = {'w_in': w_in[l], 'b_conv_w': b_conv_w[l], 'b_conv_b': b_conv_b[l], 'b_dt_bias': b_dt_bias[l],
             'b_a_log': b_a_log[l], 'b_d': b_d[l], 'b_norm_g': b_norm_g[l],
             'd_q_norm_g': d_q_norm_g[l], 'd_kv_norm_g': d_kv_norm_g[l], 'd_w_uq': d_w_uq[l],
             'd_w_uk': d_w_uk[l], 'd_w_uv': d_w_uv[l], 'w_out': w_out[l],
             'g_mem_q': g_mem_q[l], 'g_mem_kv': g_mem_kv[l], 'w_mq': w_mq[l], 'w_mk': w_mk[l],
             'w_mv': w_mv[l], 'w_mo': w_mo[l], 'g_ffn': g_ffn[l], 'w_gate': w_gate[l],
             'w_up': w_up[l], 'w_down': w_down[l]}
        mix_p, st_p = _mixer_prompt(hp, g_mix[l], pos_p, p)
        mk, mv = _mem_kv(mem_prompt, p)
        hp = _post_mix_prompt(hp, mix_p, mk, mv, p)
        new_p.append(st_p)
        mem_ks.append(mk)
        mem_vs.append(mv)
        mix_s, st_s = _mixer_sample(hs, g_mix[l], pos_s, p, l, caches, page_table, past_len)
        hs = _post_mix_sample(hs, mix_s, cache_mem_k[l], cache_mem_v[l], p)
        new_s.append(st_s)
    y_prompt = _final_norm(hp, g_final)
    y_sample = _final_norm(hs, g_final)
    (p_a_k, p_a_v, p_a_kidx, p_c_k, p_c_v, p_d_ckv, p_d_krope, p_b_conv, p_b_ssm) = [jnp.stack(t) for t in zip(*new_p)]
    (s_a_k, s_a_v, s_a_kidx, s_c_k, s_c_v, s_d_ckv, s_d_krope, s_b_conv, s_b_ssm) = [jnp.stack(t) for t in zip(*new_s)]
    p_mem_k = jnp.stack(mem_ks)
    p_mem_v = jnp.stack(mem_vs)
    return (y_prompt, y_sample,
            p_a_k, p_a_v, p_a_kidx, p_c_k, p_c_v, p_d_ckv, p_d_krope, p_b_conv, p_b_ssm, p_mem_k, p_mem_v,
            s_a_k, s_a_v, s_a_kidx, s_c_k, s_c_v, s_d_ckv, s_d_krope, s_b_conv, s_b_ssm)
```

```python
import functools
import math
import jax
import jax.numpy as jnp
from jax import lax
import numpy as np
from jax.experimental import pallas as pl
from jax.experimental.pallas import tpu as pltpu


D_MODEL = 1024
BATCH = 32
SEQ = 2048
DEPTH = 2
DEC_BATCH = 128
DEC_SEQ = 1
PAST_LEN = 16384
PAGE_SIZE = 128

N_MIXERS = 4
GROUP_WIDTH = D_MODEL // N_MIXERS
MIX_WIDTH = N_MIXERS * GROUP_WIDTH
HEAD_DIM = 64
ROPE_THETA = 10000.0
NORM_EPS = 1e-6
Q_BLOCK = 128

A_HEADS = GROUP_WIDTH // HEAD_DIM
A_KV_HEADS = 1
IDX_HEADS = 8
IDX_DIM = 32
TOPK_MAX = 256
IDX_SCALE = (IDX_HEADS * IDX_DIM) ** -0.5

SSM_P = HEAD_DIM
SSM_HEADS = GROUP_WIDTH // SSM_P
SSM_INNER = SSM_HEADS * SSM_P
SSM_GROUPS = 2
D_STATE = 128
CONV_W = 4
CONV_CH = SSM_INNER + 2 * SSM_GROUPS * D_STATE
SSD_CHUNK = 128

C_HEADS = GROUP_WIDTH // HEAD_DIM
C_KV_HEADS = 2

D_HEADS = GROUP_WIDTH // HEAD_DIM
D_NOPE = 64
D_ROPE = 32
D_V = GROUP_WIDTH // D_HEADS
Q_RANK = 256
KV_RANK = 128
MLA_SCALE = (D_NOPE + D_ROPE) ** -0.5

MEM_TOKENS = 256
MEM_HEADS = 4
MEM_HD = 64

D_FF = -(-8 * D_MODEL // (3 * 256)) * 256

IN_SIZES = (A_HEADS * HEAD_DIM, A_KV_HEADS * HEAD_DIM, A_KV_HEADS * HEAD_DIM, IDX_HEADS * IDX_DIM, IDX_DIM, IDX_HEADS,
            SSM_INNER, CONV_CH, SSM_HEADS,
            C_HEADS * HEAD_DIM, C_KV_HEADS * HEAD_DIM, C_KV_HEADS * HEAD_DIM,
            Q_RANK, KV_RANK, D_ROPE)
IN_TOTAL = sum(IN_SIZES)


def _in_offsets():
    return [int(o) for o in np.cumsum(IN_SIZES)[:-1]]


def _rmsnorm(x, g):
    xf = x.astype(jnp.float32)
    y = xf * lax.rsqrt(jnp.mean(xf * xf, axis=-1, keepdims=True) + NORM_EPS)
    return (y * g.astype(jnp.float32)).astype(x.dtype)


def _rope(x, pos):
    half = x.shape[-1] // 2
    inv = 1.0 / (ROPE_THETA ** (jnp.arange(half, dtype=jnp.float32) / half))
    ang = pos.astype(jnp.float32)[:, None] * inv[None, :]
    ang = ang.reshape((ang.shape[0],) + (1,) * (x.ndim - 3) + (half,))
    cos, sin = jnp.cos(ang), jnp.sin(ang)
    xf = x.astype(jnp.float32)
    x1, x2 = xf[..., :half], xf[..., half:]
    return jnp.concatenate([x1 * cos - x2 * sin, x2 * cos + x1 * sin], axis=-1).astype(x.dtype)


_LANES = 128
_VMEM_LIMIT = 48 * 1024 * 1024


def _row_tile(n, cap):
    tm = min(n, cap)
    assert n % tm == 0
    return tm


def _rms(x, g):
    return x * lax.rsqrt(jnp.mean(x * x, axis=-1, keepdims=True) + NORM_EPS) * g


def _norm_matmul_kernel(x_ref, g_ref, w_ref, o_ref):
    o_ref[...] = jnp.dot(_rms(x_ref[...], g_ref[...]).astype(_BF16), w_ref[...], preferred_element_type=_F32)


def _norm_matmul(x, g, w, tm_cap=256):
    n, k = x.shape
    m = w.shape[1]
    mp = -(-m // _LANES) * _LANES
    wb = jnp.pad(w.astype(_BF16), ((0, 0), (0, mp - m)))
    tm = _row_tile(n, tm_cap)
    out = pl.pallas_call(
        _norm_matmul_kernel,
        grid=(n // tm,),
        in_specs=[pl.BlockSpec((tm, k), lambda i: (i, 0)), pl.BlockSpec((1, k), lambda i: (0, 0)),
                  pl.BlockSpec((k, mp), lambda i: (0, 0))],
        out_specs=pl.BlockSpec((tm, mp), lambda i: (i, 0)),
        out_shape=jax.ShapeDtypeStruct((n, mp), _F32),
        compiler_params=pltpu.CompilerParams(dimension_semantics=("parallel",), vmem_limit_bytes=_VMEM_LIMIT),
        name="norm_matmul",
    )(x, g.reshape(1, k).astype(_F32), wb)
    return out[:, :m] if mp != m else out


def _matmul_residual_kernel(x_ref, w_ref, r_ref, o_ref):
    o_ref[...] = r_ref[...] + jnp.dot(x_ref[...].astype(_BF16), w_ref[...], preferred_element_type=_F32)


def _matmul_residual(x, w, res, tm_cap=256):
    n, k = x.shape
    m = w.shape[1]
    tm = _row_tile(n, tm_cap)
    return pl.pallas_call(
        _matmul_residual_kernel,
        grid=(n // tm,),
        in_specs=[pl.BlockSpec((tm, k), lambda i: (i, 0)), pl.BlockSpec((k, m), lambda i: (0, 0)),
                  pl.BlockSpec((tm, m), lambda i: (i, 0))],
        out_specs=pl.BlockSpec((tm, m), lambda i: (i, 0)),
        out_shape=jax.ShapeDtypeStruct((n, m), _F32),
        compiler_params=pltpu.CompilerParams(dimension_semantics=("parallel",), vmem_limit_bytes=_VMEM_LIMIT),
        name="matmul_residual",
    )(x, w.astype(_BF16), res)


_FF_SPLIT = 2


def _ffn_kernel(h_ref, g_ref, wg_ref, wu_ref, wd_ref, o_ref, hn_sc, acc_sc):
    k = pl.program_id(1)

    @pl.when(k == 0)
    def _():
        h = h_ref[...]
        hn_sc[...] = _rms(h, g_ref[...]).astype(_BF16)
        acc_sc[...] = h

    hn = hn_sc[...]
    gate = jnp.dot(hn, wg_ref[...], preferred_element_type=_F32)
    up = jnp.dot(hn, wu_ref[...], preferred_element_type=_F32)
    act = (gate * jax.nn.sigmoid(gate) * up).astype(_BF16)
    acc_sc[...] += jnp.dot(act, wd_ref[...], preferred_element_type=_F32)

    @pl.when(k == pl.num_programs(1) - 1)
    def _():
        o_ref[...] = acc_sc[...]


def _ffn(h, g, w_gate, w_up, w_down, tm_cap=512):
    n, dm = h.shape
    ff = w_gate.shape[1]
    tf = ff // _FF_SPLIT
    assert tf * _FF_SPLIT == ff and tf % _LANES == 0
    tm = _row_tile(n, tm_cap)
    return pl.pallas_call(
        _ffn_kernel,
        grid=(n // tm, _FF_SPLIT),
        in_specs=[pl.BlockSpec((tm, dm), lambda i, k: (i, 0)), pl.BlockSpec((1, dm), lambda i, k: (0, 0)),
                  pl.BlockSpec((dm, tf), lambda i, k: (0, k)), pl.BlockSpec((dm, tf), lambda i, k: (0, k)),
                  pl.BlockSpec((tf, dm), lambda i, k: (k, 0))],
        out_specs=pl.BlockSpec((tm, dm), lambda i, k: (i, 0)),
        out_shape=jax.ShapeDtypeStruct((n, dm), _F32),
        scratch_shapes=[pltpu.VMEM((tm, dm), _BF16), pltpu.VMEM((tm, dm), _F32)],
        compiler_params=pltpu.CompilerParams(dimension_semantics=("parallel", "arbitrary"),
                                             vmem_limit_bytes=_VMEM_LIMIT),
        name="ffn",
    )(h, g.reshape(1, dm).astype(_F32), w_gate.astype(_BF16), w_up.astype(_BF16), w_down.astype(_BF16))


def _mix_mem_kernel(h_ref, mix_ref, wout_ref, gq_ref, wmq_ref, mk_ref, mvt_ref, wmo_ref, o_ref):
    heads, hd = mk_ref.shape[1], mk_ref.shape[3]
    h1 = h_ref[...] + jnp.dot(mix_ref[...].astype(_BF16), wout_ref[...], preferred_element_type=_F32)
    q = jnp.dot(_rms(h1, gq_ref[...]).astype(_BF16), wmq_ref[...], preferred_element_type=_F32)
    outs = []
    for hh in range(heads):
        qh = q[:, hh * hd:(hh + 1) * hd].astype(_BF16)
        sc = lax.dot_general(mk_ref[0, hh], qh, (((1,), (1,)), ((), ())), preferred_element_type=_F32) * (hd ** -0.5)
        p = jnp.exp(sc - jnp.max(sc, axis=0, keepdims=True))
        o_t = jnp.dot(mvt_ref[0, hh], p.astype(_BF16), preferred_element_type=_F32)
        outs.append(o_t / jnp.sum(p, axis=0, keepdims=True))
    o = jnp.concatenate(outs, axis=0).T
    o_ref[...] = h1 + jnp.dot(o.astype(_BF16), wmo_ref[...], preferred_element_type=_F32)


def _mix_mem(h, mix, mem_k, mem_v, p, rows_per_batch, tm_cap=256):
    n, dm = h.shape
    b, mt, heads, hd = mem_k.shape
    tm = _row_tile(rows_per_batch, tm_cap)
    per = rows_per_batch // tm
    mk = jnp.transpose(mem_k, (0, 2, 1, 3)).astype(_BF16)
    mvt = jnp.transpose(mem_v, (0, 2, 3, 1)).astype(_BF16)
    const = lambda i: (0, 0)
    return pl.pallas_call(
        _mix_mem_kernel,
        grid=(n // tm,),
        in_specs=[pl.BlockSpec((tm, dm), lambda i: (i, 0)), pl.BlockSpec((tm, dm), lambda i: (i, 0)),
                  pl.BlockSpec((dm, dm), const), pl.BlockSpec((1, dm), const),
                  pl.BlockSpec((dm, heads * hd), const),
                  pl.BlockSpec((1, heads, mt, hd), lambda i: (i // per, 0, 0, 0)),
                  pl.BlockSpec((1, heads, hd, mt), lambda i: (i // per, 0, 0, 0)),
                  pl.BlockSpec((heads * hd, dm), const)],
        out_specs=pl.BlockSpec((tm, dm), lambda i: (i, 0)),
        out_shape=jax.ShapeDtypeStruct((n, dm), _F32),
        compiler_params=pltpu.CompilerParams(dimension_semantics=("parallel",), vmem_limit_bytes=_VMEM_LIMIT),
        name="mix_mem",
    )(h, mix, p['w_out'].astype(_BF16), p['g_mem_q'].reshape(1, dm).astype(_F32), p['w_mq'].astype(_BF16),
      mk, mvt, p['w_mo'].astype(_BF16))


def _gather_pages(pool, l, page_table):
    g = pool[l, page_table]
    return g.reshape((g.shape[0], g.shape[1] * g.shape[2]) + g.shape[3:])


def _gather_rows(pool, l, page_table, new, idx, past_len):
    page = jnp.clip(idx // PAGE_SIZE, 0, page_table.shape[1] - 1)
    phys = jnp.take_along_axis(page_table, page, axis=1)
    rows = pool[l, phys, idx % PAGE_SIZE]
    extra = (1,) * (new.ndim - 2)
    j = jnp.clip(idx - past_len, 0, new.shape[1] - 1)
    fresh = jnp.take_along_axis(new, j.reshape(j.shape + extra), axis=1)
    return jnp.where((idx < past_len).reshape(idx.shape + extra), rows.astype(new.dtype), fresh)


def _norm_matmul_multi_kernel(x_ref, g_ref, w_ref, *o_refs):
    y = jnp.dot(_rms(x_ref[...], g_ref[...]).astype(_BF16), w_ref[...], preferred_element_type=_F32)
    off = 0
    for o_ref in o_refs:
        o_ref[...] = y[:, off:off + o_ref.shape[1]]
        off += o_ref.shape[1]


def _norm_matmul_multi(x, g, w_groups, tm_cap=256):
    n, k = x.shape
    pads = [-(-w.shape[1] // _LANES) * _LANES for w in w_groups]
    wb = jnp.concatenate([jnp.pad(w.astype(_BF16), ((0, 0), (0, mp - w.shape[1]))) for w, mp in zip(w_groups, pads)],
                         axis=1)
    tm = _row_tile(n, tm_cap)
    return pl.pallas_call(
        _norm_matmul_multi_kernel,
        grid=(n // tm,),
        in_specs=[pl.BlockSpec((tm, k), lambda i: (i, 0)), pl.BlockSpec((1, k), lambda i: (0, 0)),
                  pl.BlockSpec((k, sum(pads)), lambda i: (0, 0))],
        out_specs=[pl.BlockSpec((tm, mp), lambda i: (i, 0)) for mp in pads],
        out_shape=[jax.ShapeDtypeStruct((n, mp), _F32) for mp in pads],
        compiler_params=pltpu.CompilerParams(dimension_semantics=("parallel",), vmem_limit_bytes=_VMEM_LIMIT),
        name="norm_matmul_multi",
    )(x, g.reshape(1, k).astype(_F32), wb)


def _project(h, g_mix, pos, p):
    b, L, dm = h.shape
    cols = jnp.split(p['w_in'], _in_offsets(), axis=1)
    cat = lambda ids: jnp.concatenate([cols[i] for i in ids], axis=1)
    outs = _norm_matmul_multi(h.reshape(b * L, dm), g_mix,
                              [cols[0], cols[3], cols[6], cols[7], cols[9], cols[10], cols[11], cols[12], cols[13],
                               cat([1, 2]), cat([4, 14, 5, 8])])
    a_q, a_qi, b_z, b_xbc, c_q, c_k, c_v, d_cq, d_ckv, kv_a, small = [o.reshape(b, L, -1) for o in outs]
    a_k, a_v = kv_a[..., :HEAD_DIM], kv_a[..., HEAD_DIM:2 * HEAD_DIM]
    o1, o2, o3 = IDX_DIM, IDX_DIM + D_ROPE, IDX_DIM + D_ROPE + IDX_HEADS
    a_ki, d_kr, a_w, b_dt = small[..., :o1], small[..., o1:o2], small[..., o2:o3], small[..., o3:o3 + SSM_HEADS]
    q_d = _norm_matmul(d_cq.reshape(b * L, Q_RANK), p['d_q_norm_g'], p['d_w_uq']).reshape(
        b, L, D_HEADS, D_NOPE + D_ROPE)
    return {
        'd_ckv_raw': d_ckv,
        'a_q': _rope(a_q.reshape(b, L, A_KV_HEADS, A_HEADS // A_KV_HEADS, HEAD_DIM), pos),
        'a_k': _rope(a_k.reshape(b, L, A_KV_HEADS, HEAD_DIM), pos),
        'a_v': a_v.reshape(b, L, A_KV_HEADS, HEAD_DIM),
        'a_qi': _rope(a_qi.reshape(b, L, IDX_HEADS, IDX_DIM), pos),
        'a_ki': _rope(a_ki, pos),
        'a_w': a_w,
        'b_z': b_z, 'b_xbc': b_xbc, 'b_dt': b_dt,
        'c_q': c_q.reshape(b, L, C_KV_HEADS, C_HEADS // C_KV_HEADS, HEAD_DIM),
        'c_k': c_k.reshape(b, L, C_KV_HEADS, HEAD_DIM),
        'c_v': c_v.reshape(b, L, C_KV_HEADS, HEAD_DIM),
        'd_qn': q_d[..., :D_NOPE],
        'd_qr': _rope(q_d[..., D_NOPE:], pos),
        'd_ckv': _rmsnorm(d_ckv, p['d_kv_norm_g']),
        'd_kr': _rope(d_kr, pos),
    }


def _indexer_scores(q_idx, w_idx, k_idx, q_pos, k_pos):
    rel = jax.nn.relu(jnp.einsum('bqhe,bse->bqhs', q_idx, k_idx).astype(jnp.float32))
    sc = jnp.einsum('bqh,bqhs->bqs', w_idx.astype(jnp.float32), rel) * IDX_SCALE
    return jnp.where(k_pos[None, None, :] <= q_pos[None, :, None], sc, -jnp.inf)


def _sparse_attend(q, k_sel, v_sel, valid):
    sc = jnp.einsum('btgrd,btkgd->btgrk', q, k_sel).astype(jnp.float32) * HEAD_DIM ** -0.5
    sc = jnp.where(valid[:, :, None, None, :], sc, -jnp.inf)
    pr = jax.nn.softmax(sc, axis=-1).astype(v_sel.dtype)
    return jnp.einsum('btgrk,btkgd->btgrd', pr, v_sel)


_BF16 = jnp.bfloat16
_F32 = jnp.float32
_INT_MIN = -2 ** 31
_NEG_BIG = -1e30
_TQ = 256
_TK = 128


def _sortable_key(x):
    u = lax.bitcast_convert_type(x, jnp.int32)
    return jnp.where(u < 0, -(u & 0x7FFFFFFF), u)


def _kth_largest_key(key_sc, nchunks, n_sel):
    def count_ge(cand):
        def body(c, cnt):
            kk = key_sc[pl.ds(pl.multiple_of(c * _TK, _TK), _TK), :]
            return cnt + jnp.where(kk >= cand, 1, 0)
        cnt = lax.fori_loop(0, nchunks, body, jnp.zeros((_TK, _TQ), jnp.int32))
        return jnp.sum(cnt, axis=0, keepdims=True)

    def bit_step(i, carry):
        t, cnt_t = carry
        cand = t ^ lax.shift_left(jnp.int32(1), 31 - i)
        cnt = count_ge(cand)
        ok = cnt >= n_sel
        return jnp.where(ok, cand, t), jnp.where(ok, cnt, cnt_t)

    t0 = jnp.full((1, _TQ), _INT_MIN, jnp.int32)
    c0 = jnp.zeros((1, _TQ), jnp.int32) + nchunks * _TK
    return lax.fori_loop(0, 32, bit_step, (t0, c0))


def _drop_late_ties(key_sc, nchunks, thr, n_sel):
    def count_gt(c, cnt):
        kk = key_sc[pl.ds(pl.multiple_of(c * _TK, _TK), _TK), :]
        return cnt + jnp.where(kk > thr, 1, 0)
    n_gt = jnp.sum(lax.fori_loop(0, nchunks, count_gt, jnp.zeros((_TK, _TQ), jnp.int32)), axis=0, keepdims=True)
    room = (n_sel - n_gt).astype(_F32)
    row = lax.broadcasted_iota(jnp.int32, (_TK, _TK), 0)
    col = lax.broadcasted_iota(jnp.int32, (_TK, _TK), 1)
    before = jnp.where(col < row, 1.0, 0.0).astype(_BF16)

    def body(c, seen):
        sl = pl.ds(pl.multiple_of(c * _TK, _TK), _TK)
        kk = key_sc[sl, :]
        eq = kk == thr
        eqf = jnp.where(eq, 1.0, 0.0)
        rank = jnp.dot(before, eqf.astype(_BF16), preferred_element_type=_F32) + seen
        key_sc[sl, :] = jnp.where(eq & (rank >= room), _INT_MIN, kk)
        return seen + jnp.sum(eqf, axis=0, keepdims=True)

    lax.fori_loop(0, nchunks, body, jnp.zeros((1, _TQ), _F32))


def _dsa_prompt_kernel(q_ref, qi_ref, wt_ref, ki_ref, k_ref, vt_ref, o_ref, key_sc, *, n_sel):
    j = pl.program_id(1)
    nchunks = (j + 1) * (_TQ // _TK)
    heads, d = q_ref.shape[1], q_ref.shape[3]
    ih, e = qi_ref.shape[1], qi_ref.shape[3]
    qidx = qi_ref[0].reshape(ih * _TQ, e)
    wt = wt_ref[0]
    t_pos = j * _TQ + lax.broadcasted_iota(jnp.int32, (_TK, _TQ), 1)
    s_loc = lax.broadcasted_iota(jnp.int32, (_TK, _TQ), 0)

    def score_chunk(c, _):
        sl = pl.ds(pl.multiple_of(c * _TK, _TK), _TK)
        dots = lax.dot_general(ki_ref[0, sl, :], qidx, (((1,), (1,)), ((), ())), preferred_element_type=_F32)
        acc = jnp.zeros((_TK, _TQ), _F32)
        for h in range(ih):
            acc = acc + wt[h:h + 1, :] * jnp.maximum(dots[:, h * _TQ:(h + 1) * _TQ], 0.0)
        sc = jnp.where(c * _TK + s_loc <= t_pos, acc * IDX_SCALE, -jnp.inf)
        key_sc[sl, :] = _sortable_key(sc)
        return 0

    lax.fori_loop(0, nchunks, score_chunk, 0)
    thr, cnt_thr = _kth_largest_key(key_sc, nchunks, n_sel)

    @pl.when(jnp.max(cnt_thr) > n_sel)
    def _():
        _drop_late_ties(key_sc, nchunks, thr, n_sel)

    q_all = q_ref[0].reshape(heads * _TQ, d)
    t_pos_w = j * _TQ + lax.broadcasted_iota(jnp.int32, (_TK, _TQ), 1)

    def att_chunk(c, carry):
        m, l, acc = carry
        sl = pl.ds(pl.multiple_of(c * _TK, _TK), _TK)
        sel = (key_sc[sl, :] >= thr) & (c * _TK + s_loc <= t_pos_w)
        bias1 = jnp.where(sel, 0.0, _NEG_BIG)
        bias = jnp.concatenate([bias1] * heads, axis=1)
        logit = lax.dot_general(k_ref[0, sl, :], q_all, (((1,), (1,)), ((), ())),
                                preferred_element_type=_F32) * (d ** -0.5) + bias
        m_new = jnp.maximum(m, jnp.max(logit, axis=0, keepdims=True))
        p = jnp.exp(logit - m_new)
        alpha = jnp.exp(m - m_new)
        l = alpha * l + jnp.sum(p, axis=0, keepdims=True)
        acc = alpha * acc + jnp.dot(vt_ref[0, :, sl], p.astype(_BF16), preferred_element_type=_F32)
        return m_new, l, acc

    m0 = jnp.full((1, heads * _TQ), _NEG_BIG, _F32)
    l0 = jnp.zeros((1, heads * _TQ), _F32)
    a0 = jnp.zeros((d, heads * _TQ), _F32)
    _, l, acc = lax.fori_loop(0, nchunks, att_chunk, (m0, l0, a0))
    out_t = acc / l
    stacked = jnp.concatenate([out_t[:, h * _TQ:(h + 1) * _TQ] for h in range(heads)], axis=0)
    o_ref[0] = stacked.T


def _dsa_prompt(q, k, v, q_idx, k_idx, w_idx):
    b, L = q.shape[:2]
    heads, d = q.shape[3], q.shape[4]
    ih, e = q_idx.shape[2], q_idx.shape[3]
    n_sel = min(TOPK_MAX, L // 4)
    qh = jnp.transpose(q.reshape(b, L, heads, d), (0, 2, 1, 3)).astype(_BF16)
    qih = jnp.transpose(q_idx, (0, 2, 1, 3)).astype(_BF16)
    wt = jnp.transpose(w_idx, (0, 2, 1)).astype(_F32)
    kk = k.reshape(b, L, d).astype(_BF16)
    vt = jnp.transpose(v.reshape(b, L, d), (0, 2, 1)).astype(_BF16)
    return pl.pallas_call(
        functools.partial(_dsa_prompt_kernel, n_sel=n_sel),
        grid=(b, L // _TQ),
        in_specs=[
            pl.BlockSpec((1, heads, _TQ, d), lambda i, j: (i, 0, j, 0)),
            pl.BlockSpec((1, ih, _TQ, e), lambda i, j: (i, 0, j, 0)),
            pl.BlockSpec((1, ih, _TQ), lambda i, j: (i, 0, j)),
            pl.BlockSpec((1, L, e), lambda i, j: (i, 0, 0)),
            pl.BlockSpec((1, L, d), lambda i, j: (i, 0, 0)),
            pl.BlockSpec((1, d, L), lambda i, j: (i, 0, 0)),
        ],
        out_specs=pl.BlockSpec((1, _TQ, heads * d), lambda i, j: (i, j, 0)),
        out_shape=jax.ShapeDtypeStruct((b, L, heads * d), _F32),
        scratch_shapes=[pltpu.VMEM((L, _TQ), jnp.int32)],
        compiler_params=pltpu.CompilerParams(dimension_semantics=("parallel", "arbitrary")),
        name="dsa_prompt",
    )(qh, qih, wt, k_idx.astype(_BF16), kk, vt)


def _ssd_scan(x, dt, a, bm, cm, s0):
    b, l, h, p = x.shape
    g, n = bm.shape[2], bm.shape[3]
    r = h // g
    q = SSD_CHUNK if l % SSD_CHUNK == 0 else l
    c = l // q
    f32 = jnp.float32
    xdt = (x.astype(f32) * dt[..., None]).reshape(b, c, q, g, r, p)
    acs = jnp.cumsum((dt * a).reshape(b, c, q, g, r), axis=2)
    bc = bm.astype(f32).reshape(b, c, q, g, n)
    cc = cm.astype(f32).reshape(b, c, q, g, n)
    acs_t = jnp.moveaxis(acs, 2, -1)
    tril = jnp.tril(jnp.ones((q, q), dtype=bool))
    seg = jnp.exp(jnp.where(tril, acs_t[..., :, None] - acs_t[..., None, :], -jnp.inf))
    cb = jnp.einsum('bcign,bcjgn->bcgij', cc, bc)
    y_diag = jnp.einsum('bcgrij,bcjgrp->bcigrp', cb[:, :, :, None] * seg, xdt)
    decay_end = jnp.exp(acs[:, :, -1:] - acs)
    states = jnp.einsum('bcjgn,bcjgrp->bcgrpn', bc, xdt * decay_end[..., None])
    chunk_decay = jnp.exp(acs[:, :, -1])

    def step(s, inp):
        st, dc = inp
        return dc[..., None, None] * s + st, s

    s_fin, s_in = lax.scan(step, s0.astype(f32).reshape(b, g, r, p, n),
                           (jnp.moveaxis(states, 1, 0), jnp.moveaxis(chunk_decay, 1, 0)))
    s_in = jnp.moveaxis(s_in, 0, 1)
    y_off = jnp.einsum('bcign,bcgrpn->bcigrp', cc, s_in) * jnp.exp(acs)[..., None]
    return (y_diag + y_off).reshape(b, l, h, p), s_fin.reshape(b, h, p, n)


def _ssd_mixer(z, xbc, dt_raw, conv_prev, ssm_prev, p):
    b, T = xbc.shape[:2]
    f32 = jnp.float32
    xin = jnp.concatenate([conv_prev.astype(xbc.dtype), xbc], axis=1)
    cw = p['b_conv_w']
    conv = p['b_conv_b'] + xin[:, 0:T] * cw[0]
    for w in range(1, CONV_W):
        conv = conv + xin[:, w:w + T] * cw[w]
    xbc_c = jax.nn.silu(conv)
    xs, bm, cm = jnp.split(xbc_c, [SSM_INNER, SSM_INNER + SSM_GROUPS * D_STATE], axis=-1)
    xs = xs.reshape(b, T, SSM_HEADS, SSM_P)
    bm = bm.reshape(b, T, SSM_GROUPS, D_STATE)
    cm = cm.reshape(b, T, SSM_GROUPS, D_STATE)
    dt = jax.nn.softplus(dt_raw.astype(f32) + p['b_dt_bias'].astype(f32))
    a = -jnp.exp(p['b_a_log'].astype(f32))
    y, s_fin = _ssd_scan(xs, dt, a, bm, cm, ssm_prev)
    y = y + p['b_d'].astype(f32)[:, None] * xs.astype(f32)
    y = y.reshape(b, T, SSM_INNER) * jax.nn.silu(z.astype(f32))
    out = _rmsnorm(y, p['b_norm_g']).astype(z.dtype)
    return out, xin[:, -(CONV_W - 1):], s_fin


def _split_dot(a, b_mat, a_is_exact):
    x = b_mat if a_is_exact else a
    hi = x.astype(_BF16)
    lo = (x - hi.astype(_F32)).astype(_BF16)
    if a_is_exact:
        return jnp.dot(a, hi, preferred_element_type=_F32) + jnp.dot(a, lo, preferred_element_type=_F32)
    return jnp.dot(hi, b_mat, preferred_element_type=_F32) + jnp.dot(lo, b_mat, preferred_element_type=_F32)


def _ssd_prompt_kernel(xbc_ref, z_ref, dtc_ref, dtr_ref, cw_ref, cb_ref, hc_ref, hr_ref, ng_ref,
                       y_ref, conv_ref, st_ref, buf_sc, st_sc):
    c = pl.program_id(1)
    q = xbc_ref.shape[1]
    heads = SSM_HEADS
    inner = z_ref.shape[2]
    hp = inner // heads
    n = (xbc_ref.shape[2] - inner) // (2 * SSM_GROUPS)
    rep = heads // SSM_GROUPS
    lead = _SUB - (CONV_W - 1)

    @pl.when(c == 0)
    def _():
        buf_sc[0:_SUB, :] = jnp.zeros((_SUB, buf_sc.shape[1]), _F32)
        st_sc[...] = jnp.zeros_like(st_sc)

    xbc = xbc_ref[0]
    buf_sc[_SUB:_SUB + q, :] = xbc
    conv = cb_ref[...] + buf_sc[lead:lead + q, :] * cw_ref[0:1, :]
    for w in range(1, CONV_W):
        conv = conv + buf_sc[lead + w:lead + w + q, :] * cw_ref[w:w + 1, :]
    act = conv * jax.nn.sigmoid(conv)
    tail = buf_sc[q + lead:q + _SUB, :]
    buf_sc[lead:_SUB, :] = tail

    dt_c = _softplus(dtc_ref[0] + hc_ref[0:1, :])
    dt_r = _softplus(dtr_ref[0] + hr_ref[:, 0:1])
    da_c = dt_c * -jnp.exp(hc_ref[1:2, :])
    da_r = dt_r * -jnp.exp(hr_ref[:, 1:2])
    ri = lax.broadcasted_iota(jnp.int32, (q, q), 0)
    ci = lax.broadcasted_iota(jnp.int32, (q, q), 1)
    causal = ci <= ri
    acs_c = _split_dot(jnp.where(causal, 1.0, 0.0).astype(_BF16), da_c, True)
    acs_r = _split_dot(da_r, jnp.where(ri <= ci, 1.0, 0.0).astype(_BF16), False)
    acs_end = acs_c[q - 1:q, :]
    decay_end = jnp.exp(acs_end - acs_c)
    grow = jnp.exp(acs_c)

    ys = []
    for hh in range(heads):
        g = hh // rep
        xs = act[:, hh * hp:(hh + 1) * hp]
        bm = act[:, inner + g * n:inner + (g + 1) * n]
        cm = act[:, inner + SSM_GROUPS * n + g * n:inner + SSM_GROUPS * n + (g + 1) * n]
        xdt = xs * dt_c[:, hh:hh + 1]
        cbm = lax.dot_general(cm.astype(_BF16), bm.astype(_BF16), _NT, preferred_element_type=_F32)
        seg = jnp.exp(jnp.where(causal, acs_c[:, hh:hh + 1] - acs_r[hh:hh + 1, :], -jnp.inf))
        y = jnp.dot((cbm * seg).astype(_BF16), xdt.astype(_BF16), preferred_element_type=_F32)
        s_in = st_sc[hh]
        y = y + jnp.dot(cm.astype(_BF16), s_in.astype(_BF16), preferred_element_type=_F32) * grow[:, hh:hh + 1]
        new = jnp.dot(bm.T.astype(_BF16), (xdt * decay_end[:, hh:hh + 1]).astype(_BF16), preferred_element_type=_F32)
        st_sc[hh] = jnp.exp(acs_end[:, hh:hh + 1]) * s_in + new
        ys.append(y + hr_ref[hh:hh + 1, 2:3] * xs)
    y = jnp.concatenate(ys, axis=1)
    zz = z_ref[0]
    y_ref[0] = _rms(y * (zz * jax.nn.sigmoid(zz)), ng_ref[...])

    @pl.when(c == pl.num_programs(1) - 1)
    def _():
        conv_ref[0] = tail
        st_ref[0] = st_sc[...]


def _ssd_prompt(z, xbc, dt_raw, p):
    b, L, inner = z.shape
    heads = dt_raw.shape[-1]
    ch = xbc.shape[-1]
    q = SSD_CHUNK
    assert L % q == 0
    assert heads == SSM_HEADS and heads <= _SUB
    hc = jnp.stack([p['b_dt_bias'], p['b_a_log'], p['b_d']]).astype(_F32)
    hc_l = jnp.pad(hc, ((0, 0), (0, _LANES - heads)))
    hr_s = jnp.pad(hc.T, ((0, _SUB - heads), (0, _LANES - 3)))
    dt_l = jnp.pad(dt_raw, ((0, 0), (0, 0), (0, _LANES - heads)))
    dt_s = jnp.pad(jnp.swapaxes(dt_raw, 1, 2), ((0, 0), (0, _SUB - heads), (0, 0)))
    y, conv_new, st = pl.pallas_call(
        _ssd_prompt_kernel,
        grid=(b, L // q),
        in_specs=[pl.BlockSpec((1, q, ch), lambda i, c: (i, c, 0)), pl.BlockSpec((1, q, inner), lambda i, c: (i, c, 0)),
                  pl.BlockSpec((1, q, _LANES), lambda i, c: (i, c, 0)), pl.BlockSpec((1, _SUB, q), lambda i, c: (i, 0, c)),
                  pl.BlockSpec((CONV_W, ch), lambda i, c: (0, 0)), pl.BlockSpec((1, ch), lambda i, c: (0, 0)),
                  pl.BlockSpec((3, _LANES), lambda i, c: (0, 0)), pl.BlockSpec((_SUB, _LANES), lambda i, c: (0, 0)),
                  pl.BlockSpec((1, inner), lambda i, c: (0, 0))],
        out_specs=[pl.BlockSpec((1, q, inner), lambda i, c: (i, c, 0)),
                   pl.BlockSpec((1, CONV_W - 1, ch), lambda i, c: (i, 0, 0)),
                   pl.BlockSpec((1, heads, D_STATE, inner // heads), lambda i, c: (i, 0, 0, 0))],
        out_shape=[jax.ShapeDtypeStruct((b, L, inner), _F32), jax.ShapeDtypeStruct((b, CONV_W - 1, ch), _F32),
                   jax.ShapeDtypeStruct((b, heads, D_STATE, inner // heads), _F32)],
        scratch_shapes=[pltpu.VMEM((q + _SUB, ch), _F32), pltpu.VMEM((heads, D_STATE, inner // heads), _F32)],
        compiler_params=pltpu.CompilerParams(dimension_semantics=("parallel", "arbitrary")),
        name="ssd_prompt",
    )(xbc, z, dt_l, dt_s, p['b_conv_w'].astype(_F32), p['b_conv_b'].reshape(1, ch).astype(_F32),
      hc_l, hr_s, p['b_norm_g'].reshape(1, inner).astype(_F32))
    return y, conv_new, jnp.swapaxes(st, 2, 3)


def _stick_breaking(q, k, v, q_pos, k_pos):
    z = jnp.einsum('btgrd,bsgd->bgrts', q, k).astype(jnp.float32) * HEAD_DIM ** -0.5
    m = k_pos[None, :] < q_pos[:, None]
    log_1m = jnp.where(m, jax.nn.log_sigmoid(-z), 0.0)
    cum = jnp.cumsum(log_1m, axis=-1)
    log_w = jax.nn.log_sigmoid(z) + cum[..., -1:] - cum
    w = jnp.where(m, jnp.exp(log_w), 0.0).astype(v.dtype)
    return jnp.einsum('bgrts,bsgd->btgrd', w, v)


def _heads_to_rows(out_t, heads):
    stacked = jnp.concatenate([out_t[:, h * _TQ:(h + 1) * _TQ] for h in range(heads)], axis=0)
    return stacked.T


def _softplus(z):
    return jnp.maximum(z, 0.0) + jnp.log(1.0 + jnp.exp(-jnp.abs(z)))


def _sb_prompt_kernel(q_ref, k_ref, vt_ref, o_ref, *, rep):
    j = pl.program_id(1)
    heads, d = q_ref.shape[1], q_ref.shape[3]
    groups = k_ref.shape[1]
    gw = rep * _TQ
    width = heads * _TQ
    qg = [q_ref[0, g * rep:(g + 1) * rep].reshape(gw, d) for g in range(groups)]
    row = lax.broadcasted_iota(jnp.int32, (_TK, _TK), 0)
    col = lax.broadcasted_iota(jnp.int32, (_TK, _TK), 1)
    after = jnp.where(col > row, 1.0, 0.0).astype(_BF16)
    s_loc = lax.broadcasted_iota(jnp.int32, (_TK, width), 0)
    t_loc = lax.broadcasted_iota(jnp.int32, (_TK, width), 1) & (_TQ - 1)
    diag = _TQ // _TK

    def chunk(c, carry, diag_off=None):
        masked = diag_off is not None
        if masked:
            strictly_before = s_loc + diag_off < t_loc
        tail, accs = carry
        sl = pl.ds(pl.multiple_of(c * _TK, _TK), _TK)
        z = jnp.concatenate(
            [lax.dot_general(k_ref[0, g, sl, :], qg[g], (((1,), (1,)), ((), ())), preferred_element_type=_F32)
             for g in range(groups)], axis=1) * (d ** -0.5)
        sp = _softplus(z)
        if masked:
            sp = jnp.where(strictly_before, sp, 0.0)
        hi = sp.astype(_BF16)
        lo = (sp - hi.astype(_F32)).astype(_BF16)
        later = (jnp.dot(after, hi, preferred_element_type=_F32)
                 + jnp.dot(after, lo, preferred_element_type=_F32))
        w = jnp.exp(z - sp - later - tail)
        if masked:
            w = jnp.where(strictly_before, w, 0.0)
        wb = w.astype(_BF16)
        accs = tuple(accs[g] + jnp.dot(vt_ref[0, g, :, sl], wb[:, g * gw:(g + 1) * gw],
                                       preferred_element_type=_F32) for g in range(groups))
        return tail + jnp.sum(sp, axis=0, keepdims=True), accs

    init = (jnp.zeros((1, width), _F32), tuple(jnp.zeros((d, gw), _F32) for _ in range(groups)))
    carry = init
    for i in reversed(range(diag)):
        carry = chunk(j * diag + i, carry, diag_off=i * _TK)
    _, accs = lax.fori_loop(0, j * diag, lambda i, cr: chunk(j * diag - 1 - i, cr), carry)
    o_ref[0] = _heads_to_rows(jnp.concatenate(accs, axis=1), heads)


def _sb_prompt(q, k, v):
    b, L, groups, rep, d = q.shape
    heads = groups * rep
    qh = jnp.transpose(q.reshape(b, L, heads, d), (0, 2, 1, 3)).astype(_BF16)
    kg = jnp.transpose(k, (0, 2, 1, 3)).astype(_BF16)
    vt = jnp.transpose(v, (0, 2, 3, 1)).astype(_BF16)
    return pl.pallas_call(
        functools.partial(_sb_prompt_kernel, rep=rep),
        grid=(b, L // _TQ),
        in_specs=[
            pl.BlockSpec((1, heads, _TQ, d), lambda i, j: (i, 0, j, 0)),
            pl.BlockSpec((1, groups, L, d), lambda i, j: (i, 0, 0, 0)),
            pl.BlockSpec((1, groups, d, L), lambda i, j: (i, 0, 0, 0)),
        ],
        out_specs=pl.BlockSpec((1, _TQ, heads * d), lambda i, j: (i, j, 0)),
        out_shape=jax.ShapeDtypeStruct((b, L, heads * d), _F32),
        compiler_params=pltpu.CompilerParams(dimension_semantics=("parallel", "arbitrary")),
        name="sb_prompt",
    )(qh, kg, vt)


def _causal_mha_kernel(q_ref, k_ref, vt_ref, o_ref, *, scale):
    j = pl.program_id(1)
    heads = q_ref.shape[1]
    dv = vt_ref.shape[2]
    width = heads * _TQ
    s_loc = lax.broadcasted_iota(jnp.int32, (_TK, width), 0)
    t_loc = lax.broadcasted_iota(jnp.int32, (_TK, width), 1) & (_TQ - 1)
    diag = _TQ // _TK

    def chunk(c, carry, diag_off=None):
        masked = diag_off is not None
        if masked:
            visible = s_loc + diag_off <= t_loc
        m, l, accs = carry
        sl = pl.ds(pl.multiple_of(c * _TK, _TK), _TK)
        logit = jnp.concatenate(
            [lax.dot_general(k_ref[0, h, sl, :], q_ref[0, h], (((1,), (1,)), ((), ())), preferred_element_type=_F32)
             for h in range(heads)], axis=1) * scale
        if masked:
            logit = jnp.where(visible, logit, _NEG_BIG)
        m_new = jnp.maximum(m, jnp.max(logit, axis=0, keepdims=True))
        p = jnp.exp(logit - m_new)
        alpha = jnp.exp(m - m_new)
        l = alpha * l + jnp.sum(p, axis=0, keepdims=True)
        pb = p.astype(_BF16)
        accs = tuple(alpha[:, h * _TQ:(h + 1) * _TQ] * accs[h]
                     + jnp.dot(vt_ref[0, h, :, sl], pb[:, h * _TQ:(h + 1) * _TQ], preferred_element_type=_F32)
                     for h in range(heads))
        return m_new, l, accs

    init = (jnp.full((1, width), _NEG_BIG, _F32), jnp.zeros((1, width), _F32),
            tuple(jnp.zeros((dv, _TQ), _F32) for _ in range(heads)))
    carry = init
    for i in range(diag):
        carry = chunk(j * diag + i, carry, diag_off=i * _TK)
    _, l, accs = lax.fori_loop(0, j * diag, lambda i, cr: chunk(i, cr), carry)
    o_ref[0] = _heads_to_rows(jnp.concatenate(accs, axis=1) / l, heads)


def _mla_prompt(q_nope, q_rope, k_nope, k_rope, v):
    b, L, heads, _ = q_nope.shape
    dv = v.shape[-1]
    q = jnp.transpose(jnp.concatenate([q_nope, q_rope], axis=-1), (0, 2, 1, 3)).astype(_BF16)
    kr = jnp.broadcast_to(k_rope[:, :, None, :], (b, L, heads, k_rope.shape[-1]))
    k = jnp.transpose(jnp.concatenate([k_nope, kr], axis=-1), (0, 2, 1, 3)).astype(_BF16)
    vt = jnp.transpose(v, (0, 2, 3, 1)).astype(_BF16)
    dq = q.shape[-1]
    return pl.pallas_call(
        functools.partial(_causal_mha_kernel, scale=MLA_SCALE),
        grid=(b, L // _TQ),
        in_specs=[
            pl.BlockSpec((1, heads, _TQ, dq), lambda i, j: (i, 0, j, 0)),
            pl.BlockSpec((1, heads, L, dq), lambda i, j: (i, 0, 0, 0)),
            pl.BlockSpec((1, heads, dv, L), lambda i, j: (i, 0, 0, 0)),
        ],
        out_specs=pl.BlockSpec((1, _TQ, heads * dv), lambda i, j: (i, j, 0)),
        out_shape=jax.ShapeDtypeStruct((b, L, heads * dv), _F32),
        compiler_params=pltpu.CompilerParams(dimension_semantics=("parallel", "arbitrary")),
        name="mla_prompt",
    )(q, k, vt)


def _mla_latent(q_nope, q_rope, c_all, r_all, w_uk, w_uv, q_pos, k_pos):
    b, T = q_nope.shape[:2]
    q_lat = jnp.einsum('bthd,chd->bthc', q_nope, w_uk)
    sc = (jnp.einsum('bthc,bsc->bhts', q_lat, c_all)
          + jnp.einsum('bthr,bsr->bhts', q_rope, r_all)).astype(jnp.float32) * MLA_SCALE
    sc = jnp.where(k_pos[None, :] <= q_pos[:, None], sc, -jnp.inf)
    pr = jax.nn.softmax(sc, axis=-1).astype(c_all.dtype)
    o_lat = jnp.einsum('bhts,bsc->bthc', pr, c_all)
    return jnp.einsum('bthc,chd->bthd', o_lat, w_uv).reshape(b, T, D_HEADS * D_V)


def _mixer_prompt(h, g_mix, pos, p):
    b, L = h.shape[:2]
    u = _project(h, g_mix, pos, p)
    a_out = _dsa_prompt(u['a_q'], u['a_k'], u['a_v'], u['a_qi'], u['a_ki'], u['a_w'])
    b_out, conv_new, ssm_new = _ssd_prompt(u['b_z'], u['b_xbc'], u['b_dt'], p)
    c_out = _sb_prompt(u['c_q'], u['c_k'], u['c_v'])
    w_ukv = jnp.concatenate([p['d_w_uk'].reshape(KV_RANK, D_HEADS * D_NOPE),
                             p['d_w_uv'].reshape(KV_RANK, D_HEADS * D_V)], axis=1)
    kv_up = _norm_matmul(u['d_ckv_raw'].reshape(b * L, KV_RANK), p['d_kv_norm_g'], w_ukv)
    k_nope = kv_up[:, :D_HEADS * D_NOPE].reshape(b, L, D_HEADS, D_NOPE)
    v_d = kv_up[:, D_HEADS * D_NOPE:].reshape(b, L, D_HEADS, D_V)
    d_out = _mla_prompt(u['d_qn'], u['d_qr'], k_nope, u['d_kr'], v_d)
    mix = jnp.concatenate([a_out, b_out, c_out, d_out], axis=-1)
    new = (u['a_k'], u['a_v'], u['a_ki'], u['c_k'], u['c_v'], u['d_ckv'], u['d_kr'],
           conv_new, ssm_new.astype(h.dtype))
    return mix, new


_PPS = 16
_SUB = 8


def _page_specs(rows, width, layer, n_pages, descending):
    specs = []
    for slot in range(_PPS):
        if descending:
            idx = lambda b, c, pt, slot=slot: (layer, pt[b, n_pages - (c + 1) * _PPS + slot], 0, 0)
        else:
            idx = lambda b, c, pt, slot=slot: (layer, pt[b, c * _PPS + slot], 0, 0)
        specs.append(pl.BlockSpec((None, None, rows, width), idx))
    return specs


def _seq_spec(rows, width):
    return pl.BlockSpec((1, rows, width), lambda b, c, pt: (b, 0, 0))


def _pad_rows(x):
    return jnp.pad(x, ((0, 0), (0, _SUB - x.shape[1]), (0, 0)))


_NT = (((1,), (1,)), ((), ()))


def _paged_call(kernel_fn, page_table, seq_inputs, pools, layer, descending, out_shape, out_spec, scratch, name):
    b, n_pages = page_table.shape
    assert n_pages % _PPS == 0
    in_specs = [_seq_spec(x.shape[1], x.shape[2]) for x in seq_inputs]
    args = list(seq_inputs)
    for pool in pools:
        in_specs += _page_specs(pool.shape[2], pool.shape[3], layer, n_pages, descending)
        args += [pool] * _PPS
    return pl.pallas_call(
        kernel_fn,
        grid_spec=pltpu.PrefetchScalarGridSpec(
            num_scalar_prefetch=1, grid=(b, n_pages // _PPS),
            in_specs=in_specs, out_specs=out_spec, scratch_shapes=scratch),
        out_shape=out_shape,
        compiler_params=pltpu.CompilerParams(dimension_semantics=("parallel", "arbitrary")),
        name=name,
    )(page_table, *args)


def _pool_t(pool):
    lead = pool.shape[:2]
    flat = pool.reshape(lead + (PAGE_SIZE, -1))
    return jnp.swapaxes(flat, 2, 3)


def _suffix_sum(x):
    width = x.shape[1]
    lane = lax.broadcasted_iota(jnp.int32, x.shape, 1)
    k = 1
    while k < width:
        x = x + jnp.where(lane < width - k, pltpu.roll(x, width - k, axis=1), 0.0)
        k *= 2
    return x


def _head_rows(rows):
    pad = jnp.zeros((_SUB - len(rows), rows[0].shape[1]), _F32)
    return jnp.concatenate(rows + [pad], axis=0)


def _sb_decode_kernel(pt_ref, q_ref, *refs, scale, heads, rep):
    k_refs, v_refs = refs[:_PPS], refs[_PPS:2 * _PPS]
    o_ref, tail_sc, acc_sc = refs[2 * _PPS:]
    c = pl.program_id(1)
    d = q_ref.shape[1] // heads

    @pl.when(c == 0)
    def _():
        tail_sc[...] = jnp.zeros_like(tail_sc)
        acc_sc[...] = jnp.zeros_like(acc_sc)

    zs = []
    for slot in range(_PPS):
        kt = k_refs[slot][...]
        rows = []
        for hh in range(heads):
            g = hh // rep
            rows.append(jnp.sum(kt[g * d:(g + 1) * d, :] * q_ref[0, hh * d:(hh + 1) * d, :], axis=0, keepdims=True))
        zs.append(_head_rows(rows))
    z = jnp.concatenate(zs, axis=1) * scale
    incl = _suffix_sum(_softplus(z))
    w = jnp.exp(z - incl - tail_sc[...][:, 0:1])
    for slot in range(_PPS):
        vt = v_refs[slot][...]
        for hh in range(heads):
            g = hh // rep
            acc_sc[hh * d:(hh + 1) * d, :] += (vt[g * d:(g + 1) * d, :]
                                               * w[hh:hh + 1, slot * PAGE_SIZE:(slot + 1) * PAGE_SIZE])
    tail_sc[...] = tail_sc[...] + incl[:, 0:1]

    @pl.when(c == pl.num_programs(1) - 1)
    def _():
        o_ref[0] = jnp.broadcast_to(jnp.sum(acc_sc[...], axis=1, keepdims=True), acc_sc.shape)


def _sb_decode(q, pool_k, pool_v, layer, page_table):
    b, t, groups, rep, d = q.shape
    assert t == 1
    heads = groups * rep
    qb = jnp.broadcast_to(q.reshape(b, heads * d, 1), (b, heads * d, PAGE_SIZE)).astype(_F32)
    out = _paged_call(
        functools.partial(_sb_decode_kernel, scale=d ** -0.5, heads=heads, rep=rep), page_table, [qb],
        [_pool_t(pool_k), _pool_t(pool_v)], layer, True, jax.ShapeDtypeStruct((b, heads * d, PAGE_SIZE), _F32),
        pl.BlockSpec((1, heads * d, PAGE_SIZE), lambda i, c, pt: (i, 0, 0)),
        [pltpu.VMEM((_SUB, PAGE_SIZE), _F32), pltpu.VMEM((heads * d, PAGE_SIZE), _F32)], "sb_decode")
    return out[:, :, 0]


def _softmax_pages(logits, values, m, l, acc):
    mx = logits[0].max(axis=1, keepdims=True)
    for lg in logits[1:]:
        mx = jnp.maximum(mx, lg.max(axis=1, keepdims=True))
    m_new = jnp.maximum(m, mx)
    alpha = jnp.exp(m - m_new)
    l = alpha * l
    acc = alpha[:, :acc.shape[1]] * acc
    for lg, val in zip(logits, values):
        pr = jnp.exp(lg - m_new)
        l = l + jnp.sum(pr, axis=1, keepdims=True)
        acc = acc + jnp.dot(pr.astype(_BF16), val, preferred_element_type=_F32)
    return m_new, l, acc


def _mla_decode_kernel(pt_ref, ql_ref, qr_ref, lnew_ref, cnew_ref, *refs, scale):
    c_refs, r_refs = refs[:_PPS], refs[_PPS:2 * _PPS]
    o_ref, m_sc, l_sc, acc_sc = refs[2 * _PPS:]
    c = pl.program_id(1)

    @pl.when(c == 0)
    def _():
        m_sc[...] = lnew_ref[0]
        l_sc[...] = jnp.ones_like(l_sc)
        acc_sc[...] = jnp.broadcast_to(cnew_ref[0], acc_sc.shape)

    ql, qr = ql_ref[0], qr_ref[0]
    lat = [c_refs[s][...].astype(_BF16) for s in range(_PPS)]
    logits = [(lax.dot_general(ql, lat[s], _NT, preferred_element_type=_F32)
               + jnp.dot(qr, r_refs[s][...].astype(_BF16), preferred_element_type=_F32)) * scale
              for s in range(_PPS)]
    m, l, acc = _softmax_pages(logits, lat, m_sc[...], l_sc[...], acc_sc[...])
    m_sc[...], l_sc[...], acc_sc[...] = m, l, acc

    @pl.when(c == pl.num_programs(1) - 1)
    def _():
        o_ref[0] = acc / l


def _mla_decode(q_nope, q_rope, c_new, r_new, pool_c, pool_r, w_uk, w_uv, layer, page_table):
    b, t, heads, _ = q_nope.shape
    assert t == 1
    q_lat = jnp.einsum('bthd,chd->bhc', q_nope, w_uk)
    qr = q_rope[:, 0]
    l_new = (jnp.einsum('bhc,bc->bh', q_lat, c_new[:, 0]) + jnp.einsum('bhr,br->bh', qr, r_new[:, 0])) * MLA_SCALE
    l_new = jnp.broadcast_to(_pad_rows(l_new[:, :, None]), (b, _SUB, PAGE_SIZE)).astype(_F32)
    rank = pool_c.shape[-1]
    o_lat = _paged_call(
        functools.partial(_mla_decode_kernel, scale=MLA_SCALE), page_table,
        [_pad_rows(q_lat).astype(_BF16), _pad_rows(qr).astype(_BF16), l_new, c_new.astype(_F32)],
        [pool_c, _pool_t(pool_r)], layer, False, jax.ShapeDtypeStruct((b, _SUB, rank), _F32),
        pl.BlockSpec((1, _SUB, rank), lambda i, c, pt: (i, 0, 0)),
        [pltpu.VMEM((_SUB, PAGE_SIZE), _F32), pltpu.VMEM((_SUB, PAGE_SIZE), _F32), pltpu.VMEM((_SUB, rank), _F32)],
        "mla_decode")
    return jnp.einsum('bhc,chd->bhd', o_lat[:, :heads], w_uv).reshape(b, t, heads * w_uv.shape[-1])


def _idx_decode_kernel(pt_ref, qi_ref, w_ref, *refs):
    ki_refs, o_ref = refs[:_PPS], refs[_PPS]
    qi, w = qi_ref[0], w_ref[0]
    for s in range(_PPS):
        dots = jnp.dot(qi, ki_refs[s][...].astype(_BF16), preferred_element_type=_F32)
        o_ref[0, s:s + 1, :] = jnp.sum(w * jnp.maximum(dots, 0.0), axis=0, keepdims=True) * IDX_SCALE


def _dsa_decode_kernel(pt_ref, sc_ref, scnew_ref, q_ref, lnew_ref, vnew_ref, *refs, n_sel, heads):
    k_refs, v_refs = refs[:_PPS], refs[_PPS:2 * _PPS]
    o_ref, bias_sc, m_sc, l_sc, acc_sc = refs[2 * _PPS:]
    c = pl.program_id(1)
    n_rows = sc_ref.shape[1]

    @pl.when(c == 0)
    def _():
        key = _sortable_key(sc_ref[0])
        key_new = _sortable_key(scnew_ref[0])

        def count(pred_tile, pred_new):
            return jnp.sum(jnp.where(pred_tile, 1, 0)) + jnp.max(jnp.where(pred_new, 1, 0))

        def bit_step(i, carry):
            t, cnt_t = carry
            cand = t ^ lax.shift_left(jnp.int32(1), 31 - i)
            cnt = count(key >= cand, key_new >= cand)
            ok = cnt >= n_sel
            return jnp.where(ok, cand, t), jnp.where(ok, cnt, cnt_t)

        thr, cnt_thr = lax.fori_loop(0, 32, bit_step, (jnp.int32(_INT_MIN), jnp.int32(n_rows * PAGE_SIZE + 1)))
        bias_sc[...] = jnp.where(key >= thr, 0.0, _NEG_BIG)
        sel_new = key_new >= thr

        def init(sel_new):
            m_sc[...] = jnp.where(sel_new, lnew_ref[0], _NEG_BIG)
            l_sc[...] = jnp.where(sel_new, 1.0, 0.0) + jnp.zeros_like(l_sc)
            acc_sc[...] = jnp.where(sel_new, vnew_ref[0], 0.0)

        init(sel_new)

        @pl.when(cnt_thr > n_sel)
        def _():
            room = (n_sel - count(key > thr, key_new > thr)).astype(_F32)
            eq = key == thr
            eqf = jnp.where(eq, 1.0, 0.0)
            r_i = lax.broadcasted_iota(jnp.int32, (PAGE_SIZE, PAGE_SIZE), 0)
            c_i = lax.broadcasted_iota(jnp.int32, (PAGE_SIZE, PAGE_SIZE), 1)
            in_row = jnp.dot(eqf.astype(_BF16), jnp.where(r_i < c_i, 1.0, 0.0).astype(_BF16),
                             preferred_element_type=_F32)
            row_tot = jnp.broadcast_to(jnp.sum(eqf, axis=1, keepdims=True), eqf.shape).astype(_BF16)
            rr = lax.broadcasted_iota(jnp.int32, (n_rows, n_rows), 0)
            rc = lax.broadcasted_iota(jnp.int32, (n_rows, n_rows), 1)
            rows_before = jnp.dot(jnp.where(rc < rr, 1.0, 0.0).astype(_BF16), row_tot, preferred_element_type=_F32)
            keep = (key > thr) | (eq & (in_row + rows_before < room))
            bias_sc[...] = jnp.where(keep, 0.0, _NEG_BIG)
            init((key_new > thr) | ((key_new == thr) & (jnp.sum(eqf) < room)))

    d = k_refs[0].shape[0]
    lg = []
    for s in range(_PPS):
        kt = k_refs[s][...]
        rows = [jnp.sum(kt * q_ref[0, hh * d:(hh + 1) * d, :], axis=0, keepdims=True) for hh in range(heads)]
        lg.append(_head_rows(rows) * (d ** -0.5) + bias_sc[pl.ds(c * _PPS + s, 1), :])
    logit = jnp.concatenate(lg, axis=1)
    m_old = m_sc[...][:, 0:1]
    m_new = jnp.maximum(m_old, jnp.max(logit, axis=1, keepdims=True))
    alpha = jnp.exp(m_old - m_new)
    pr = jnp.exp(logit - m_new)
    l_new = alpha * l_sc[...][:, 0:1] + jnp.sum(pr, axis=1, keepdims=True)
    for hh in range(heads):
        acc_h = alpha[hh:hh + 1, :] * acc_sc[hh * d:(hh + 1) * d, :]
        for s in range(_PPS):
            acc_h = acc_h + v_refs[s][...] * pr[hh:hh + 1, s * PAGE_SIZE:(s + 1) * PAGE_SIZE]
        acc_sc[hh * d:(hh + 1) * d, :] = acc_h
    m_sc[...] = jnp.broadcast_to(m_new, m_sc.shape)
    l_sc[...] = jnp.broadcast_to(l_new, l_sc.shape)

    @pl.when(c == pl.num_programs(1) - 1)
    def _():
        for hh in range(heads):
            tot = jnp.sum(acc_sc[hh * d:(hh + 1) * d, :], axis=1, keepdims=True) / l_new[hh:hh + 1, :]
            o_ref[0, hh * d:(hh + 1) * d, :] = jnp.broadcast_to(tot, (d, PAGE_SIZE))


def _dsa_decode(q, k_new, v_new, q_idx, ki_new, w_idx, pool_k, pool_v, pool_ki, layer, page_table):
    b, t = q.shape[:2]
    assert t == 1
    heads, d = q.shape[3], q.shape[4]
    ih, e = q_idx.shape[2], q_idx.shape[3]
    n_pages = page_table.shape[1]
    n_sel = min(TOPK_MAX, (n_pages * PAGE_SIZE + t) // 4)
    assert ih == _SUB
    w8 = jnp.broadcast_to(w_idx[:, 0, :, None], (b, ih, PAGE_SIZE)).astype(_F32)
    scores = _paged_call(
        _idx_decode_kernel, page_table, [q_idx[:, 0].astype(_BF16), w8], [_pool_t(pool_ki)], layer, False,
        jax.ShapeDtypeStruct((b, n_pages, PAGE_SIZE), _F32),
        pl.BlockSpec((1, _PPS, PAGE_SIZE), lambda i, c, pt: (i, c, 0)), [], "idx_decode")
    rel_new = jax.nn.relu(jnp.einsum('bhe,be->bh', q_idx[:, 0], ki_new[:, 0]))
    sc_new = jnp.einsum('bh,bh->b', w_idx[:, 0], rel_new) * IDX_SCALE
    sc_new = jnp.broadcast_to(sc_new[:, None, None], (b, 1, PAGE_SIZE)).astype(_F32)
    qh = q.reshape(b, heads, d)
    l_new = jnp.einsum('bhd,bd->bh', qh, k_new.reshape(b, d)) * d ** -0.5
    l_new = jnp.broadcast_to(_pad_rows(l_new[:, :, None]), (b, _SUB, PAGE_SIZE)).astype(_F32)
    qb = jnp.broadcast_to(qh.reshape(b, heads * d, 1), (b, heads * d, PAGE_SIZE)).astype(_F32)
    v_rep = jnp.tile(v_new.reshape(b, d), (1, heads))
    v_lane0 = jnp.zeros((b, heads * d, PAGE_SIZE), _F32).at[:, :, 0].set(v_rep)
    out = _paged_call(
        functools.partial(_dsa_decode_kernel, n_sel=n_sel, heads=heads), page_table,
        [scores, sc_new, qb, l_new, v_lane0],
        [_pool_t(pool_k), _pool_t(pool_v)], layer, False, jax.ShapeDtypeStruct((b, heads * d, PAGE_SIZE), _F32),
        pl.BlockSpec((1, heads * d, PAGE_SIZE), lambda i, c, pt: (i, 0, 0)),
        [pltpu.VMEM((n_pages, PAGE_SIZE), _F32), pltpu.VMEM((_SUB, PAGE_SIZE), _F32),
         pltpu.VMEM((_SUB, PAGE_SIZE), _F32), pltpu.VMEM((heads * d, PAGE_SIZE), _F32)], "dsa_decode")
    return out[:, :, 0].reshape(b, t, heads * d)


def _mixer_sample(h, g_mix, pos, p, l, caches, page_table, past_len):
    (ca_k, ca_v, ca_ki, cc_k, cc_v, cd_ckv, cd_kr, sb_conv, sb_ssm) = caches
    hn = h
    b, T = h.shape[:2]
    L = past_len + T
    k_pos = jnp.arange(L)
    u = _project(h, g_mix, pos, p)
    a_out = _dsa_decode(u['a_q'], u['a_k'], u['a_v'], u['a_qi'], u['a_ki'], u['a_w'], ca_k, ca_v, ca_ki, l, page_table)
    b_out, conv_new, ssm_new = _ssd_mixer(u['b_z'], u['b_xbc'], u['b_dt'], sb_conv[l], sb_ssm[l], p)
    c_out = _sb_decode(u['c_q'], cc_k, cc_v, l, page_table).reshape(b, T, C_HEADS * HEAD_DIM)
    d_out = _mla_decode(u['d_qn'], u['d_qr'], u['d_ckv'], u['d_kr'], cd_ckv, cd_kr, p['d_w_uk'], p['d_w_uv'],
                        l, page_table)
    mix = jnp.concatenate([a_out, b_out, c_out, d_out], axis=-1)
    new = (u['a_k'], u['a_v'], u['a_ki'], u['c_k'], u['c_v'], u['d_ckv'], u['d_kr'],
           conv_new.astype(sb_conv.dtype), ssm_new.astype(sb_ssm.dtype))
    return mix, new


def _mem_kv(mem, p):
    b, mt, dm = mem.shape
    kv = _norm_matmul(mem.reshape(b * mt, dm), p['g_mem_kv'], jnp.concatenate([p['w_mk'], p['w_mv']], axis=1))
    width = MEM_HEADS * MEM_HD
    return kv[:, :width].reshape(b, mt, MEM_HEADS, MEM_HD), kv[:, width:].reshape(b, mt, MEM_HEADS, MEM_HD)


def _post_mix_prompt(h, mix, mem_k, mem_v, p):
    b, L, dm = h.shape
    h2 = _mix_mem(h.reshape(b * L, dm), mix.reshape(b * L, dm), mem_k, mem_v, p, rows_per_batch=L)
    return _ffn(h2, p['g_ffn'], p['w_gate'], p['w_up'], p['w_down']).reshape(b, L, dm)


def _post_mix_sample(h, mix, mem_k, mem_v, p):
    b, L, dm = h.shape
    h1 = _matmul_residual(mix.reshape(b * L, dm), p['w_out'], h.reshape(b * L, dm))
    q = _norm_matmul(h1, p['g_mem_q'], p['w_mq']).reshape(b, L, MEM_HEADS, MEM_HD)
    sc = jnp.einsum('blhd,bmhd->bhlm', q, mem_k.astype(q.dtype)).astype(jnp.float32) * MEM_HD ** -0.5
    pr = jax.nn.softmax(sc, axis=-1).astype(q.dtype)
    o = jnp.einsum('bhlm,bmhd->blhd', pr, mem_v.astype(q.dtype)).reshape(b * L, MEM_HEADS * MEM_HD)
    h2 = _matmul_residual(o, p['w_mo'], h1)
    return _ffn(h2, p['g_ffn'], p['w_gate'], p['w_up'], p['w_down']).reshape(b, L, dm)


def _final_norm_kernel(x_ref, g_ref, o_ref):
    x = x_ref[...]
    o_ref[...] = x * lax.rsqrt(jnp.mean(x * x, axis=-1, keepdims=True) + NORM_EPS) * g_ref[...]


def _final_norm(x, g):
    shp = x.shape
    x2 = x.reshape(-1, shp[-1])
    n = x2.shape[0]
    tm = min(n, 1024)
    out = pl.pallas_call(
        _final_norm_kernel,
        grid=(n // tm,),
        in_specs=[pl.BlockSpec((tm, shp[-1]), lambda i: (i, 0)), pl.BlockSpec((1, shp[-1]), lambda i: (0, 0))],
        out_specs=pl.BlockSpec((tm, shp[-1]), lambda i: (i, 0)),
        out_shape=jax.ShapeDtypeStruct(x2.shape, x2.dtype),
    )(x2, g.reshape(1, -1))
    return out.reshape(shp)


def kernel(x_prompt, x_sample, mem_prompt, cache_a_k, cache_a_v, cache_a_kidx, cache_c_k, cache_c_v, cache_d_ckv, cache_d_krope, state_b_conv, state_b_ssm, cache_mem_k, cache_mem_v, page_table, g_mix, w_in, b_conv_w, b_conv_b, b_dt_bias, b_a_log, b_d, b_norm_g, d_q_norm_g, d_kv_norm_g, d_w_uq, d_w_uk, d_w_uv, w_out, g_mem_q, g_mem_kv, w_mq, w_mk, w_mv, w_mo, g_ffn, w_gate, w_up, w_down, g_final):
    s_prompt = x_prompt.shape[1]
    t_new = x_sample.shape[1]
    past_len = page_table.shape[1] * PAGE_SIZE
    pos_p = jnp.arange(s_prompt)
    pos_s = past_len + jnp.arange(t_new)
    caches = (cache_a_k, cache_a_v, cache_a_kidx, cache_c_k, cache_c_v, cache_d_ckv, cache_d_krope,
              state_b_conv, state_b_ssm)
    hp, hs = x_prompt, x_sample
    new_p, new_s, mem_ks, mem_vs = [], [], [], []
    for l in range(DEPTH):
        p = {'w_in': w_in[l], 'b_conv_w': b_conv_w[l], 'b_conv_b': b_conv_b[l], 'b_dt_bias': b_dt_bias[l],
             'b_a_log': b_a_log[l], 'b_d': b_d[l], 'b_norm_g': b_norm_g[l],
             'd_q_norm_g': d_q_norm_g[l], 'd_kv_norm_g': d_kv_norm_g[l], 'd_w_uq': d_w_uq[l],
             'd_w_uk': d_w_uk[l], 'd_w_uv': d_w_uv[l], 'w_out': w_out[l],
             'g_mem_q': g_mem_q[l], 'g_mem_kv': g_mem_kv[l], 'w_mq': w_mq[l], 'w_mk': w_mk[l],
             'w_mv': w_mv[l], 'w_mo': w_mo[l], 'g_ffn': g_ffn[l], 'w_gate': w_gate[l],
             'w_up': w_up[l], 'w_down': w_down[l]}
        mix_p, st_p = _mixer_prompt(hp, g_mix[l], pos_p, p)
        mk, mv = _mem_kv(mem_prompt, p)
        hp = _post_mix_prompt(hp, mix_p, mk, mv, p)
        new_p.append(st_p)
        mem_ks.append(mk)
        mem_vs.append(mv)
        mix_s, st_s = _mixer_sample(hs, g_mix[l], pos_s, p, l, caches, page_table, past_len)
        hs = _post_mix_sample(hs, mix_s, cache_mem_k[l], cache_mem_v[l], p)
        new_s.append(st_s)
    y_prompt = _final_norm(hp, g_final)
    y_sample = _final_norm(hs, g_final)
    (p_a_k, p_a_v, p_a_kidx, p_c_k, p_c_v, p_d_ckv, p_d_krope, p_b_conv, p_b_ssm) = [jnp.stack(t) for t in zip(*new_p)]
    (s_a_k, s_a_v, s_a_kidx, s_c_k, s_c_v, s_d_ckv, s_d_krope, s_b_conv, s_b_ssm) = [jnp.stack(t) for t in zip(*new_s)]
    p_mem_k = jnp.stack(mem_ks)
    p_mem_v = jnp.stack(mem_vs)
    return (y_prompt, y_sample,
            p_a_k, p_a_v, p_a_kidx, p_c_k, p_c_v, p_d_ckv, p_d_krope, p_b_conv, p_b_ssm, p_mem_k, p_mem_v,
            s_a_k, s_a_v, s_a_kidx, s_c_k, s_c_v, s_d_ckv, s_d_krope, s_b_conv, s_b_ssm)
```

```python
import functools
import math
import jax
import jax.numpy as jnp
from jax import lax
import numpy as np
from jax.experimental import pallas as pl
from jax.experimental.pallas import tpu as pltpu


D_MODEL = 1024
BATCH = 32
SEQ = 2048
DEPTH = 2
DEC_BATCH = 128
DEC_SEQ = 1
PAST_LEN = 16384
PAGE_SIZE = 128

N_MIXERS = 4
GROUP_WIDTH = D_MODEL // N_MIXERS
MIX_WIDTH = N_MIXERS * GROUP_WIDTH
HEAD_DIM = 64
ROPE_THETA = 10000.0
NORM_EPS = 1e-6
Q_BLOCK = 128

A_HEADS = GROUP_WIDTH // HEAD_DIM
A_KV_HEADS = 1
IDX_HEADS = 8
IDX_DIM = 32
TOPK_MAX = 256
IDX_SCALE = (IDX_HEADS * IDX_DIM) ** -0.5

SSM_P = HEAD_DIM
SSM_HEADS = GROUP_WIDTH // SSM_P
SSM_INNER = SSM_HEADS * SSM_P
SSM_GROUPS = 2
D_STATE = 128
CONV_W = 4
CONV_CH = SSM_INNER + 2 * SSM_GROUPS * D_STATE
SSD_CHUNK = 128

C_HEADS = GROUP_WIDTH // HEAD_DIM
C_KV_HEADS = 2

D_HEADS = GROUP_WIDTH // HEAD_DIM
D_NOPE = 64
D_ROPE = 32
D_V = GROUP_WIDTH // D_HEADS
Q_RANK = 256
KV_RANK = 128
MLA_SCALE = (D_NOPE + D_ROPE) ** -0.5

MEM_TOKENS = 256
MEM_HEADS = 4
MEM_HD = 64

D_FF = -(-8 * D_MODEL // (3 * 256)) * 256

IN_SIZES = (A_HEADS * HEAD_DIM, A_KV_HEADS * HEAD_DIM, A_KV_HEADS * HEAD_DIM, IDX_HEADS * IDX_DIM, IDX_DIM, IDX_HEADS,
            SSM_INNER, CONV_CH, SSM_HEADS,
            C_HEADS * HEAD_DIM, C_KV_HEADS * HEAD_DIM, C_KV_HEADS * HEAD_DIM,
            Q_RANK, KV_RANK, D_ROPE)
IN_TOTAL = sum(IN_SIZES)


def _in_offsets():
    return [int(o) for o in np.cumsum(IN_SIZES)[:-1]]


def _rmsnorm(x, g):
    xf = x.astype(jnp.float32)
    y = xf * lax.rsqrt(jnp.mean(xf * xf, axis=-1, keepdims=True) + NORM_EPS)
    return (y * g.astype(jnp.float32)).astype(x.dtype)


def _rope(x, pos):
    half = x.shape[-1] // 2
    inv = 1.0 / (ROPE_THETA ** (jnp.arange(half, dtype=jnp.float32) / half))
    ang = pos.astype(jnp.float32)[:, None] * inv[None, :]
    ang = ang.reshape((ang.shape[0],) + (1,) * (x.ndim - 3) + (half,))
    cos, sin = jnp.cos(ang), jnp.sin(ang)
    xf = x.astype(jnp.float32)
    x1, x2 = xf[..., :half], xf[..., half:]
    return jnp.concatenate([x1 * cos - x2 * sin, x2 * cos + x1 * sin], axis=-1).astype(x.dtype)


_LANES = 128
_VMEM_LIMIT = 48 * 1024 * 1024


def _row_tile(n, cap):
    tm = min(n, cap)
    assert n % tm == 0
    return tm


def _rms(x, g):
    return x * lax.rsqrt(jnp.mean(x * x, axis=-1, keepdims=True) + NORM_EPS) * g


def _norm_matmul_kernel(x_ref, g_ref, w_ref, o_ref):
    o_ref[...] = jnp.dot(_rms(x_ref[...], g_ref[...]).astype(_BF16), w_ref[...], preferred_element_type=_F32)


def _norm_matmul(x, g, w, tm_cap=256):
    n, k = x.shape
    m = w.shape[1]
    mp = -(-m // _LANES) * _LANES
    wb = jnp.pad(w.astype(_BF16), ((0, 0), (0, mp - m)))
    tm = _row_tile(n, tm_cap)
    out = pl.pallas_call(
        _norm_matmul_kernel,
        grid=(n // tm,),
        in_specs=[pl.BlockSpec((tm, k), lambda i: (i, 0)), pl.BlockSpec((1, k), lambda i: (0, 0)),
                  pl.BlockSpec((k, mp), lambda i: (0, 0))],
        out_specs=pl.BlockSpec((tm, mp), lambda i: (i, 0)),
        out_shape=jax.ShapeDtypeStruct((n, mp), _F32),
        compiler_params=pltpu.CompilerParams(dimension_semantics=("parallel",), vmem_limit_bytes=_VMEM_LIMIT),
        name="norm_matmul",
    )(x, g.reshape(1, k).astype(_F32), wb)
    return out[:, :m] if mp != m else out


def _matmul_residual_kernel(x_ref, w_ref, r_ref, o_ref):
    o_ref[...] = r_ref[...] + jnp.dot(x_ref[...].astype(_BF16), w_ref[...], preferred_element_type=_F32)


def _matmul_residual(x, w, res, tm_cap=256):
    n, k = x.shape
    m = w.shape[1]
    tm = _row_tile(n, tm_cap)
    return pl.pallas_call(
        _matmul_residual_kernel,
        grid=(n // tm,),
        in_specs=[pl.BlockSpec((tm, k), lambda i: (i, 0)), pl.BlockSpec((k, m), lambda i: (0, 0)),
                  pl.BlockSpec((tm, m), lambda i: (i, 0))],
        out_specs=pl.BlockSpec((tm, m), lambda i: (i, 0)),
        out_shape=jax.ShapeDtypeStruct((n, m), _F32),
        compiler_params=pltpu.CompilerParams(dimension_semantics=("parallel",), vmem_limit_bytes=_VMEM_LIMIT),
        name="matmul_residual",
    )(x, w.astype(_BF16), res)


_FF_SPLIT = 2


def _ffn_kernel(h_ref, g_ref, wg_ref, wu_ref, wd_ref, o_ref, hn_sc, acc_sc):
    k = pl.program_id(1)

    @pl.when(k == 0)
    def _():
        h = h_ref[...]
        hn_sc[...] = _rms(h, g_ref[...]).astype(_BF16)
        acc_sc[...] = h

    hn = hn_sc[...]
    gate = jnp.dot(hn, wg_ref[...], preferred_element_type=_F32)
    up = jnp.dot(hn, wu_ref[...], preferred_element_type=_F32)
    act = (gate * jax.nn.sigmoid(gate) * up).astype(_BF16)
    acc_sc[...] += jnp.dot(act, wd_ref[...], preferred_element_type=_F32)

    @pl.when(k == pl.num_programs(1) - 1)
    def _():
        o_ref[...] = acc_sc[...]


def _ffn(h, g, w_gate, w_up, w_down, tm_cap=512):
    n, dm = h.shape
    ff = w_gate.shape[1]
    tf = ff // _FF_SPLIT
    assert tf * _FF_SPLIT == ff and tf % _LANES == 0
    tm = _row_tile(n, tm_cap)
    return pl.pallas_call(
        _ffn_kernel,
        grid=(n // tm, _FF_SPLIT),
        in_specs=[pl.BlockSpec((tm, dm), lambda i, k: (i, 0)), pl.BlockSpec((1, dm), lambda i, k: (0, 0)),
                  pl.BlockSpec((dm, tf), lambda i, k: (0, k)), pl.BlockSpec((dm, tf), lambda i, k: (0, k)),
                  pl.BlockSpec((tf, dm), lambda i, k: (k, 0))],
        out_specs=pl.BlockSpec((tm, dm), lambda i, k: (i, 0)),
        out_shape=jax.ShapeDtypeStruct((n, dm), _F32),
        scratch_shapes=[pltpu.VMEM((tm, dm), _BF16), pltpu.VMEM((tm, dm), _F32)],
        compiler_params=pltpu.CompilerParams(dimension_semantics=("parallel", "arbitrary"),
                                             vmem_limit_bytes=_VMEM_LIMIT),
        name="ffn",
    )(h, g.reshape(1, dm).astype(_F32), w_gate.astype(_BF16), w_up.astype(_BF16), w_down.astype(_BF16))


def _mix_mem_kernel(h_ref, mix_ref, wout_ref, gq_ref, wmq_ref, mk_ref, mvt_ref, wmo_ref, o_ref):
    heads, hd = mk_ref.shape[1], mk_ref.shape[3]
    h1 = h_ref[...] + jnp.dot(mix_ref[...].astype(_BF16), wout_ref[...], preferred_element_type=_F32)
    q = jnp.dot(_rms(h1, gq_ref[...]).astype(_BF16), wmq_ref[...], preferred_element_type=_F32)
    outs = []
    for hh in range(heads):
        qh = q[:, hh * hd:(hh + 1) * hd].astype(_BF16)
        sc = lax.dot_general(mk_ref[0, hh], qh, (((1,), (1,)), ((), ())), preferred_element_type=_F32) * (hd ** -0.5)
        p = jnp.exp(sc - jnp.max(sc, axis=0, keepdims=True))
        o_t = jnp.dot(mvt_ref[0, hh], p.astype(_BF16), preferred_element_type=_F32)
        outs.append(o_t / jnp.sum(p, axis=0, keepdims=True))
    o = jnp.concatenate(outs, axis=0).T
    o_ref[...] = h1 + jnp.dot(o.astype(_BF16), wmo_ref[...], preferred_element_type=_F32)


def _mix_mem(h, mix, mem_k, mem_v, p, rows_per_batch, tm_cap=256):
    n, dm = h.shape
    b, mt, heads, hd = mem_k.shape
    tm = _row_tile(rows_per_batch, tm_cap)
    per = rows_per_batch // tm
    mk = jnp.transpose(mem_k, (0, 2, 1, 3)).astype(_BF16)
    mvt = jnp.transpose(mem_v, (0, 2, 3, 1)).astype(_BF16)
    const = lambda i: (0, 0)
    return pl.pallas_call(
        _mix_mem_kernel,
        grid=(n // tm,),
        in_specs=[pl.BlockSpec((tm, dm), lambda i: (i, 0)), pl.BlockSpec((tm, dm), lambda i: (i, 0)),
                  pl.BlockSpec((dm, dm), const), pl.BlockSpec((1, dm), const),
                  pl.BlockSpec((dm, heads * hd), const),
                  pl.BlockSpec((1, heads, mt, hd), lambda i: (i // per, 0, 0, 0)),
                  pl.BlockSpec((1, heads, hd, mt), lambda i: (i // per, 0, 0, 0)),
                  pl.BlockSpec((heads * hd, dm), const)],
        out_specs=pl.BlockSpec((tm, dm), lambda i: (i, 0)),
        out_shape=jax.ShapeDtypeStruct((n, dm), _F32),
        compiler_params=pltpu.CompilerParams(dimension_semantics=("parallel",), vmem_limit_bytes=_VMEM_LIMIT),
        name="mix_mem",
    )(h, mix, p['w_out'].astype(_BF16), p['g_mem_q'].reshape(1, dm).astype(_F32), p['w_mq'].astype(_BF16),
      mk, mvt, p['w_mo'].astype(_BF16))


def _gather_pages(pool, l, page_table):
    g = pool[l, page_table]
    return g.reshape((g.shape[0], g.shape[1] * g.shape[2]) + g.shape[3:])


def _gather_rows(pool, l, page_table, new, idx, past_len):
    page = jnp.clip(idx // PAGE_SIZE, 0, page_table.shape[1] - 1)
    phys = jnp.take_along_axis(page_table, page, axis=1)
    rows = pool[l, phys, idx % PAGE_SIZE]
    extra = (1,) * (new.ndim - 2)
    j = jnp.clip(idx - past_len, 0, new.shape[1] - 1)
    fresh = jnp.take_along_axis(new, j.reshape(j.shape + extra), axis=1)
    return jnp.where((idx < past_len).reshape(idx.shape + extra), rows.astype(new.dtype), fresh)


def _norm_matmul_multi_kernel(x_ref, g_ref, w_ref, *o_refs):
    y = jnp.dot(_rms(x_ref[...], g_ref[...]).astype(_BF16), w_ref[...], preferred_element_type=_F32)
    off = 0
    for o_ref in o_refs:
        o_ref[...] = y[:, off:off + o_ref.shape[1]]
        off += o_ref.shape[1]


def _norm_matmul_multi(x, g, w_groups, tm_cap=256):
    n, k = x.shape
    pads = [-(-w.shape[1] // _LANES) * _LANES for w in w_groups]
    wb = jnp.concatenate([jnp.pad(w.astype(_BF16), ((0, 0), (0, mp - w.shape[1]))) for w, mp in zip(w_groups, pads)],
                         axis=1)
    tm = _row_tile(n, tm_cap)
    return pl.pallas_call(
        _norm_matmul_multi_kernel,
        grid=(n // tm,),
        in_specs=[pl.BlockSpec((tm, k), lambda i: (i, 0)), pl.BlockSpec((1, k), lambda i: (0, 0)),
                  pl.BlockSpec((k, sum(pads)), lambda i: (0, 0))],
        out_specs=[pl.BlockSpec((tm, mp), lambda i: (i, 0)) for mp in pads],
        out_shape=[jax.ShapeDtypeStruct((n, mp), _F32) for mp in pads],
        compiler_params=pltpu.CompilerParams(dimension_semantics=("parallel",), vmem_limit_bytes=_VMEM_LIMIT),
        name="norm_matmul_multi",
    )(x, g.reshape(1, k).astype(_F32), wb)


def _project(h, g_mix, pos, p):
    b, L, dm = h.shape
    cols = jnp.split(p['w_in'], _in_offsets(), axis=1)
    cat = lambda ids: jnp.concatenate([cols[i] for i in ids], axis=1)
    outs = _norm_matmul_multi(h.reshape(b * L, dm), g_mix,
                              [cols[0], cols[3], cols[6], cols[7], cols[9], cols[10], cols[11], cols[12], cols[13],
                               cat([1, 2]), cat([4, 14, 5, 8])])
    a_q, a_qi, b_z, b_xbc, c_q, c_k, c_v, d_cq, d_ckv, kv_a, small = [o.reshape(b, L, -1) for o in outs]
    a_k, a_v = kv_a[..., :HEAD_DIM], kv_a[..., HEAD_DIM:2 * HEAD_DIM]
    o1, o2, o3 = IDX_DIM, IDX_DIM + D_ROPE, IDX_DIM + D_ROPE + IDX_HEADS
    a_ki, d_kr, a_w, b_dt = small[..., :o1], small[..., o1:o2], small[..., o2:o3], small[..., o3:o3 + SSM_HEADS]
    q_d = _norm_matmul(d_cq.reshape(b * L, Q_RANK), p['d_q_norm_g'], p['d_w_uq']).reshape(
        b, L, D_HEADS, D_NOPE + D_ROPE)
    return {
        'd_ckv_raw': d_ckv,
        'a_q': _rope(a_q.reshape(b, L, A_KV_HEADS, A_HEADS // A_KV_HEADS, HEAD_DIM), pos),
        'a_k': _rope(a_k.reshape(b, L, A_KV_HEADS, HEAD_DIM), pos),
        'a_v': a_v.reshape(b, L, A_KV_HEADS, HEAD_DIM),
        'a_qi': _rope(a_qi.reshape(b, L, IDX_HEADS, IDX_DIM), pos),
        'a_ki': _rope(a_ki, pos),
        'a_w': a_w,
        'b_z': b_z, 'b_xbc': b_xbc, 'b_dt': b_dt,
        'c_q': c_q.reshape(b, L, C_KV_HEADS, C_HEADS // C_KV_HEADS, HEAD_DIM),
        'c_k': c_k.reshape(b, L, C_KV_HEADS, HEAD_DIM),
        'c_v': c_v.reshape(b, L, C_KV_HEADS, HEAD_DIM),
        'd_qn': q_d[..., :D_NOPE],
        'd_qr': _rope(q_d[..., D_NOPE:], pos),
        'd_ckv': _rmsnorm(d_ckv, p['d_kv_norm_g']),
        'd_kr': _rope(d_kr, pos),
    }


def _indexer_scores(q_idx, w_idx, k_idx, q_pos, k_pos):
    rel = jax.nn.relu(jnp.einsum('bqhe,bse->bqhs', q_idx, k_idx).astype(jnp.float32))
    sc = jnp.einsum('bqh,bqhs->bqs', w_idx.astype(jnp.float32), rel) * IDX_SCALE
    return jnp.where(k_pos[None, None, :] <= q_pos[None, :, None], sc, -jnp.inf)


def _sparse_attend(q, k_sel, v_sel, valid):
    sc = jnp.einsum('btgrd,btkgd->btgrk', q, k_sel).astype(jnp.float32) * HEAD_DIM ** -0.5
    sc = jnp.where(valid[:, :, None, None, :], sc, -jnp.inf)
    pr = jax.nn.softmax(sc, axis=-1).astype(v_sel.dtype)
    return jnp.einsum('btgrk,btkgd->btgrd', pr, v_sel)


_BF16 = jnp.bfloat16
_F32 = jnp.float32
_INT_MIN = -2 ** 31
_NEG_BIG = -1e30
_TQ = 256
_TK = 128


def _sortable_key(x):
    u = lax.bitcast_convert_type(x, jnp.int32)
    return jnp.where(u < 0, -(u & 0x7FFFFFFF), u)


def _loop_pairs(n_pairs, body, init):
    return lax.fori_loop(0, n_pairs, lambda i, cr: body(2 * i + 1, body(2 * i, cr)), init)


def _kth_largest_key(key_sc, nchunks, n_sel):
    assert (_TQ // _TK) % 2 == 0

    def count_ge(cand):
        def body(c, cnt):
            kk = key_sc[pl.ds(pl.multiple_of(c * _TK, _TK), _TK), :]
            return cnt + jnp.where(kk >= cand, 1, 0)
        cnt = _loop_pairs(nchunks // 2, body, jnp.zeros((_TK, _TQ), jnp.int32))
        return jnp.sum(cnt, axis=0, keepdims=True)

    def bit_step(carry):
        i, t, cnt_t, _ = carry
        cand = t ^ lax.shift_left(jnp.int32(1), 31 - i)
        cnt = count_ge(cand)
        ok = cnt >= n_sel
        cnt_t = jnp.where(ok, cnt, cnt_t)
        return i + 1, jnp.where(ok, cand, t), cnt_t, jnp.max(jnp.where(cnt_t != n_sel, 1, 0))

    t0 = jnp.full((1, _TQ), _INT_MIN, jnp.int32)
    c0 = jnp.zeros((1, _TQ), jnp.int32) + nchunks * _TK
    unsettled0 = jnp.where(nchunks * _TK != n_sel, 1, 0).astype(jnp.int32)
    _, t, cnt_t, _ = lax.while_loop(lambda cr: (cr[0] < 32) & (cr[3] > 0), bit_step,
                                    (jnp.int32(0), t0, c0, unsettled0))
    return t, cnt_t


def _drop_late_ties(key_sc, nchunks, thr, n_sel):
    def count_gt(c, cnt):
        kk = key_sc[pl.ds(pl.multiple_of(c * _TK, _TK), _TK), :]
        return cnt + jnp.where(kk > thr, 1, 0)
    n_gt = jnp.sum(lax.fori_loop(0, nchunks, count_gt, jnp.zeros((_TK, _TQ), jnp.int32)), axis=0, keepdims=True)
    room = (n_sel - n_gt).astype(_F32)
    row = lax.broadcasted_iota(jnp.int32, (_TK, _TK), 0)
    col = lax.broadcasted_iota(jnp.int32, (_TK, _TK), 1)
    before = jnp.where(col < row, 1.0, 0.0).astype(_BF16)

    def body(c, seen):
        sl = pl.ds(pl.multiple_of(c * _TK, _TK), _TK)
        kk = key_sc[sl, :]
        eq = kk == thr
        eqf = jnp.where(eq, 1.0, 0.0)
        rank = jnp.dot(before, eqf.astype(_BF16), preferred_element_type=_F32) + seen
        key_sc[sl, :] = jnp.where(eq & (rank >= room), _INT_MIN, kk)
        return seen + jnp.sum(eqf, axis=0, keepdims=True)

    lax.fori_loop(0, nchunks, body, jnp.zeros((1, _TQ), _F32))


def _dsa_prompt_kernel(q_ref, qi_ref, wt_ref, ki_ref, k_ref, vt_ref, o_ref, key_sc, *, n_sel):
    j = pl.program_id(1)
    nchunks = (j + 1) * (_TQ // _TK)
    heads, d = q_ref.shape[1], q_ref.shape[3]
    ih, e = qi_ref.shape[1], qi_ref.shape[3]
    qidx = qi_ref[0].reshape(ih * _TQ, e)
    wt = wt_ref[0]
    t_pos = j * _TQ + lax.broadcasted_iota(jnp.int32, (_TK, _TQ), 1)
    s_loc = lax.broadcasted_iota(jnp.int32, (_TK, _TQ), 0)

    def score_chunk(c, _):
        sl = pl.ds(pl.multiple_of(c * _TK, _TK), _TK)
        dots = lax.dot_general(ki_ref[0, sl, :], qidx, (((1,), (1,)), ((), ())), preferred_element_type=_F32)
        acc = jnp.zeros((_TK, _TQ), _F32)
        for h in range(ih):
            acc = acc + wt[h:h + 1, :] * jnp.maximum(dots[:, h * _TQ:(h + 1) * _TQ], 0.0)
        sc = jnp.where(c * _TK + s_loc <= t_pos, acc * IDX_SCALE, -jnp.inf)
        key_sc[sl, :] = _sortable_key(sc)
        return 0

    _loop_pairs(nchunks // 2, score_chunk, 0)
    thr, cnt_thr = _kth_largest_key(key_sc, nchunks, n_sel)

    @pl.when(jnp.max(cnt_thr) > n_sel)
    def _():
        _drop_late_ties(key_sc, nchunks, thr, n_sel)

    q_all = q_ref[0].reshape(heads * _TQ, d)
    t_pos_w = j * _TQ + lax.broadcasted_iota(jnp.int32, (_TK, _TQ), 1)

    def att_chunk(c, carry):
        m, l, acc = carry
        sl = pl.ds(pl.multiple_of(c * _TK, _TK), _TK)
        sel = (key_sc[sl, :] >= thr) & (c * _TK + s_loc <= t_pos_w)
        bias1 = jnp.where(sel, 0.0, _NEG_BIG)
        bias = jnp.concatenate([bias1] * heads, axis=1)
        logit = lax.dot_general(k_ref[0, sl, :], q_all, (((1,), (1,)), ((), ())),
                                preferred_element_type=_F32) * (d ** -0.5) + bias
        m_new = jnp.maximum(m, jnp.max(logit, axis=0, keepdims=True))
        p = jnp.exp(logit - m_new)
        alpha = jnp.exp(m - m_new)
        l = alpha * l + jnp.sum(p, axis=0, keepdims=True)
        acc = alpha * acc + jnp.dot(vt_ref[0, :, sl], p.astype(_BF16), preferred_element_type=_F32)
        return m_new, l, acc

    m0 = jnp.full((1, heads * _TQ), _NEG_BIG, _F32)
    l0 = jnp.zeros((1, heads * _TQ), _F32)
    a0 = jnp.zeros((d, heads * _TQ), _F32)
    _, l, acc = _loop_pairs(nchunks // 2, att_chunk, (m0, l0, a0))
    out_t = acc / l
    stacked = jnp.concatenate([out_t[:, h * _TQ:(h + 1) * _TQ] for h in range(heads)], axis=0)
    o_ref[0] = stacked.T


def _dsa_prompt(q, k, v, q_idx, k_idx, w_idx):
    b, L = q.shape[:2]
    heads, d = q.shape[3], q.shape[4]
    ih, e = q_idx.shape[2], q_idx.shape[3]
    n_sel = min(TOPK_MAX, L // 4)
    qh = jnp.transpose(q.reshape(b, L, heads, d), (0, 2, 1, 3)).astype(_BF16)
    qih = jnp.transpose(q_idx, (0, 2, 1, 3)).astype(_BF16)
    wt = jnp.transpose(w_idx, (0, 2, 1)).astype(_F32)
    kk = k.reshape(b, L, d).astype(_BF16)
    vt = jnp.transpose(v.reshape(b, L, d), (0, 2, 1)).astype(_BF16)
    return pl.pallas_call(
        functools.partial(_dsa_prompt_kernel, n_sel=n_sel),
        grid=(b, L // _TQ),
        in_specs=[
            pl.BlockSpec((1, heads, _TQ, d), lambda i, j: (i, 0, j, 0)),
            pl.BlockSpec((1, ih, _TQ, e), lambda i, j: (i, 0, j, 0)),
            pl.BlockSpec((1, ih, _TQ), lambda i, j: (i, 0, j)),
            pl.BlockSpec((1, L, e), lambda i, j: (i, 0, 0)),
            pl.BlockSpec((1, L, d), lambda i, j: (i, 0, 0)),
            pl.BlockSpec((1, d, L), lambda i, j: (i, 0, 0)),
        ],
        out_specs=pl.BlockSpec((1, _TQ, heads * d), lambda i, j: (i, j, 0)),
        out_shape=jax.ShapeDtypeStruct((b, L, heads * d), _F32),
        scratch_shapes=[pltpu.VMEM((L, _TQ), jnp.int32)],
        compiler_params=pltpu.CompilerParams(dimension_semantics=("parallel", "arbitrary")),
        name="dsa_prompt",
    )(qh, qih, wt, k_idx.astype(_BF16), kk, vt)


def _ssd_scan(x, dt, a, bm, cm, s0):
    b, l, h, p = x.shape
    g, n = bm.shape[2], bm.shape[3]
    r = h // g
    q = SSD_CHUNK if l % SSD_CHUNK == 0 else l
    c = l // q
    f32 = jnp.float32
    xdt = (x.astype(f32) * dt[..., None]).reshape(b, c, q, g, r, p)
    acs = jnp.cumsum((dt * a).reshape(b, c, q, g, r), axis=2)
    bc = bm.astype(f32).reshape(b, c, q, g, n)
    cc = cm.astype(f32).reshape(b, c, q, g, n)
    acs_t = jnp.moveaxis(acs, 2, -1)
    tril = jnp.tril(jnp.ones((q, q), dtype=bool))
    seg = jnp.exp(jnp.where(tril, acs_t[..., :, None] - acs_t[..., None, :], -jnp.inf))
    cb = jnp.einsum('bcign,bcjgn->bcgij', cc, bc)
    y_diag = jnp.einsum('bcgrij,bcjgrp->bcigrp', cb[:, :, :, None] * seg, xdt)
    decay_end = jnp.exp(acs[:, :, -1:] - acs)
    states = jnp.einsum('bcjgn,bcjgrp->bcgrpn', bc, xdt * decay_end[..., None])
    chunk_decay = jnp.exp(acs[:, :, -1])

    def step(s, inp):
        st, dc = inp
        return dc[..., None, None] * s + st, s

    s_fin, s_in = lax.scan(step, s0.astype(f32).reshape(b, g, r, p, n),
                           (jnp.moveaxis(states, 1, 0), jnp.moveaxis(chunk_decay, 1, 0)))
    s_in = jnp.moveaxis(s_in, 0, 1)
    y_off = jnp.einsum('bcign,bcgrpn->bcigrp', cc, s_in) * jnp.exp(acs)[..., None]
    return (y_diag + y_off).reshape(b, l, h, p), s_fin.reshape(b, h, p, n)


def _ssd_mixer(z, xbc, dt_raw, conv_prev, ssm_prev, p):
    b, T = xbc.shape[:2]
    f32 = jnp.float32
    xin = jnp.concatenate([conv_prev.astype(xbc.dtype), xbc], axis=1)
    cw = p['b_conv_w']
    conv = p['b_conv_b'] + xin[:, 0:T] * cw[0]
    for w in range(1, CONV_W):
        conv = conv + xin[:, w:w + T] * cw[w]
    xbc_c = jax.nn.silu(conv)
    xs, bm, cm = jnp.split(xbc_c, [SSM_INNER, SSM_INNER + SSM_GROUPS * D_STATE], axis=-1)
    xs = xs.reshape(b, T, SSM_HEADS, SSM_P)
    bm = bm.reshape(b, T, SSM_GROUPS, D_STATE)
    cm = cm.reshape(b, T, SSM_GROUPS, D_STATE)
    dt = jax.nn.softplus(dt_raw.astype(f32) + p['b_dt_bias'].astype(f32))
    a = -jnp.exp(p['b_a_log'].astype(f32))
    y, s_fin = _ssd_scan(xs, dt, a, bm, cm, ssm_prev)
    y = y + p['b_d'].astype(f32)[:, None] * xs.astype(f32)
    y = y.reshape(b, T, SSM_INNER) * jax.nn.silu(z.astype(f32))
    out = _rmsnorm(y, p['b_norm_g']).astype(z.dtype)
    return out, xin[:, -(CONV_W - 1):], s_fin


def _split_dot(a, b_mat, a_is_exact):
    x = b_mat if a_is_exact else a
    hi = x.astype(_BF16)
    lo = (x - hi.astype(_F32)).astype(_BF16)
    if a_is_exact:
        return jnp.dot(a, hi, preferred_element_type=_F32) + jnp.dot(a, lo, preferred_element_type=_F32)
    return jnp.dot(hi, b_mat, preferred_element_type=_F32) + jnp.dot(lo, b_mat, preferred_element_type=_F32)


def _ssd_prompt_kernel(xbc_ref, z_ref, dtc_ref, dtr_ref, cw_ref, cb_ref, hc_ref, hr_ref, ng_ref,
                       y_ref, conv_ref, st_ref, buf_sc, st_sc):
    c = pl.program_id(1)
    q = xbc_ref.shape[1]
    heads = SSM_HEADS
    inner = z_ref.shape[2]
    hp = inner // heads
    n = (xbc_ref.shape[2] - inner) // (2 * SSM_GROUPS)
    rep = heads // SSM_GROUPS
    lead = _SUB - (CONV_W - 1)

    @pl.when(c == 0)
    def _():
        buf_sc[0:_SUB, :] = jnp.zeros((_SUB, buf_sc.shape[1]), _F32)
        st_sc[...] = jnp.zeros_like(st_sc)

    xbc = xbc_ref[0]
    buf_sc[_SUB:_SUB + q, :] = xbc
    conv = cb_ref[...] + buf_sc[lead:lead + q, :] * cw_ref[0:1, :]
    for w in range(1, CONV_W):
        conv = conv + buf_sc[lead + w:lead + w + q, :] * cw_ref[w:w + 1, :]
    act = conv * jax.nn.sigmoid(conv)
    tail = buf_sc[q + lead:q + _SUB, :]
    buf_sc[lead:_SUB, :] = tail

    dt_c = _softplus(dtc_ref[0] + hc_ref[0:1, :])
    dt_r = _softplus(dtr_ref[0] + hr_ref[:, 0:1])
    da_c = dt_c * -jnp.exp(hc_ref[1:2, :])
    da_r = dt_r * -jnp.exp(hr_ref[:, 1:2])
    ri = lax.broadcasted_iota(jnp.int32, (q, q), 0)
    ci = lax.broadcasted_iota(jnp.int32, (q, q), 1)
    causal = ci <= ri
    acs_c = _split_dot(jnp.where(causal, 1.0, 0.0).astype(_BF16), da_c, True)
    acs_r = _split_dot(da_r, jnp.where(ri <= ci, 1.0, 0.0).astype(_BF16), False)
    acs_end = acs_c[q - 1:q, :]
    decay_end = jnp.exp(acs_end - acs_c)
    grow = jnp.exp(acs_c)

    ys = []
    for hh in range(heads):
        g = hh // rep
        xs = act[:, hh * hp:(hh + 1) * hp]
        bm = act[:, inner + g * n:inner + (g + 1) * n]
        cm = act[:, inner + SSM_GROUPS * n + g * n:inner + SSM_GROUPS * n + (g + 1) * n]
        xdt = xs * dt_c[:, hh:hh + 1]
        cbm = lax.dot_general(cm.astype(_BF16), bm.astype(_BF16), _NT, preferred_element_type=_F32)
        seg = jnp.exp(jnp.where(causal, acs_c[:, hh:hh + 1] - acs_r[hh:hh + 1, :], -jnp.inf))
        y = jnp.dot((cbm * seg).astype(_BF16), xdt.astype(_BF16), preferred_element_type=_F32)
        s_in = st_sc[hh]
        y = y + jnp.dot(cm.astype(_BF16), s_in.astype(_BF16), preferred_element_type=_F32) * grow[:, hh:hh + 1]
        new = jnp.dot(bm.T.astype(_BF16), (xdt * decay_end[:, hh:hh + 1]).astype(_BF16), preferred_element_type=_F32)
        st_sc[hh] = jnp.exp(acs_end[:, hh:hh + 1]) * s_in + new
        ys.append(y + hr_ref[hh:hh + 1, 2:3] * xs)
    y = jnp.concatenate(ys, axis=1)
    zz = z_ref[0]
    y_ref[0] = _rms(y * (zz * jax.nn.sigmoid(zz)), ng_ref[...])

    @pl.when(c == pl.num_programs(1) - 1)
    def _():
        conv_ref[0] = tail
        st_ref[0] = st_sc[...]


def _ssd_prompt(z, xbc, dt_raw, p):
    b, L, inner = z.shape
    heads = dt_raw.shape[-1]
    ch = xbc.shape[-1]
    q = SSD_CHUNK
    assert L % q == 0
    assert heads == SSM_HEADS and heads <= _SUB
    hc = jnp.stack([p['b_dt_bias'], p['b_a_log'], p['b_d']]).astype(_F32)
    hc_l = jnp.pad(hc, ((0, 0), (0, _LANES - heads)))
    hr_s = jnp.pad(hc.T, ((0, _SUB - heads), (0, _LANES - 3)))
    dt_l = jnp.pad(dt_raw, ((0, 0), (0, 0), (0, _LANES - heads)))
    dt_s = jnp.pad(jnp.swapaxes(dt_raw, 1, 2), ((0, 0), (0, _SUB - heads), (0, 0)))
    y, conv_new, st = pl.pallas_call(
        _ssd_prompt_kernel,
        grid=(b, L // q),
        in_specs=[pl.BlockSpec((1, q, ch), lambda i, c: (i, c, 0)), pl.BlockSpec((1, q, inner), lambda i, c: (i, c, 0)),
                  pl.BlockSpec((1, q, _LANES), lambda i, c: (i, c, 0)), pl.BlockSpec((1, _SUB, q), lambda i, c: (i, 0, c)),
                  pl.BlockSpec((CONV_W, ch), lambda i, c: (0, 0)), pl.BlockSpec((1, ch), lambda i, c: (0, 0)),
                  pl.BlockSpec((3, _LANES), lambda i, c: (0, 0)), pl.BlockSpec((_SUB, _LANES), lambda i, c: (0, 0)),
                  pl.BlockSpec((1, inner), lambda i, c: (0, 0))],
        out_specs=[pl.BlockSpec((1, q, inner), lambda i, c: (i, c, 0)),
                   pl.BlockSpec((1, CONV_W - 1, ch), lambda i, c: (i, 0, 0)),
                   pl.BlockSpec((1, heads, D_STATE, inner // heads), lambda i, c: (i, 0, 0, 0))],
        out_shape=[jax.ShapeDtypeStruct((b, L, inner), _F32), jax.ShapeDtypeStruct((b, CONV_W - 1, ch), _F32),
                   jax.ShapeDtypeStruct((b, heads, D_STATE, inner // heads), _F32)],
        scratch_shapes=[pltpu.VMEM((q + _SUB, ch), _F32), pltpu.VMEM((heads, D_STATE, inner // heads), _F32)],
        compiler_params=pltpu.CompilerParams(dimension_semantics=("parallel", "arbitrary")),
        name="ssd_prompt",
    )(xbc, z, dt_l, dt_s, p['b_conv_w'].astype(_F32), p['b_conv_b'].reshape(1, ch).astype(_F32),
      hc_l, hr_s, p['b_norm_g'].reshape(1, inner).astype(_F32))
    return y, conv_new, jnp.swapaxes(st, 2, 3)


def _stick_breaking(q, k, v, q_pos, k_pos):
    z = jnp.einsum('btgrd,bsgd->bgrts', q, k).astype(jnp.float32) * HEAD_DIM ** -0.5
    m = k_pos[None, :] < q_pos[:, None]
    log_1m = jnp.where(m, jax.nn.log_sigmoid(-z), 0.0)
    cum = jnp.cumsum(log_1m, axis=-1)
    log_w = jax.nn.log_sigmoid(z) + cum[..., -1:] - cum
    w = jnp.where(m, jnp.exp(log_w), 0.0).astype(v.dtype)
    return jnp.einsum('bgrts,bsgd->btgrd', w, v)


def _heads_to_rows(out_t, heads):
    stacked = jnp.concatenate([out_t[:, h * _TQ:(h + 1) * _TQ] for h in range(heads)], axis=0)
    return stacked.T


def _softplus(z):
    return jnp.maximum(z, 0.0) + jnp.log(1.0 + jnp.exp(-jnp.abs(z)))


def _sb_prompt_kernel(q_ref, k_ref, vt_ref, o_ref, *, rep):
    j = pl.program_id(1)
    heads, d = q_ref.shape[1], q_ref.shape[3]
    groups = k_ref.shape[1]
    gw = rep * _TQ
    width = heads * _TQ
    qg = [q_ref[0, g * rep:(g + 1) * rep].reshape(gw, d) for g in range(groups)]
    row = lax.broadcasted_iota(jnp.int32, (_TK, _TK), 0)
    col = lax.broadcasted_iota(jnp.int32, (_TK, _TK), 1)
    after = jnp.where(col > row, 1.0, 0.0).astype(_BF16)
    s_loc = lax.broadcasted_iota(jnp.int32, (_TK, width), 0)
    t_loc = lax.broadcasted_iota(jnp.int32, (_TK, width), 1) & (_TQ - 1)
    diag = _TQ // _TK

    def chunk(c, carry, diag_off=None):
        masked = diag_off is not None
        if masked:
            strictly_before = s_loc + diag_off < t_loc
        tail, accs = carry
        sl = pl.ds(pl.multiple_of(c * _TK, _TK), _TK)
        z = jnp.concatenate(
            [lax.dot_general(k_ref[0, g, sl, :], qg[g], (((1,), (1,)), ((), ())), preferred_element_type=_F32)
             for g in range(groups)], axis=1) * (d ** -0.5)
        sp = _softplus(z)
        if masked:
            sp = jnp.where(strictly_before, sp, 0.0)
        hi = sp.astype(_BF16)
        lo = (sp - hi.astype(_F32)).astype(_BF16)
        later = (jnp.dot(after, hi, preferred_element_type=_F32)
                 + jnp.dot(after, lo, preferred_element_type=_F32))
        w = jnp.exp(z - sp - later - tail)
        if masked:
            w = jnp.where(strictly_before, w, 0.0)
        wb = w.astype(_BF16)
        accs = tuple(accs[g] + jnp.dot(vt_ref[0, g, :, sl], wb[:, g * gw:(g + 1) * gw],
                                       preferred_element_type=_F32) for g in range(groups))
        return tail + jnp.sum(sp, axis=0, keepdims=True), accs

    init = (jnp.zeros((1, width), _F32), tuple(jnp.zeros((d, gw), _F32) for _ in range(groups)))
    carry = init
    for i in reversed(range(diag)):
        carry = chunk(j * diag + i, carry, diag_off=i * _TK)
    assert diag % 2 == 0
    _, accs = _loop_pairs(j * (diag // 2), lambda i, cr: chunk(j * diag - 1 - i, cr), carry)
    o_ref[0] = _heads_to_rows(jnp.concatenate(accs, axis=1), heads)


def _sb_prompt(q, k, v):
    b, L, groups, rep, d = q.shape
    heads = groups * rep
    qh = jnp.transpose(q.reshape(b, L, heads, d), (0, 2, 1, 3)).astype(_BF16)
    kg = jnp.transpose(k, (0, 2, 1, 3)).astype(_BF16)
    vt = jnp.transpose(v, (0, 2, 3, 1)).astype(_BF16)
    return pl.pallas_call(
        functools.partial(_sb_prompt_kernel, rep=rep),
        grid=(b, L // _TQ),
        in_specs=[
            pl.BlockSpec((1, heads, _TQ, d), lambda i, j: (i, 0, j, 0)),
            pl.BlockSpec((1, groups, L, d), lambda i, j: (i, 0, 0, 0)),
            pl.BlockSpec((1, groups, d, L), lambda i, j: (i, 0, 0, 0)),
        ],
        out_specs=pl.BlockSpec((1, _TQ, heads * d), lambda i, j: (i, j, 0)),
        out_shape=jax.ShapeDtypeStruct((b, L, heads * d), _F32),
        compiler_params=pltpu.CompilerParams(dimension_semantics=("parallel", "arbitrary")),
        name="sb_prompt",
    )(qh, kg, vt)


def _causal_mha_kernel(q_ref, k_ref, vt_ref, o_ref, *, scale):
    j = pl.program_id(1)
    heads = q_ref.shape[1]
    dv = vt_ref.shape[2]
    width = heads * _TQ
    s_loc = lax.broadcasted_iota(jnp.int32, (_TK, width), 0)
    t_loc = lax.broadcasted_iota(jnp.int32, (_TK, width), 1) & (_TQ - 1)
    diag = _TQ // _TK

    def chunk(c, carry, diag_off=None):
        masked = diag_off is not None
        if masked:
            visible = s_loc + diag_off <= t_loc
        m, l, accs = carry
        sl = pl.ds(pl.multiple_of(c * _TK, _TK), _TK)
        logit = jnp.concatenate(
            [lax.dot_general(k_ref[0, h, sl, :], q_ref[0, h], (((1,), (1,)), ((), ())), preferred_element_type=_F32)
             for h in range(heads)], axis=1) * scale
        if masked:
            logit = jnp.where(visible, logit, _NEG_BIG)
        m_new = jnp.maximum(m, jnp.max(logit, axis=0, keepdims=True))
        p = jnp.exp(logit - m_new)
        alpha = jnp.exp(m - m_new)
        l = alpha * l + jnp.sum(p, axis=0, keepdims=True)
        pb = p.astype(_BF16)
        accs = tuple(alpha[:, h * _TQ:(h + 1) * _TQ] * accs[h]
                     + jnp.dot(vt_ref[0, h, :, sl], pb[:, h * _TQ:(h + 1) * _TQ], preferred_element_type=_F32)
                     for h in range(heads))
        return m_new, l, accs

    init = (jnp.full((1, width), _NEG_BIG, _F32), jnp.zeros((1, width), _F32),
            tuple(jnp.zeros((dv, _TQ), _F32) for _ in range(heads)))
    carry = init
    for i in range(diag):
        carry = chunk(j * diag + i, carry, diag_off=i * _TK)
    assert diag % 2 == 0
    _, l, accs = _loop_pairs(j * (diag // 2), lambda i, cr: chunk(i, cr), carry)
    o_ref[0] = _heads_to_rows(jnp.concatenate(accs, axis=1) / l, heads)


def _mla_prompt(q_nope, q_rope, k_nope, k_rope, v):
    b, L, heads, _ = q_nope.shape
    dv = v.shape[-1]
    q = jnp.transpose(jnp.concatenate([q_nope, q_rope], axis=-1), (0, 2, 1, 3)).astype(_BF16)
    kr = jnp.broadcast_to(k_rope[:, :, None, :], (b, L, heads, k_rope.shape[-1]))
    k = jnp.transpose(jnp.concatenate([k_nope, kr], axis=-1), (0, 2, 1, 3)).astype(_BF16)
    vt = jnp.transpose(v, (0, 2, 3, 1)).astype(_BF16)
    dq = q.shape[-1]
    return pl.pallas_call(
        functools.partial(_causal_mha_kernel, scale=MLA_SCALE),
        grid=(b, L // _TQ),
        in_specs=[
            pl.BlockSpec((1, heads, _TQ, dq), lambda i, j: (i, 0, j, 0)),
            pl.BlockSpec((1, heads, L, dq), lambda i, j: (i, 0, 0, 0)),
            pl.BlockSpec((1, heads, dv, L), lambda i, j: (i, 0, 0, 0)),
        ],
        out_specs=pl.BlockSpec((1, _TQ, heads * dv), lambda i, j: (i, j, 0)),
        out_shape=jax.ShapeDtypeStruct((b, L, heads * dv), _F32),
        compiler_params=pltpu.CompilerParams(dimension_semantics=("parallel", "arbitrary")),
        name="mla_prompt",
    )(q, k, vt)


def _mla_latent(q_nope, q_rope, c_all, r_all, w_uk, w_uv, q_pos, k_pos):
    b, T = q_nope.shape[:2]
    q_lat = jnp.einsum('bthd,chd->bthc', q_nope, w_uk)
    sc = (jnp.einsum('bthc,bsc->bhts', q_lat, c_all)
          + jnp.einsum('bthr,bsr->bhts', q_rope, r_all)).astype(jnp.float32) * MLA_SCALE
    sc = jnp.where(k_pos[None, :] <= q_pos[:, None], sc, -jnp.inf)
    pr = jax.nn.softmax(sc, axis=-1).astype(c_all.dtype)
    o_lat = jnp.einsum('bhts,bsc->bthc', pr, c_all)
    return jnp.einsum('bthc,chd->bthd', o_lat, w_uv).reshape(b, T, D_HEADS * D_V)


def _mixer_prompt(h, g_mix, pos, p):
    b, L = h.shape[:2]
    u = _project(h, g_mix, pos, p)
    a_out = _dsa_prompt(u['a_q'], u['a_k'], u['a_v'], u['a_qi'], u['a_ki'], u['a_w'])
    b_out, conv_new, ssm_new = _ssd_prompt(u['b_z'], u['b_xbc'], u['b_dt'], p)
    c_out = _sb_prompt(u['c_q'], u['c_k'], u['c_v'])
    w_ukv = jnp.concatenate([p['d_w_uk'].reshape(KV_RANK, D_HEADS * D_NOPE),
                             p['d_w_uv'].reshape(KV_RANK, D_HEADS * D_V)], axis=1)
    kv_up = _norm_matmul(u['d_ckv_raw'].reshape(b * L, KV_RANK), p['d_kv_norm_g'], w_ukv)
    k_nope = kv_up[:, :D_HEADS * D_NOPE].reshape(b, L, D_HEADS, D_NOPE)
    v_d = kv_up[:, D_HEADS * D_NOPE:].reshape(b, L, D_HEADS, D_V)
    d_out = _mla_prompt(u['d_qn'], u['d_qr'], k_nope, u['d_kr'], v_d)
    mix = jnp.concatenate([a_out, b_out, c_out, d_out], axis=-1)
    new = (u['a_k'], u['a_v'], u['a_ki'], u['c_k'], u['c_v'], u['d_ckv'], u['d_kr'],
           conv_new, ssm_new.astype(h.dtype))
    return mix, new


_PPS = 16
_SUB = 8


def _page_specs(rows, width, layer, n_pages, descending):
    specs = []
    for slot in range(_PPS):
        if descending:
            idx = lambda b, c, pt, slot=slot: (layer, pt[b, n_pages - (c + 1) * _PPS + slot], 0, 0)
        else:
            idx = lambda b, c, pt, slot=slot: (layer, pt[b, c * _PPS + slot], 0, 0)
        specs.append(pl.BlockSpec((None, None, rows, width), idx))
    return specs


def _seq_spec(rows, width):
    return pl.BlockSpec((1, rows, width), lambda b, c, pt: (b, 0, 0))


def _pad_rows(x):
    return jnp.pad(x, ((0, 0), (0, _SUB - x.shape[1]), (0, 0)))


_NT = (((1,), (1,)), ((), ()))


def _paged_call(kernel_fn, page_table, seq_inputs, pools, layer, descending, out_shape, out_spec, scratch, name):
    b, n_pages = page_table.shape
    assert n_pages % _PPS == 0
    in_specs = [_seq_spec(x.shape[1], x.shape[2]) for x in seq_inputs]
    args = list(seq_inputs)
    for pool in pools:
        in_specs += _page_specs(pool.shape[2], pool.shape[3], layer, n_pages, descending)
        args += [pool] * _PPS
    return pl.pallas_call(
        kernel_fn,
        grid_spec=pltpu.PrefetchScalarGridSpec(
            num_scalar_prefetch=1, grid=(b, n_pages // _PPS),
            in_specs=in_specs, out_specs=out_spec, scratch_shapes=scratch),
        out_shape=out_shape,
        compiler_params=pltpu.CompilerParams(dimension_semantics=("parallel", "arbitrary")),
        name=name,
    )(page_table, *args)


def _pool_t(pool):
    lead = pool.shape[:2]
    flat = pool.reshape(lead + (PAGE_SIZE, -1))
    return jnp.swapaxes(flat, 2, 3)


def _suffix_sum(x):
    width = x.shape[1]
    lane = lax.broadcasted_iota(jnp.int32, x.shape, 1)
    k = 1
    while k < width:
        x = x + jnp.where(lane < width - k, pltpu.roll(x, width - k, axis=1), 0.0)
        k *= 2
    return x


def _head_rows(rows):
    pad = jnp.zeros((_SUB - len(rows), rows[0].shape[1]), _F32)
    return jnp.concatenate(rows + [pad], axis=0)


def _sb_decode_kernel(pt_ref, q_ref, *refs, scale, heads, rep):
    k_refs, v_refs = refs[:_PPS], refs[_PPS:2 * _PPS]
    o_ref, tail_sc, acc_sc = refs[2 * _PPS:]
    c = pl.program_id(1)
    d = q_ref.shape[1] // heads

    @pl.when(c == 0)
    def _():
        tail_sc[...] = jnp.zeros_like(tail_sc)
        acc_sc[...] = jnp.zeros_like(acc_sc)

    zs = []
    for slot in range(_PPS):
        kt = k_refs[slot][...]
        rows = []
        for hh in range(heads):
            g = hh // rep
            rows.append(jnp.sum(kt[g * d:(g + 1) * d, :] * q_ref[0, hh * d:(hh + 1) * d, :], axis=0, keepdims=True))
        zs.append(_head_rows(rows))
    z = jnp.concatenate(zs, axis=1) * scale
    incl = _suffix_sum(_softplus(z))
    w = jnp.exp(z - incl - tail_sc[...][:, 0:1])
    for slot in range(_PPS):
        vt = v_refs[slot][...]
        for hh in range(heads):
            g = hh // rep
            acc_sc[hh * d:(hh + 1) * d, :] += (vt[g * d:(g + 1) * d, :]
                                               * w[hh:hh + 1, slot * PAGE_SIZE:(slot + 1) * PAGE_SIZE])
    tail_sc[...] = tail_sc[...] + incl[:, 0:1]

    @pl.when(c == pl.num_programs(1) - 1)
    def _():
        o_ref[0] = jnp.broadcast_to(jnp.sum(acc_sc[...], axis=1, keepdims=True), acc_sc.shape)


def _sb_decode(q, pool_k, pool_v, layer, page_table):
    b, t, groups, rep, d = q.shape
    assert t == 1
    heads = groups * rep
    qb = jnp.broadcast_to(q.reshape(b, heads * d, 1), (b, heads * d, PAGE_SIZE)).astype(_F32)
    out = _paged_call(
        functools.partial(_sb_decode_kernel, scale=d ** -0.5, heads=heads, rep=rep), page_table, [qb],
        [_pool_t(pool_k), _pool_t(pool_v)], layer, True, jax.ShapeDtypeStruct((b, heads * d, PAGE_SIZE), _F32),
        pl.BlockSpec((1, heads * d, PAGE_SIZE), lambda i, c, pt: (i, 0, 0)),
        [pltpu.VMEM((_SUB, PAGE_SIZE), _F32), pltpu.VMEM((heads * d, PAGE_SIZE), _F32)], "sb_decode")
    return out[:, :, 0]


def _softmax_pages(logits, values, m, l, acc):
    mx = logits[0].max(axis=1, keepdims=True)
    for lg in logits[1:]:
        mx = jnp.maximum(mx, lg.max(axis=1, keepdims=True))
    m_new = jnp.maximum(m, mx)
    alpha = jnp.exp(m - m_new)
    l = alpha * l
    acc = alpha[:, :acc.shape[1]] * acc
    for lg, val in zip(logits, values):
        pr = jnp.exp(lg - m_new)
        l = l + jnp.sum(pr, axis=1, keepdims=True)
        acc = acc + jnp.dot(pr.astype(_BF16), val, preferred_element_type=_F32)
    return m_new, l, acc


def _mla_decode_kernel(pt_ref, ql_ref, qr_ref, lnew_ref, cnew_ref, *refs, scale):
    c_refs, r_refs = refs[:_PPS], refs[_PPS:2 * _PPS]
    o_ref, m_sc, l_sc, acc_sc = refs[2 * _PPS:]
    c = pl.program_id(1)

    @pl.when(c == 0)
    def _():
        m_sc[...] = lnew_ref[0]
        l_sc[...] = jnp.ones_like(l_sc)
        acc_sc[...] = jnp.broadcast_to(cnew_ref[0], acc_sc.shape)

    ql, qr = ql_ref[0], qr_ref[0]
    lat = [c_refs[s][...].astype(_BF16) for s in range(_PPS)]
    logits = [(lax.dot_general(ql, lat[s], _NT, preferred_element_type=_F32)
               + jnp.dot(qr, r_refs[s][...].astype(_BF16), preferred_element_type=_F32)) * scale
              for s in range(_PPS)]
    m, l, acc = _softmax_pages(logits, lat, m_sc[...], l_sc[...], acc_sc[...])
    m_sc[...], l_sc[...], acc_sc[...] = m, l, acc

    @pl.when(c == pl.num_programs(1) - 1)
    def _():
        o_ref[0] = acc / l


def _mla_decode(q_nope, q_rope, c_new, r_new, pool_c, pool_r, w_uk, w_uv, layer, page_table):
    b, t, heads, _ = q_nope.shape
    assert t == 1
    q_lat = jnp.einsum('bthd,chd->bhc', q_nope, w_uk)
    qr = q_rope[:, 0]
    l_new = (jnp.einsum('bhc,bc->bh', q_lat, c_new[:, 0]) + jnp.einsum('bhr,br->bh', qr, r_new[:, 0])) * MLA_SCALE
    l_new = jnp.broadcast_to(_pad_rows(l_new[:, :, None]), (b, _SUB, PAGE_SIZE)).astype(_F32)
    rank = pool_c.shape[-1]
    o_lat = _paged_call(
        functools.partial(_mla_decode_kernel, scale=MLA_SCALE), page_table,
        [_pad_rows(q_lat).astype(_BF16), _pad_rows(qr).astype(_BF16), l_new, c_new.astype(_F32)],
        [pool_c, _pool_t(pool_r)], layer, False, jax.ShapeDtypeStruct((b, _SUB, rank), _F32),
        pl.BlockSpec((1, _SUB, rank), lambda i, c, pt: (i, 0, 0)),
        [pltpu.VMEM((_SUB, PAGE_SIZE), _F32), pltpu.VMEM((_SUB, PAGE_SIZE), _F32), pltpu.VMEM((_SUB, rank), _F32)],
        "mla_decode")
    return jnp.einsum('bhc,chd->bhd', o_lat[:, :heads], w_uv).reshape(b, t, heads * w_uv.shape[-1])


def _idx_decode_kernel(pt_ref, qi_ref, w_ref, *refs):
    ki_refs, o_ref = refs[:_PPS], refs[_PPS]
    qi, w = qi_ref[0], w_ref[0]
    for s in range(_PPS):
        dots = jnp.dot(qi, ki_refs[s][...].astype(_BF16), preferred_element_type=_F32)
        o_ref[0, s:s + 1, :] = jnp.sum(w * jnp.maximum(dots, 0.0), axis=0, keepdims=True) * IDX_SCALE


def _dsa_decode_kernel(pt_ref, sc_ref, scnew_ref, q_ref, lnew_ref, vnew_ref, *refs, n_sel, heads):
    k_refs, v_refs = refs[:_PPS], refs[_PPS:2 * _PPS]
    o_ref, bias_sc, m_sc, l_sc, acc_sc = refs[2 * _PPS:]
    c = pl.program_id(1)
    n_rows = sc_ref.shape[1]

    @pl.when(c == 0)
    def _():
        key = _sortable_key(sc_ref[0])
        key_new = _sortable_key(scnew_ref[0])

        def count(pred_tile, pred_new):
            return jnp.sum(jnp.where(pred_tile, 1, 0)) + jnp.max(jnp.where(pred_new, 1, 0))

        def bit_step(i, carry):
            t, cnt_t = carry
            cand = t ^ lax.shift_left(jnp.int32(1), 31 - i)
            cnt = count(key >= cand, key_new >= cand)
            ok = cnt >= n_sel
            return jnp.where(ok, cand, t), jnp.where(ok, cnt, cnt_t)

        thr, cnt_thr = lax.fori_loop(0, 32, bit_step, (jnp.int32(_INT_MIN), jnp.int32(n_rows * PAGE_SIZE + 1)))
        bias_sc[...] = jnp.where(key >= thr, 0.0, _NEG_BIG)
        sel_new = key_new >= thr

        def init(sel_new):
            m_sc[...] = jnp.where(sel_new, lnew_ref[0], _NEG_BIG)
            l_sc[...] = jnp.where(sel_new, 1.0, 0.0) + jnp.zeros_like(l_sc)
            acc_sc[...] = jnp.where(sel_new, vnew_ref[0], 0.0)

        init(sel_new)

        @pl.when(cnt_thr > n_sel)
        def _():
            room = (n_sel - count(key > thr, key_new > thr)).astype(_F32)
            eq = key == thr
            eqf = jnp.where(eq, 1.0, 0.0)
            r_i = lax.broadcasted_iota(jnp.int32, (PAGE_SIZE, PAGE_SIZE), 0)
            c_i = lax.broadcasted_iota(jnp.int32, (PAGE_SIZE, PAGE_SIZE), 1)
            in_row = jnp.dot(eqf.astype(_BF16), jnp.where(r_i < c_i, 1.0, 0.0).astype(_BF16),
                             preferred_element_type=_F32)
            row_tot = jnp.broadcast_to(jnp.sum(eqf, axis=1, keepdims=True), eqf.shape).astype(_BF16)
            rr = lax.broadcasted_iota(jnp.int32, (n_rows, n_rows), 0)
            rc = lax.broadcasted_iota(jnp.int32, (n_rows, n_rows), 1)
            rows_before = jnp.dot(jnp.where(rc < rr, 1.0, 0.0).astype(_BF16), row_tot, preferred_element_type=_F32)
            keep = (key > thr) | (eq & (in_row + rows_before < room))
            bias_sc[...] = jnp.where(keep, 0.0, _NEG_BIG)
            init((key_new > thr) | ((key_new == thr) & (jnp.sum(eqf) < room)))

    d = k_refs[0].shape[0]
    lg = []
    for s in range(_PPS):
        kt = k_refs[s][...]
        rows = [jnp.sum(kt * q_ref[0, hh * d:(hh + 1) * d, :], axis=0, keepdims=True) for hh in range(heads)]
        lg.append(_head_rows(rows) * (d ** -0.5) + bias_sc[pl.ds(c * _PPS + s, 1), :])
    logit = jnp.concatenate(lg, axis=1)
    m_old = m_sc[...][:, 0:1]
    m_new = jnp.maximum(m_old, jnp.max(logit, axis=1, keepdims=True))
    alpha = jnp.exp(m_old - m_new)
    pr = jnp.exp(logit - m_new)
    l_new = alpha * l_sc[...][:, 0:1] + jnp.sum(pr, axis=1, keepdims=True)
    for hh in range(heads):
        acc_h = alpha[hh:hh + 1, :] * acc_sc[hh * d:(hh + 1) * d, :]
        for s in range(_PPS):
            acc_h = acc_h + v_refs[s][...] * pr[hh:hh + 1, s * PAGE_SIZE:(s + 1) * PAGE_SIZE]
        acc_sc[hh * d:(hh + 1) * d, :] = acc_h
    m_sc[...] = jnp.broadcast_to(m_new, m_sc.shape)
    l_sc[...] = jnp.broadcast_to(l_new, l_sc.shape)

    @pl.when(c == pl.num_programs(1) - 1)
    def _():
        for hh in range(heads):
            tot = jnp.sum(acc_sc[hh * d:(hh + 1) * d, :], axis=1, keepdims=True) / l_new[hh:hh + 1, :]
            o_ref[0, hh * d:(hh + 1) * d, :] = jnp.broadcast_to(tot, (d, PAGE_SIZE))


def _dsa_decode(q, k_new, v_new, q_idx, ki_new, w_idx, pool_k, pool_v, pool_ki, layer, page_table):
    b, t = q.shape[:2]
    assert t == 1
    heads, d = q.shape[3], q.shape[4]
    ih, e = q_idx.shape[2], q_idx.shape[3]
    n_pages = page_table.shape[1]
    n_sel = min(TOPK_MAX, (n_pages * PAGE_SIZE + t) // 4)
    assert ih == _SUB
    w8 = jnp.broadcast_to(w_idx[:, 0, :, None], (b, ih, PAGE_SIZE)).astype(_F32)
    scores = _paged_call(
        _idx_decode_kernel, page_table, [q_idx[:, 0].astype(_BF16), w8], [_pool_t(pool_ki)], layer, False,
        jax.ShapeDtypeStruct((b, n_pages, PAGE_SIZE), _F32),
        pl.BlockSpec((1, _PPS, PAGE_SIZE), lambda i, c, pt: (i, c, 0)), [], "idx_decode")
    rel_new = jax.nn.relu(jnp.einsum('bhe,be->bh', q_idx[:, 0], ki_new[:, 0]))
    sc_new = jnp.einsum('bh,bh->b', w_idx[:, 0], rel_new) * IDX_SCALE
    sc_new = jnp.broadcast_to(sc_new[:, None, None], (b, 1, PAGE_SIZE)).astype(_F32)
    qh = q.reshape(b, heads, d)
    l_new = jnp.einsum('bhd,bd->bh', qh, k_new.reshape(b, d)) * d ** -0.5
    l_new = jnp.broadcast_to(_pad_rows(l_new[:, :, None]), (b, _SUB, PAGE_SIZE)).astype(_F32)
    qb = jnp.broadcast_to(qh.reshape(b, heads * d, 1), (b, heads * d, PAGE_SIZE)).astype(_F32)
    v_rep = jnp.tile(v_new.reshape(b, d), (1, heads))
    v_lane0 = jnp.zeros((b, heads * d, PAGE_SIZE), _F32).at[:, :, 0].set(v_rep)
    out = _paged_call(
        functools.partial(_dsa_decode_kernel, n_sel=n_sel, heads=heads), page_table,
        [scores, sc_new, qb, l_new, v_lane0],
        [_pool_t(pool_k), _pool_t(pool_v)], layer, False, jax.ShapeDtypeStruct((b, heads * d, PAGE_SIZE), _F32),
        pl.BlockSpec((1, heads * d, PAGE_SIZE), lambda i, c, pt: (i, 0, 0)),
        [pltpu.VMEM((n_pages, PAGE_SIZE), _F32), pltpu.VMEM((_SUB, PAGE_SIZE), _F32),
         pltpu.VMEM((_SUB, PAGE_SIZE), _F32), pltpu.VMEM((heads * d, PAGE_SIZE), _F32)], "dsa_decode")
    return out[:, :, 0].reshape(b, t, heads * d)


def _mixer_sample(h, g_mix, pos, p, l, caches, page_table, past_len):
    (ca_k, ca_v, ca_ki, cc_k, cc_v, cd_ckv, cd_kr, sb_conv, sb_ssm) = caches
    hn = h
    b, T = h.shape[:2]
    L = past_len + T
    k_pos = jnp.arange(L)
    u = _project(h, g_mix, pos, p)
    a_out = _dsa_decode(u['a_q'], u['a_k'], u['a_v'], u['a_qi'], u['a_ki'], u['a_w'], ca_k, ca_v, ca_ki, l, page_table)
    b_out, conv_new, ssm_new = _ssd_mixer(u['b_z'], u['b_xbc'], u['b_dt'], sb_conv[l], sb_ssm[l], p)
    c_out = _sb_decode(u['c_q'], cc_k, cc_v, l, page_table).reshape(b, T, C_HEADS * HEAD_DIM)
    d_out = _mla_decode(u['d_qn'], u['d_qr'], u['d_ckv'], u['d_kr'], cd_ckv, cd_kr, p['d_w_uk'], p['d_w_uv'],
                        l, page_table)
    mix = jnp.concatenate([a_out, b_out, c_out, d_out], axis=-1)
    new = (u['a_k'], u['a_v'], u['a_ki'], u['c_k'], u['c_v'], u['d_ckv'], u['d_kr'],
           conv_new.astype(sb_conv.dtype), ssm_new.astype(sb_ssm.dtype))
    return mix, new


def _mem_kv(mem, p):
    b, mt, dm = mem.shape
    kv = _norm_matmul(mem.reshape(b * mt, dm), p['g_mem_kv'], jnp.concatenate([p['w_mk'], p['w_mv']], axis=1))
    width = MEM_HEADS * MEM_HD
    return kv[:, :width].reshape(b, mt, MEM_HEADS, MEM_HD), kv[:, width:].reshape(b, mt, MEM_HEADS, MEM_HD)


def _post_mix_prompt(h, mix, mem_k, mem_v, p):
    b, L, dm = h.shape
    h2 = _mix_mem(h.reshape(b * L, dm), mix.reshape(b * L, dm), mem_k, mem_v, p, rows_per_batch=L)
    return _ffn(h2, p['g_ffn'], p['w_gate'], p['w_up'], p['w_down']).reshape(b, L, dm)


def _post_mix_sample(h, mix, mem_k, mem_v, p):
    b, L, dm = h.shape
    h1 = _matmul_residual(mix.reshape(b * L, dm), p['w_out'], h.reshape(b * L, dm))
    q = _norm_matmul(h1, p['g_mem_q'], p['w_mq']).reshape(b, L, MEM_HEADS, MEM_HD)
    sc = jnp.einsum('blhd,bmhd->bhlm', q, mem_k.astype(q.dtype)).astype(jnp.float32) * MEM_HD ** -0.5
    pr = jax.nn.softmax(sc, axis=-1).astype(q.dtype)
    o = jnp.einsum('bhlm,bmhd->blhd', pr, mem_v.astype(q.dtype)).reshape(b * L, MEM_HEADS * MEM_HD)
    h2 = _matmul_residual(o, p['w_mo'], h1)
    return _ffn(h2, p['g_ffn'], p['w_gate'], p['w_up'], p['w_down']).reshape(b, L, dm)


def _final_norm_kernel(x_ref, g_ref, o_ref):
    x = x_ref[...]
    o_ref[...] = x * lax.rsqrt(jnp.mean(x * x, axis=-1, keepdims=True) + NORM_EPS) * g_ref[...]


def _final_norm(x, g):
    shp = x.shape
    x2 = x.reshape(-1, shp[-1])
    n = x2.shape[0]
    tm = min(n, 1024)
    out = pl.pallas_call(
        _final_norm_kernel,
        grid=(n // tm,),
        in_specs=[pl.BlockSpec((tm, shp[-1]), lambda i: (i, 0)), pl.BlockSpec((1, shp[-1]), lambda i: (0, 0))],
        out_specs=pl.BlockSpec((tm, shp[-1]), lambda i: (i, 0)),
        out_shape=jax.ShapeDtypeStruct(x2.shape, x2.dtype),
    )(x2, g.reshape(1, -1))
    return out.reshape(shp)


def kernel(x_prompt, x_sample, mem_prompt, cache_a_k, cache_a_v, cache_a_kidx, cache_c_k, cache_c_v, cache_d_ckv, cache_d_krope, state_b_conv, state_b_ssm, cache_mem_k, cache_mem_v, page_table, g_mix, w_in, b_conv_w, b_conv_b, b_dt_bias, b_a_log, b_d, b_norm_g, d_q_norm_g, d_kv_norm_g, d_w_uq, d_w_uk, d_w_uv, w_out, g_mem_q, g_mem_kv, w_mq, w_mk, w_mv, w_mo, g_ffn, w_gate, w_up, w_down, g_final):
    s_prompt = x_prompt.shape[1]
    t_new = x_sample.shape[1]
    past_len = page_table.shape[1] * PAGE_SIZE
    pos_p = jnp.arange(s_prompt)
    pos_s = past_len + jnp.arange(t_new)
    caches = (cache_a_k, cache_a_v, cache_a_kidx, cache_c_k, cache_c_v, cache_d_ckv, cache_d_krope,
              state_b_conv, state_b_ssm)
    hp, hs = x_prompt, x_sample
    new_p, new_s, mem_ks, mem_vs = [], [], [], []
    for l in range(DEPTH):
        p = {'w_in': w_in[l], 'b_conv_w': b_conv_w[l], 'b_conv_b': b_conv_b[l], 'b_dt_bias': b_dt_bias[l],
             'b_a_log': b_a_log[l], 'b_d': b_d[l], 'b_norm_g': b_norm_g[l],
             'd_q_norm_g': d_q_norm_g[l], 'd_kv_norm_g': d_kv_norm_g[l], 'd_w_uq': d_w_uq[l],
             'd_w_uk': d_w_uk[l], 'd_w_uv': d_w_uv[l], 'w_out': w_out[l],
             'g_mem_q': g_mem_q[l], 'g_mem_kv': g_mem_kv[l], 'w_mq': w_mq[l], 'w_mk': w_mk[l],
             'w_mv': w_mv[l], 'w_mo': w_mo[l], 'g_ffn': g_ffn[l], 'w_gate': w_gate[l],
             'w_up': w_up[l], 'w_down': w_down[l]}
        mix_p, st_p = _mixer_prompt(hp, g_mix[l], pos_p, p)
        mk, mv = _mem_kv(mem_prompt, p)
        hp = _post_mix_prompt(hp, mix_p, mk, mv, p)
        new_p.append(st_p)
        mem_ks.append(mk)
        mem_vs.append(mv)
        mix_s, st_s = _mixer_sample(hs, g_mix[l], pos_s, p, l, caches, page_table, past_len)
        hs = _post_mix_sample(hs, mix_s, cache_mem_k[l], cache_mem_v[l], p)
        new_s.append(st_s)
    y_prompt = _final_norm(hp, g_final)
    y_sample = _final_norm(hs, g_final)
    (p_a_k, p_a_v, p_a_kidx, p_c_k, p_c_v, p_d_ckv, p_d_krope, p_b_conv, p_b_ssm) = [jnp.stack(t) for t in zip(*new_p)]
    (s_a_k, s_a_v, s_a_kidx, s_c_k, s_c_v, s_d_ckv, s_d_krope, s_b_conv, s_b_ssm) = [jnp.stack(t) for t in zip(*new_s)]
    p_mem_k = jnp.stack(mem_ks)
    p_mem_v = jnp.stack(mem_vs)
    return (y_prompt, y_sample,
            p_a_k, p_a_v, p_a_kidx, p_c_k, p_c_v, p_d_ckv, p_d_krope, p_b_conv, p_b_ssm, p_mem_k, p_mem_v,
            s_a_k, s_a_v, s_a_kidx, s_c_k, s_c_v, s_d_ckv, s_d_krope, s_b_conv, s_b_ssm)
```

```python
import functools
import math
import jax
import jax.numpy as jnp
from jax import lax
import numpy as np
from jax.experimental import pallas as pl
from jax.experimental.pallas import tpu as pltpu


D_MODEL = 1024
BATCH = 32
SEQ = 2048
DEPTH = 2
DEC_BATCH = 128
DEC_SEQ = 1
PAST_LEN = 16384
PAGE_SIZE = 128

N_MIXERS = 4
GROUP_WIDTH = D_MODEL // N_MIXERS
MIX_WIDTH = N_MIXERS * GROUP_WIDTH
HEAD_DIM = 64
ROPE_THETA = 10000.0
NORM_EPS = 1e-6
Q_BLOCK = 128

A_HEADS = GROUP_WIDTH // HEAD_DIM
A_KV_HEADS = 1
IDX_HEADS = 8
IDX_DIM = 32
TOPK_MAX = 256
IDX_SCALE = (IDX_HEADS * IDX_DIM) ** -0.5

SSM_P = HEAD_DIM
SSM_HEADS = GROUP_WIDTH // SSM_P
SSM_INNER = SSM_HEADS * SSM_P
SSM_GROUPS = 2
D_STATE = 128
CONV_W = 4
CONV_CH = SSM_INNER + 2 * SSM_GROUPS * D_STATE
SSD_CHUNK = 128

C_HEADS = GROUP_WIDTH // HEAD_DIM
C_KV_HEADS = 2

D_HEADS = GROUP_WIDTH // HEAD_DIM
D_NOPE = 64
D_ROPE = 32
D_V = GROUP_WIDTH // D_HEADS
Q_RANK = 256
KV_RANK = 128
MLA_SCALE = (D_NOPE + D_ROPE) ** -0.5

MEM_TOKENS = 256
MEM_HEADS = 4
MEM_HD = 64

D_FF = -(-8 * D_MODEL // (3 * 256)) * 256

IN_SIZES = (A_HEADS * HEAD_DIM, A_KV_HEADS * HEAD_DIM, A_KV_HEADS * HEAD_DIM, IDX_HEADS * IDX_DIM, IDX_DIM, IDX_HEADS,
            SSM_INNER, CONV_CH, SSM_HEADS,
            C_HEADS * HEAD_DIM, C_KV_HEADS * HEAD_DIM, C_KV_HEADS * HEAD_DIM,
            Q_RANK, KV_RANK, D_ROPE)
IN_TOTAL = sum(IN_SIZES)


def _in_offsets():
    return [int(o) for o in np.cumsum(IN_SIZES)[:-1]]


def _rmsnorm(x, g):
    xf = x.astype(jnp.float32)
    y = xf * lax.rsqrt(jnp.mean(xf * xf, axis=-1, keepdims=True) + NORM_EPS)
    return (y * g.astype(jnp.float32)).astype(x.dtype)


def _rope(x, pos):
    half = x.shape[-1] // 2
    inv = 1.0 / (ROPE_THETA ** (jnp.arange(half, dtype=jnp.float32) / half))
    ang = pos.astype(jnp.float32)[:, None] * inv[None, :]
    ang = ang.reshape((ang.shape[0],) + (1,) * (x.ndim - 3) + (half,))
    cos, sin = jnp.cos(ang), jnp.sin(ang)
    xf = x.astype(jnp.float32)
    x1, x2 = xf[..., :half], xf[..., half:]
    return jnp.concatenate([x1 * cos - x2 * sin, x2 * cos + x1 * sin], axis=-1).astype(x.dtype)


_LANES = 128
_VMEM_LIMIT = 48 * 1024 * 1024


def _row_tile(n, cap):
    tm = min(n, cap)
    assert n % tm == 0
    return tm


def _rms(x, g):
    return x * lax.rsqrt(jnp.mean(x * x, axis=-1, keepdims=True) + NORM_EPS) * g


def _norm_matmul_kernel(x_ref, g_ref, w_ref, o_ref):
    o_ref[...] = jnp.dot(_rms(x_ref[...], g_ref[...]).astype(_BF16), w_ref[...], preferred_element_type=_F32)


def _norm_matmul(x, g, w, tm_cap=256):
    n, k = x.shape
    m = w.shape[1]
    mp = -(-m // _LANES) * _LANES
    wb = jnp.pad(w.astype(_BF16), ((0, 0), (0, mp - m)))
    tm = _row_tile(n, tm_cap)
    out = pl.pallas_call(
        _norm_matmul_kernel,
        grid=(n // tm,),
        in_specs=[pl.BlockSpec((tm, k), lambda i: (i, 0)), pl.BlockSpec((1, k), lambda i: (0, 0)),
                  pl.BlockSpec((k, mp), lambda i: (0, 0))],
        out_specs=pl.BlockSpec((tm, mp), lambda i: (i, 0)),
        out_shape=jax.ShapeDtypeStruct((n, mp), _F32),
        compiler_params=pltpu.CompilerParams(dimension_semantics=("parallel",), vmem_limit_bytes=_VMEM_LIMIT),
        name="norm_matmul",
    )(x, g.reshape(1, k).astype(_F32), wb)
    return out[:, :m] if mp != m else out


def _matmul_residual_kernel(x_ref, w_ref, r_ref, o_ref):
    o_ref[...] = r_ref[...] + jnp.dot(x_ref[...].astype(_BF16), w_ref[...], preferred_element_type=_F32)


def _matmul_residual(x, w, res, tm_cap=256):
    n, k = x.shape
    m = w.shape[1]
    tm = _row_tile(n, tm_cap)
    return pl.pallas_call(
        _matmul_residual_kernel,
        grid=(n // tm,),
        in_specs=[pl.BlockSpec((tm, k), lambda i: (i, 0)), pl.BlockSpec((k, m), lambda i: (0, 0)),
                  pl.BlockSpec((tm, m), lambda i: (i, 0))],
        out_specs=pl.BlockSpec((tm, m), lambda i: (i, 0)),
        out_shape=jax.ShapeDtypeStruct((n, m), _F32),
        compiler_params=pltpu.CompilerParams(dimension_semantics=("parallel",), vmem_limit_bytes=_VMEM_LIMIT),
        name="matmul_residual",
    )(x, w.astype(_BF16), res)


_FF_SPLIT = 2


def _ffn_kernel(h_ref, g_ref, wg_ref, wu_ref, wd_ref, o_ref, hn_sc, acc_sc):
    k = pl.program_id(1)

    @pl.when(k == 0)
    def _():
        h = h_ref[...]
        hn_sc[...] = _rms(h, g_ref[...]).astype(_BF16)
        acc_sc[...] = h

    hn = hn_sc[...]
    gate = jnp.dot(hn, wg_ref[...], preferred_element_type=_F32)
    up = jnp.dot(hn, wu_ref[...], preferred_element_type=_F32)
    act = (gate * jax.nn.sigmoid(gate) * up).astype(_BF16)
    acc_sc[...] += jnp.dot(act, wd_ref[...], preferred_element_type=_F32)

    @pl.when(k == pl.num_programs(1) - 1)
    def _():
        o_ref[...] = acc_sc[...]


def _ffn(h, g, w_gate, w_up, w_down, tm_cap=512):
    n, dm = h.shape
    ff = w_gate.shape[1]
    tf = ff // _FF_SPLIT
    assert tf * _FF_SPLIT == ff and tf % _LANES == 0
    tm = _row_tile(n, tm_cap)
    return pl.pallas_call(
        _ffn_kernel,
        grid=(n // tm, _FF_SPLIT),
        in_specs=[pl.BlockSpec((tm, dm), lambda i, k: (i, 0)), pl.BlockSpec((1, dm), lambda i, k: (0, 0)),
                  pl.BlockSpec((dm, tf), lambda i, k: (0, k)), pl.BlockSpec((dm, tf), lambda i, k: (0, k)),
                  pl.BlockSpec((tf, dm), lambda i, k: (k, 0))],
        out_specs=pl.BlockSpec((tm, dm), lambda i, k: (i, 0)),
        out_shape=jax.ShapeDtypeStruct((n, dm), _F32),
        scratch_shapes=[pltpu.VMEM((tm, dm), _BF16), pltpu.VMEM((tm, dm), _F32)],
        compiler_params=pltpu.CompilerParams(dimension_semantics=("parallel", "arbitrary"),
                                             vmem_limit_bytes=_VMEM_LIMIT),
        name="ffn",
    )(h, g.reshape(1, dm).astype(_F32), w_gate.astype(_BF16), w_up.astype(_BF16), w_down.astype(_BF16))


def _mix_mem_kernel(h_ref, mix_ref, wout_ref, gq_ref, wmq_ref, mk_ref, mvt_ref, wmo_ref, o_ref):
    heads, hd = mk_ref.shape[1], mk_ref.shape[3]
    h1 = h_ref[...] + jnp.dot(mix_ref[...].astype(_BF16), wout_ref[...], preferred_element_type=_F32)
    q = jnp.dot(_rms(h1, gq_ref[...]).astype(_BF16), wmq_ref[...], preferred_element_type=_F32)
    outs = []
    for hh in range(heads):
        qh = q[:, hh * hd:(hh + 1) * hd].astype(_BF16)
        sc = lax.dot_general(mk_ref[0, hh], qh, (((1,), (1,)), ((), ())), preferred_element_type=_F32) * (hd ** -0.5)
        p = jnp.exp(sc - jnp.max(sc, axis=0, keepdims=True))
        o_t = jnp.dot(mvt_ref[0, hh], p.astype(_BF16), preferred_element_type=_F32)
        outs.append(o_t / jnp.sum(p, axis=0, keepdims=True))
    o = jnp.concatenate(outs, axis=0).T
    o_ref[...] = h1 + jnp.dot(o.astype(_BF16), wmo_ref[...], preferred_element_type=_F32)


def _mix_mem(h, mix, mem_k, mem_v, p, rows_per_batch, tm_cap=256):
    n, dm = h.shape
    b, mt, heads, hd = mem_k.shape
    tm = _row_tile(rows_per_batch, tm_cap)
    per = rows_per_batch // tm
    mk = jnp.transpose(mem_k, (0, 2, 1, 3)).astype(_BF16)
    mvt = jnp.transpose(mem_v, (0, 2, 3, 1)).astype(_BF16)
    const = lambda i: (0, 0)
    return pl.pallas_call(
        _mix_mem_kernel,
        grid=(n // tm,),
        in_specs=[pl.BlockSpec((tm, dm), lambda i: (i, 0)), pl.BlockSpec((tm, dm), lambda i: (i, 0)),
                  pl.BlockSpec((dm, dm), const), pl.BlockSpec((1, dm), const),
                  pl.BlockSpec((dm, heads * hd), const),
                  pl.BlockSpec((1, heads, mt, hd), lambda i: (i // per, 0, 0, 0)),
                  pl.BlockSpec((1, heads, hd, mt), lambda i: (i // per, 0, 0, 0)),
                  pl.BlockSpec((heads * hd, dm), const)],
        out_specs=pl.BlockSpec((tm, dm), lambda i: (i, 0)),
        out_shape=jax.ShapeDtypeStruct((n, dm), _F32),
        compiler_params=pltpu.CompilerParams(dimension_semantics=("parallel",), vmem_limit_bytes=_VMEM_LIMIT),
        name="mix_mem",
    )(h, mix, p['w_out'].astype(_BF16), p['g_mem_q'].reshape(1, dm).astype(_F32), p['w_mq'].astype(_BF16),
      mk, mvt, p['w_mo'].astype(_BF16))


def _gather_pages(pool, l, page_table):
    g = pool[l, page_table]
    return g.reshape((g.shape[0], g.shape[1] * g.shape[2]) + g.shape[3:])


def _gather_rows(pool, l, page_table, new, idx, past_len):
    page = jnp.clip(idx // PAGE_SIZE, 0, page_table.shape[1] - 1)
    phys = jnp.take_along_axis(page_table, page, axis=1)
    rows = pool[l, phys, idx % PAGE_SIZE]
    extra = (1,) * (new.ndim - 2)
    j = jnp.clip(idx - past_len, 0, new.shape[1] - 1)
    fresh = jnp.take_along_axis(new, j.reshape(j.shape + extra), axis=1)
    return jnp.where((idx < past_len).reshape(idx.shape + extra), rows.astype(new.dtype), fresh)


def _norm_matmul_multi_kernel(x_ref, g_ref, w_ref, *o_refs):
    y = jnp.dot(_rms(x_ref[...], g_ref[...]).astype(_BF16), w_ref[...], preferred_element_type=_F32)
    off = 0
    for o_ref in o_refs:
        o_ref[...] = y[:, off:off + o_ref.shape[1]]
        off += o_ref.shape[1]


def _norm_matmul_multi(x, g, w_groups, tm_cap=256):
    n, k = x.shape
    pads = [-(-w.shape[1] // _LANES) * _LANES for w in w_groups]
    wb = jnp.concatenate([jnp.pad(w.astype(_BF16), ((0, 0), (0, mp - w.shape[1]))) for w, mp in zip(w_groups, pads)],
                         axis=1)
    tm = _row_tile(n, tm_cap)
    return pl.pallas_call(
        _norm_matmul_multi_kernel,
        grid=(n // tm,),
        in_specs=[pl.BlockSpec((tm, k), lambda i: (i, 0)), pl.BlockSpec((1, k), lambda i: (0, 0)),
                  pl.BlockSpec((k, sum(pads)), lambda i: (0, 0))],
        out_specs=[pl.BlockSpec((tm, mp), lambda i: (i, 0)) for mp in pads],
        out_shape=[jax.ShapeDtypeStruct((n, mp), _F32) for mp in pads],
        compiler_params=pltpu.CompilerParams(dimension_semantics=("parallel",), vmem_limit_bytes=_VMEM_LIMIT),
        name="norm_matmul_multi",
    )(x, g.reshape(1, k).astype(_F32), wb)


def _project(h, g_mix, pos, p):
    b, L, dm = h.shape
    cols = jnp.split(p['w_in'], _in_offsets(), axis=1)
    cat = lambda ids: jnp.concatenate([cols[i] for i in ids], axis=1)
    outs = _norm_matmul_multi(h.reshape(b * L, dm), g_mix,
                              [cols[0], cols[3], cols[6], cols[7], cols[9], cols[10], cols[11], cols[12], cols[13],
                               cat([1, 2]), cat([4, 14, 5, 8])])
    a_q, a_qi, b_z, b_xbc, c_q, c_k, c_v, d_cq, d_ckv, kv_a, small = [o.reshape(b, L, -1) for o in outs]
    a_k, a_v = kv_a[..., :HEAD_DIM], kv_a[..., HEAD_DIM:2 * HEAD_DIM]
    o1, o2, o3 = IDX_DIM, IDX_DIM + D_ROPE, IDX_DIM + D_ROPE + IDX_HEADS
    a_ki, d_kr, a_w, b_dt = small[..., :o1], small[..., o1:o2], small[..., o2:o3], small[..., o3:o3 + SSM_HEADS]
    q_d = _norm_matmul(d_cq.reshape(b * L, Q_RANK), p['d_q_norm_g'], p['d_w_uq']).reshape(
        b, L, D_HEADS, D_NOPE + D_ROPE)
    return {
        'd_ckv_raw': d_ckv,
        'a_q': _rope(a_q.reshape(b, L, A_KV_HEADS, A_HEADS // A_KV_HEADS, HEAD_DIM), pos),
        'a_k': _rope(a_k.reshape(b, L, A_KV_HEADS, HEAD_DIM), pos),
        'a_v': a_v.reshape(b, L, A_KV_HEADS, HEAD_DIM),
        'a_qi': _rope(a_qi.reshape(b, L, IDX_HEADS, IDX_DIM), pos),
        'a_ki': _rope(a_ki, pos),
        'a_w': a_w,
        'b_z': b_z, 'b_xbc': b_xbc, 'b_dt': b_dt,
        'c_q': c_q.reshape(b, L, C_KV_HEADS, C_HEADS // C_KV_HEADS, HEAD_DIM),
        'c_k': c_k.reshape(b, L, C_KV_HEADS, HEAD_DIM),
        'c_v': c_v.reshape(b, L, C_KV_HEADS, HEAD_DIM),
        'd_qn': q_d[..., :D_NOPE],
        'd_qr': _rope(q_d[..., D_NOPE:], pos),
        'd_ckv': _rmsnorm(d_ckv, p['d_kv_norm_g']),
        'd_kr': _rope(d_kr, pos),
    }


def _indexer_scores(q_idx, w_idx, k_idx, q_pos, k_pos):
    rel = jax.nn.relu(jnp.einsum('bqhe,bse->bqhs', q_idx, k_idx).astype(jnp.float32))
    sc = jnp.einsum('bqh,bqhs->bqs', w_idx.astype(jnp.float32), rel) * IDX_SCALE
    return jnp.where(k_pos[None, None, :] <= q_pos[None, :, None], sc, -jnp.inf)


def _sparse_attend(q, k_sel, v_sel, valid):
    sc = jnp.einsum('btgrd,btkgd->btgrk', q, k_sel).astype(jnp.float32) * HEAD_DIM ** -0.5
    sc = jnp.where(valid[:, :, None, None, :], sc, -jnp.inf)
    pr = jax.nn.softmax(sc, axis=-1).astype(v_sel.dtype)
    return jnp.einsum('btgrk,btkgd->btgrd', pr, v_sel)


_BF16 = jnp.bfloat16
_F32 = jnp.float32
_INT_MIN = -2 ** 31
_NEG_BIG = -1e30
_TQ = 256
_TK = 128


def _sortable_key(x):
    u = lax.bitcast_convert_type(x, jnp.int32)
    return jnp.where(u < 0, -(u & 0x7FFFFFFF), u)


def _loop_pairs(n_pairs, body, init):
    return lax.fori_loop(0, n_pairs, lambda i, cr: body(2 * i + 1, body(2 * i, cr)), init)


def _kth_largest_key(key_sc, nchunks, n_sel):
    groups = _TK // _SUB

    def counts_ge(cands):
        def body(c, cnts):
            kk = key_sc[pl.ds(pl.multiple_of(c * _TK, _TK), _TK), :]
            return tuple(cnt + jnp.where(kk >= cand, 1, 0).reshape(groups, _SUB, _TQ).sum(axis=0)
                         for cnt, cand in zip(cnts, cands))
        zero = jnp.zeros((_SUB, _TQ), jnp.int32)
        cnts = lax.fori_loop(0, nchunks, body, (zero,) * len(cands))
        return [jnp.sum(cnt, axis=0, keepdims=True) for cnt in cnts]

    def two_bits(i, carry):
        t, cnt_t = carry
        hi = lax.shift_left(jnp.int32(1), 31 - 2 * i)
        lo = lax.shift_left(jnp.int32(1), 30 - 2 * i)
        cands = [t ^ lo, t ^ hi, t ^ hi ^ lo]
        for cand, cnt in zip(cands, counts_ge(cands)):
            ok = cnt >= n_sel
            t, cnt_t = jnp.where(ok, cand, t), jnp.where(ok, cnt, cnt_t)
        return t, cnt_t

    t0 = jnp.full((1, _TQ), _INT_MIN, jnp.int32)
    c0 = jnp.zeros((1, _TQ), jnp.int32) + nchunks * _TK
    return lax.fori_loop(0, 16, two_bits, (t0, c0))


def _drop_late_ties(key_sc, nchunks, thr, n_sel):
    def count_gt(c, cnt):
        kk = key_sc[pl.ds(pl.multiple_of(c * _TK, _TK), _TK), :]
        return cnt + jnp.where(kk > thr, 1, 0)
    n_gt = jnp.sum(lax.fori_loop(0, nchunks, count_gt, jnp.zeros((_TK, _TQ), jnp.int32)), axis=0, keepdims=True)
    room = (n_sel - n_gt).astype(_F32)
    row = lax.broadcasted_iota(jnp.int32, (_TK, _TK), 0)
    col = lax.broadcasted_iota(jnp.int32, (_TK, _TK), 1)
    before = jnp.where(col < row, 1.0, 0.0).astype(_BF16)

    def body(c, seen):
        sl = pl.ds(pl.multiple_of(c * _TK, _TK), _TK)
        kk = key_sc[sl, :]
        eq = kk == thr
        eqf = jnp.where(eq, 1.0, 0.0)
        rank = jnp.dot(before, eqf.astype(_BF16), preferred_element_type=_F32) + seen
        key_sc[sl, :] = jnp.where(eq & (rank >= room), _INT_MIN, kk)
        return seen + jnp.sum(eqf, axis=0, keepdims=True)

    lax.fori_loop(0, nchunks, body, jnp.zeros((1, _TQ), _F32))


def _dsa_prompt_kernel(q_ref, qi_ref, wt_ref, ki_ref, k_ref, vt_ref, o_ref, key_sc, *, n_sel):
    j = pl.program_id(1)
    nchunks = (j + 1) * (_TQ // _TK)
    heads, d = q_ref.shape[1], q_ref.shape[3]
    ih, e = qi_ref.shape[1], qi_ref.shape[3]
    qidx = qi_ref[0].reshape(ih * _TQ, e)
    wt = wt_ref[0]
    t_pos = j * _TQ + lax.broadcasted_iota(jnp.int32, (_TK, _TQ), 1)
    s_loc = lax.broadcasted_iota(jnp.int32, (_TK, _TQ), 0)

    def score_chunk(c, _):
        sl = pl.ds(pl.multiple_of(c * _TK, _TK), _TK)
        dots = lax.dot_general(ki_ref[0, sl, :], qidx, (((1,), (1,)), ((), ())), preferred_element_type=_F32)
        acc = jnp.zeros((_TK, _TQ), _F32)
        for h in range(ih):
            acc = acc + wt[h:h + 1, :] * jnp.maximum(dots[:, h * _TQ:(h + 1) * _TQ], 0.0)
        sc = jnp.where(c * _TK + s_loc <= t_pos, acc * IDX_SCALE, -jnp.inf)
        key_sc[sl, :] = _sortable_key(sc)
        return 0

    _loop_pairs(nchunks // 2, score_chunk, 0)
    thr, cnt_thr = _kth_largest_key(key_sc, nchunks, n_sel)

    @pl.when(jnp.max(cnt_thr) > n_sel)
    def _():
        _drop_late_ties(key_sc, nchunks, thr, n_sel)

    q_all = q_ref[0].reshape(heads * _TQ, d)
    t_pos_w = j * _TQ + lax.broadcasted_iota(jnp.int32, (_TK, _TQ), 1)

    def att_chunk(c, carry):
        m, l, acc = carry
        sl = pl.ds(pl.multiple_of(c * _TK, _TK), _TK)
        sel = (key_sc[sl, :] >= thr) & (c * _TK + s_loc <= t_pos_w)
        bias1 = jnp.where(sel, 0.0, _NEG_BIG)
        bias = jnp.concatenate([bias1] * heads, axis=1)
        logit = lax.dot_general(k_ref[0, sl, :], q_all, (((1,), (1,)), ((), ())),
                                preferred_element_type=_F32) * (d ** -0.5) + bias
        m_new = jnp.maximum(m, jnp.max(logit, axis=0, keepdims=True))
        p = jnp.exp(logit - m_new)
        alpha = jnp.exp(m - m_new)
        l = alpha * l + jnp.sum(p, axis=0, keepdims=True)
        acc = alpha * acc + jnp.dot(vt_ref[0, :, sl], p.astype(_BF16), preferred_element_type=_F32)
        return m_new, l, acc

    m0 = jnp.full((1, heads * _TQ), _NEG_BIG, _F32)
    l0 = jnp.zeros((1, heads * _TQ), _F32)
    a0 = jnp.zeros((d, heads * _TQ), _F32)
    _, l, acc = _loop_pairs(nchunks // 2, att_chunk, (m0, l0, a0))
    out_t = acc / l
    stacked = jnp.concatenate([out_t[:, h * _TQ:(h + 1) * _TQ] for h in range(heads)], axis=0)
    o_ref[0] = stacked.T


def _dsa_prompt(q, k, v, q_idx, k_idx, w_idx):
    b, L = q.shape[:2]
    heads, d = q.shape[3], q.shape[4]
    ih, e = q_idx.shape[2], q_idx.shape[3]
    n_sel = min(TOPK_MAX, L // 4)
    qh = jnp.transpose(q.reshape(b, L, heads, d), (0, 2, 1, 3)).astype(_BF16)
    qih = jnp.transpose(q_idx, (0, 2, 1, 3)).astype(_BF16)
    wt = jnp.transpose(w_idx, (0, 2, 1)).astype(_F32)
    kk = k.reshape(b, L, d).astype(_BF16)
    vt = jnp.transpose(v.reshape(b, L, d), (0, 2, 1)).astype(_BF16)
    return pl.pallas_call(
        functools.partial(_dsa_prompt_kernel, n_sel=n_sel),
        grid=(b, L // _TQ),
        in_specs=[
            pl.BlockSpec((1, heads, _TQ, d), lambda i, j: (i, 0, j, 0)),
            pl.BlockSpec((1, ih, _TQ, e), lambda i, j: (i, 0, j, 0)),
            pl.BlockSpec((1, ih, _TQ), lambda i, j: (i, 0, j)),
            pl.BlockSpec((1, L, e), lambda i, j: (i, 0, 0)),
            pl.BlockSpec((1, L, d), lambda i, j: (i, 0, 0)),
            pl.BlockSpec((1, d, L), lambda i, j: (i, 0, 0)),
        ],
        out_specs=pl.BlockSpec((1, _TQ, heads * d), lambda i, j: (i, j, 0)),
        out_shape=jax.ShapeDtypeStruct((b, L, heads * d), _F32),
        scratch_shapes=[pltpu.VMEM((L, _TQ), jnp.int32)],
        compiler_params=pltpu.CompilerParams(dimension_semantics=("parallel", "arbitrary")),
        name="dsa_prompt",
    )(qh, qih, wt, k_idx.astype(_BF16), kk, vt)


def _ssd_scan(x, dt, a, bm, cm, s0):
    b, l, h, p = x.shape
    g, n = bm.shape[2], bm.shape[3]
    r = h // g
    q = SSD_CHUNK if l % SSD_CHUNK == 0 else l
    c = l // q
    f32 = jnp.float32
    xdt = (x.astype(f32) * dt[..., None]).reshape(b, c, q, g, r, p)
    acs = jnp.cumsum((dt * a).reshape(b, c, q, g, r), axis=2)
    bc = bm.astype(f32).reshape(b, c, q, g, n)
    cc = cm.astype(f32).reshape(b, c, q, g, n)
    acs_t = jnp.moveaxis(acs, 2, -1)
    tril = jnp.tril(jnp.ones((q, q), dtype=bool))
    seg = jnp.exp(jnp.where(tril, acs_t[..., :, None] - acs_t[..., None, :], -jnp.inf))
    cb = jnp.einsum('bcign,bcjgn->bcgij', cc, bc)
    y_diag = jnp.einsum('bcgrij,bcjgrp->bcigrp', cb[:, :, :, None] * seg, xdt)
    decay_end = jnp.exp(acs[:, :, -1:] - acs)
    states = jnp.einsum('bcjgn,bcjgrp->bcgrpn', bc, xdt * decay_end[..., None])
    chunk_decay = jnp.exp(acs[:, :, -1])

    def step(s, inp):
        st, dc = inp
        return dc[..., None, None] * s + st, s

    s_fin, s_in = lax.scan(step, s0.astype(f32).reshape(b, g, r, p, n),
                           (jnp.moveaxis(states, 1, 0), jnp.moveaxis(chunk_decay, 1, 0)))
    s_in = jnp.moveaxis(s_in, 0, 1)
    y_off = jnp.einsum('bcign,bcgrpn->bcigrp', cc, s_in) * jnp.exp(acs)[..., None]
    return (y_diag + y_off).reshape(b, l, h, p), s_fin.reshape(b, h, p, n)


def _ssd_mixer(z, xbc, dt_raw, conv_prev, ssm_prev, p):
    b, T = xbc.shape[:2]
    f32 = jnp.float32
    xin = jnp.concatenate([conv_prev.astype(xbc.dtype), xbc], axis=1)
    cw = p['b_conv_w']
    conv = p['b_conv_b'] + xin[:, 0:T] * cw[0]
    for w in range(1, CONV_W):
        conv = conv + xin[:, w:w + T] * cw[w]
    xbc_c = jax.nn.silu(conv)
    xs, bm, cm = jnp.split(xbc_c, [SSM_INNER, SSM_INNER + SSM_GROUPS * D_STATE], axis=-1)
    xs = xs.reshape(b, T, SSM_HEADS, SSM_P)
    bm = bm.reshape(b, T, SSM_GROUPS, D_STATE)
    cm = cm.reshape(b, T, SSM_GROUPS, D_STATE)
    dt = jax.nn.softplus(dt_raw.astype(f32) + p['b_dt_bias'].astype(f32))
    a = -jnp.exp(p['b_a_log'].astype(f32))
    y, s_fin = _ssd_scan(xs, dt, a, bm, cm, ssm_prev)
    y = y + p['b_d'].astype(f32)[:, None] * xs.astype(f32)
    y = y.reshape(b, T, SSM_INNER) * jax.nn.silu(z.astype(f32))
    out = _rmsnorm(y, p['b_norm_g']).astype(z.dtype)
    return out, xin[:, -(CONV_W - 1):], s_fin


def _split_dot(a, b_mat, a_is_exact):
    x = b_mat if a_is_exact else a
    hi = x.astype(_BF16)
    lo = (x - hi.astype(_F32)).astype(_BF16)
    if a_is_exact:
        return jnp.dot(a, hi, preferred_element_type=_F32) + jnp.dot(a, lo, preferred_element_type=_F32)
    return jnp.dot(hi, b_mat, preferred_element_type=_F32) + jnp.dot(lo, b_mat, preferred_element_type=_F32)


def _ssd_prompt_kernel(xbc_ref, z_ref, dtc_ref, dtr_ref, cw_ref, cb_ref, hc_ref, hr_ref, ng_ref,
                       y_ref, conv_ref, st_ref, buf_sc, st_sc):
    c = pl.program_id(1)
    q = xbc_ref.shape[1]
    heads = SSM_HEADS
    inner = z_ref.shape[2]
    hp = inner // heads
    n = (xbc_ref.shape[2] - inner) // (2 * SSM_GROUPS)
    rep = heads // SSM_GROUPS
    lead = _SUB - (CONV_W - 1)

    @pl.when(c == 0)
    def _():
        buf_sc[0:_SUB, :] = jnp.zeros((_SUB, buf_sc.shape[1]), _F32)
        st_sc[...] = jnp.zeros_like(st_sc)

    xbc = xbc_ref[0]
    buf_sc[_SUB:_SUB + q, :] = xbc
    conv = cb_ref[...] + buf_sc[lead:lead + q, :] * cw_ref[0:1, :]
    for w in range(1, CONV_W):
        conv = conv + buf_sc[lead + w:lead + w + q, :] * cw_ref[w:w + 1, :]
    act = conv * jax.nn.sigmoid(conv)
    tail = buf_sc[q + lead:q + _SUB, :]
    buf_sc[lead:_SUB, :] = tail

    dt_c = _softplus(dtc_ref[0] + hc_ref[0:1, :])
    dt_r = _softplus(dtr_ref[0] + hr_ref[:, 0:1])
    da_c = dt_c * -jnp.exp(hc_ref[1:2, :])
    da_r = dt_r * -jnp.exp(hr_ref[:, 1:2])
    ri = lax.broadcasted_iota(jnp.int32, (q, q), 0)
    ci = lax.broadcasted_iota(jnp.int32, (q, q), 1)
    causal = ci <= ri
    acs_c = _split_dot(jnp.where(causal, 1.0, 0.0).astype(_BF16), da_c, True)
    acs_r = _split_dot(da_r, jnp.where(ri <= ci, 1.0, 0.0).astype(_BF16), False)
    acs_end = acs_c[q - 1:q, :]
    decay_end = jnp.exp(acs_end - acs_c)
    grow = jnp.exp(acs_c)

    ys = []
    for hh in range(heads):
        g = hh // rep
        xs = act[:, hh * hp:(hh + 1) * hp]
        bm = act[:, inner + g * n:inner + (g + 1) * n]
        cm = act[:, inner + SSM_GROUPS * n + g * n:inner + SSM_GROUPS * n + (g + 1) * n]
        xdt = xs * dt_c[:, hh:hh + 1]
        cbm = lax.dot_general(cm.astype(_BF16), bm.astype(_BF16), _NT, preferred_element_type=_F32)
        seg = jnp.exp(jnp.where(causal, acs_c[:, hh:hh + 1] - acs_r[hh:hh + 1, :], -jnp.inf))
        y = jnp.dot((cbm * seg).astype(_BF16), xdt.astype(_BF16), preferred_element_type=_F32)
        s_in = st_sc[hh]
        y = y + jnp.dot(cm.astype(_BF16), s_in.astype(_BF16), preferred_element_type=_F32) * grow[:, hh:hh + 1]
        new = jnp.dot(bm.T.astype(_BF16), (xdt * decay_end[:, hh:hh + 1]).astype(_BF16), preferred_element_type=_F32)
        st_sc[hh] = jnp.exp(acs_end[:, hh:hh + 1]) * s_in + new
        ys.append(y + hr_ref[hh:hh + 1, 2:3] * xs)
    y = jnp.concatenate(ys, axis=1)
    zz = z_ref[0]
    y_ref[0] = _rms(y * (zz * jax.nn.sigmoid(zz)), ng_ref[...])

    @pl.when(c == pl.num_programs(1) - 1)
    def _():
        conv_ref[0] = tail
        st_ref[0] = st_sc[...]


def _ssd_prompt(z, xbc, dt_raw, p):
    b, L, inner = z.shape
    heads = dt_raw.shape[-1]
    ch = xbc.shape[-1]
    q = SSD_CHUNK
    assert L % q == 0
    assert heads == SSM_HEADS and heads <= _SUB
    hc = jnp.stack([p['b_dt_bias'], p['b_a_log'], p['b_d']]).astype(_F32)
    hc_l = jnp.pad(hc, ((0, 0), (0, _LANES - heads)))
    hr_s = jnp.pad(hc.T, ((0, _SUB - heads), (0, _LANES - 3)))
    dt_l = jnp.pad(dt_raw, ((0, 0), (0, 0), (0, _LANES - heads)))
    dt_s = jnp.pad(jnp.swapaxes(dt_raw, 1, 2), ((0, 0), (0, _SUB - heads), (0, 0)))
    y, conv_new, st = pl.pallas_call(
        _ssd_prompt_kernel,
        grid=(b, L // q),
        in_specs=[pl.BlockSpec((1, q, ch), lambda i, c: (i, c, 0)), pl.BlockSpec((1, q, inner), lambda i, c: (i, c, 0)),
                  pl.BlockSpec((1, q, _LANES), lambda i, c: (i, c, 0)), pl.BlockSpec((1, _SUB, q), lambda i, c: (i, 0, c)),
                  pl.BlockSpec((CONV_W, ch), lambda i, c: (0, 0)), pl.BlockSpec((1, ch), lambda i, c: (0, 0)),
                  pl.BlockSpec((3, _LANES), lambda i, c: (0, 0)), pl.BlockSpec((_SUB, _LANES), lambda i, c: (0, 0)),
                  pl.BlockSpec((1, inner), lambda i, c: (0, 0))],
        out_specs=[pl.BlockSpec((1, q, inner), lambda i, c: (i, c, 0)),
                   pl.BlockSpec((1, CONV_W - 1, ch), lambda i, c: (i, 0, 0)),
                   pl.BlockSpec((1, heads, D_STATE, inner // heads), lambda i, c: (i, 0, 0, 0))],
        out_shape=[jax.ShapeDtypeStruct((b, L, inner), _F32), jax.ShapeDtypeStruct((b, CONV_W - 1, ch), _F32),
                   jax.ShapeDtypeStruct((b, heads, D_STATE, inner // heads), _F32)],
        scratch_shapes=[pltpu.VMEM((q + _SUB, ch), _F32), pltpu.VMEM((heads, D_STATE, inner // heads), _F32)],
        compiler_params=pltpu.CompilerParams(dimension_semantics=("parallel", "arbitrary")),
        name="ssd_prompt",
    )(xbc, z, dt_l, dt_s, p['b_conv_w'].astype(_F32), p['b_conv_b'].reshape(1, ch).astype(_F32),
      hc_l, hr_s, p['b_norm_g'].reshape(1, inner).astype(_F32))
    return y, conv_new, jnp.swapaxes(st, 2, 3)


def _stick_breaking(q, k, v, q_pos, k_pos):
    z = jnp.einsum('btgrd,bsgd->bgrts', q, k).astype(jnp.float32) * HEAD_DIM ** -0.5
    m = k_pos[None, :] < q_pos[:, None]
    log_1m = jnp.where(m, jax.nn.log_sigmoid(-z), 0.0)
    cum = jnp.cumsum(log_1m, axis=-1)
    log_w = jax.nn.log_sigmoid(z) + cum[..., -1:] - cum
    w = jnp.where(m, jnp.exp(log_w), 0.0).astype(v.dtype)
    return jnp.einsum('bgrts,bsgd->btgrd', w, v)


def _heads_to_rows(out_t, heads):
    stacked = jnp.concatenate([out_t[:, h * _TQ:(h + 1) * _TQ] for h in range(heads)], axis=0)
    return stacked.T


def _softplus(z):
    return jnp.maximum(z, 0.0) + jnp.log(1.0 + jnp.exp(-jnp.abs(z)))


def _sb_prompt_kernel(q_ref, k_ref, vt_ref, o_ref, *, rep):
    j = pl.program_id(1)
    heads, d = q_ref.shape[1], q_ref.shape[3]
    groups = k_ref.shape[1]
    gw = rep * _TQ
    width = heads * _TQ
    qg = [q_ref[0, g * rep:(g + 1) * rep].reshape(gw, d) for g in range(groups)]
    row = lax.broadcasted_iota(jnp.int32, (_TK, _TK), 0)
    col = lax.broadcasted_iota(jnp.int32, (_TK, _TK), 1)
    after = jnp.where(col > row, 1.0, 0.0).astype(_BF16)
    s_loc = lax.broadcasted_iota(jnp.int32, (_TK, width), 0)
    t_loc = lax.broadcasted_iota(jnp.int32, (_TK, width), 1) & (_TQ - 1)
    diag = _TQ // _TK

    def chunk(c, carry, diag_off=None):
        masked = diag_off is not None
        if masked:
            strictly_before = s_loc + diag_off < t_loc
        tail, accs = carry
        sl = pl.ds(pl.multiple_of(c * _TK, _TK), _TK)
        z = jnp.concatenate(
            [lax.dot_general(k_ref[0, g, sl, :], qg[g], (((1,), (1,)), ((), ())), preferred_element_type=_F32)
             for g in range(groups)], axis=1) * (d ** -0.5)
        sp = _softplus(z)
        if masked:
            sp = jnp.where(strictly_before, sp, 0.0)
        hi = sp.astype(_BF16)
        lo = (sp - hi.astype(_F32)).astype(_BF16)
        later = (jnp.dot(after, hi, preferred_element_type=_F32)
                 + jnp.dot(after, lo, preferred_element_type=_F32))
        w = jnp.exp(z - sp - later - tail)
        if masked:
            w = jnp.where(strictly_before, w, 0.0)
        wb = w.astype(_BF16)
        accs = tuple(accs[g] + jnp.dot(vt_ref[0, g, :, sl], wb[:, g * gw:(g + 1) * gw],
                                       preferred_element_type=_F32) for g in range(groups))
        return tail + jnp.sum(sp, axis=0, keepdims=True), accs

    init = (jnp.zeros((1, width), _F32), tuple(jnp.zeros((d, gw), _F32) for _ in range(groups)))
    carry = init
    for i in reversed(range(diag)):
        carry = chunk(j * diag + i, carry, diag_off=i * _TK)
    assert diag % 2 == 0
    _, accs = _loop_pairs(j * (diag // 2), lambda i, cr: chunk(j * diag - 1 - i, cr), carry)
    o_ref[0] = _heads_to_rows(jnp.concatenate(accs, axis=1), heads)


def _sb_prompt(q, k, v):
    b, L, groups, rep, d = q.shape
    heads = groups * rep
    qh = jnp.transpose(q.reshape(b, L, heads, d), (0, 2, 1, 3)).astype(_BF16)
    kg = jnp.transpose(k, (0, 2, 1, 3)).astype(_BF16)
    vt = jnp.transpose(v, (0, 2, 3, 1)).astype(_BF16)
    return pl.pallas_call(
        functools.partial(_sb_prompt_kernel, rep=rep),
        grid=(b, L // _TQ),
        in_specs=[
            pl.BlockSpec((1, heads, _TQ, d), lambda i, j: (i, 0, j, 0)),
            pl.BlockSpec((1, groups, L, d), lambda i, j: (i, 0, 0, 0)),
            pl.BlockSpec((1, groups, d, L), lambda i, j: (i, 0, 0, 0)),
        ],
        out_specs=pl.BlockSpec((1, _TQ, heads * d), lambda i, j: (i, j, 0)),
        out_shape=jax.ShapeDtypeStruct((b, L, heads * d), _F32),
        compiler_params=pltpu.CompilerParams(dimension_semantics=("parallel", "arbitrary")),
        name="sb_prompt",
    )(qh, kg, vt)


def _causal_mha_kernel(q_ref, k_ref, vt_ref, o_ref, *, scale):
    j = pl.program_id(1)
    heads = q_ref.shape[1]
    dv = vt_ref.shape[2]
    width = heads * _TQ
    s_loc = lax.broadcasted_iota(jnp.int32, (_TK, width), 0)
    t_loc = lax.broadcasted_iota(jnp.int32, (_TK, width), 1) & (_TQ - 1)
    diag = _TQ // _TK

    def chunk(c, carry, diag_off=None):
        masked = diag_off is not None
        if masked:
            visible = s_loc + diag_off <= t_loc
        m, l, accs = carry
        sl = pl.ds(pl.multiple_of(c * _TK, _TK), _TK)
        logit = jnp.concatenate(
            [lax.dot_general(k_ref[0, h, sl, :], q_ref[0, h], (((1,), (1,)), ((), ())), preferred_element_type=_F32)
             for h in range(heads)], axis=1) * scale
        if masked:
            logit = jnp.where(visible, logit, _NEG_BIG)
        m_new = jnp.maximum(m, jnp.max(logit, axis=0, keepdims=True))
        p = jnp.exp(logit - m_new)
        alpha = jnp.exp(m - m_new)
        l = alpha * l + jnp.sum(p, axis=0, keepdims=True)
        pb = p.astype(_BF16)
        accs = tuple(alpha[:, h * _TQ:(h + 1) * _TQ] * accs[h]
                     + jnp.dot(vt_ref[0, h, :, sl], pb[:, h * _TQ:(h + 1) * _TQ], preferred_element_type=_F32)
                     for h in range(heads))
        return m_new, l, accs

    init = (jnp.full((1, width), _NEG_BIG, _F32), jnp.zeros((1, width), _F32),
            tuple(jnp.zeros((dv, _TQ), _F32) for _ in range(heads)))
    carry = init
    for i in range(diag):
        carry = chunk(j * diag + i, carry, diag_off=i * _TK)
    assert diag % 2 == 0
    _, l, accs = _loop_pairs(j * (diag // 2), lambda i, cr: chunk(i, cr), carry)
    o_ref[0] = _heads_to_rows(jnp.concatenate(accs, axis=1) / l, heads)


def _mla_prompt(q_nope, q_rope, k_nope, k_rope, v):
    b, L, heads, _ = q_nope.shape
    dv = v.shape[-1]
    q = jnp.transpose(jnp.concatenate([q_nope, q_rope], axis=-1), (0, 2, 1, 3)).astype(_BF16)
    kr = jnp.broadcast_to(k_rope[:, :, None, :], (b, L, heads, k_rope.shape[-1]))
    k = jnp.transpose(jnp.concatenate([k_nope, kr], axis=-1), (0, 2, 1, 3)).astype(_BF16)
    vt = jnp.transpose(v, (0, 2, 3, 1)).astype(_BF16)
    dq = q.shape[-1]
    return pl.pallas_call(
        functools.partial(_causal_mha_kernel, scale=MLA_SCALE),
        grid=(b, L // _TQ),
        in_specs=[
            pl.BlockSpec((1, heads, _TQ, dq), lambda i, j: (i, 0, j, 0)),
            pl.BlockSpec((1, heads, L, dq), lambda i, j: (i, 0, 0, 0)),
            pl.BlockSpec((1, heads, dv, L), lambda i, j: (i, 0, 0, 0)),
        ],
        out_specs=pl.BlockSpec((1, _TQ, heads * dv), lambda i, j: (i, j, 0)),
        out_shape=jax.ShapeDtypeStruct((b, L, heads * dv), _F32),
        compiler_params=pltpu.CompilerParams(dimension_semantics=("parallel", "arbitrary")),
        name="mla_prompt",
    )(q, k, vt)


def _mla_latent(q_nope, q_rope, c_all, r_all, w_uk, w_uv, q_pos, k_pos):
    b, T = q_nope.shape[:2]
    q_lat = jnp.einsum('bthd,chd->bthc', q_nope, w_uk)
    sc = (jnp.einsum('bthc,bsc->bhts', q_lat, c_all)
          + jnp.einsum('bthr,bsr->bhts', q_rope, r_all)).astype(jnp.float32) * MLA_SCALE
    sc = jnp.where(k_pos[None, :] <= q_pos[:, None], sc, -jnp.inf)
    pr = jax.nn.softmax(sc, axis=-1).astype(c_all.dtype)
    o_lat = jnp.einsum('bhts,bsc->bthc', pr, c_all)
    return jnp.einsum('bthc,chd->bthd', o_lat, w_uv).reshape(b, T, D_HEADS * D_V)


def _mixer_prompt(h, g_mix, pos, p):
    b, L = h.shape[:2]
    u = _project(h, g_mix, pos, p)
    a_out = _dsa_prompt(u['a_q'], u['a_k'], u['a_v'], u['a_qi'], u['a_ki'], u['a_w'])
    b_out, conv_new, ssm_new = _ssd_prompt(u['b_z'], u['b_xbc'], u['b_dt'], p)
    c_out = _sb_prompt(u['c_q'], u['c_k'], u['c_v'])
    w_ukv = jnp.concatenate([p['d_w_uk'].reshape(KV_RANK, D_HEADS * D_NOPE),
                             p['d_w_uv'].reshape(KV_RANK, D_HEADS * D_V)], axis=1)
    kv_up = _norm_matmul(u['d_ckv_raw'].reshape(b * L, KV_RANK), p['d_kv_norm_g'], w_ukv)
    k_nope = kv_up[:, :D_HEADS * D_NOPE].reshape(b, L, D_HEADS, D_NOPE)
    v_d = kv_up[:, D_HEADS * D_NOPE:].reshape(b, L, D_HEADS, D_V)
    d_out = _mla_prompt(u['d_qn'], u['d_qr'], k_nope, u['d_kr'], v_d)
    mix = jnp.concatenate([a_out, b_out, c_out, d_out], axis=-1)
    new = (u['a_k'], u['a_v'], u['a_ki'], u['c_k'], u['c_v'], u['d_ckv'], u['d_kr'],
           conv_new, ssm_new.astype(h.dtype))
    return mix, new


_PPS = 16
_SUB = 8


def _page_specs(rows, width, layer, n_pages, descending):
    specs = []
    for slot in range(_PPS):
        if descending:
            idx = lambda b, c, pt, slot=slot: (layer, pt[b, n_pages - (c + 1) * _PPS + slot], 0, 0)
        else:
            idx = lambda b, c, pt, slot=slot: (layer, pt[b, c * _PPS + slot], 0, 0)
        specs.append(pl.BlockSpec((None, None, rows, width), idx))
    return specs


def _seq_spec(rows, width):
    return pl.BlockSpec((1, rows, width), lambda b, c, pt: (b, 0, 0))


def _pad_rows(x):
    return jnp.pad(x, ((0, 0), (0, _SUB - x.shape[1]), (0, 0)))


_NT = (((1,), (1,)), ((), ()))


def _paged_call(kernel_fn, page_table, seq_inputs, pools, layer, descending, out_shape, out_spec, scratch, name):
    b, n_pages = page_table.shape
    assert n_pages % _PPS == 0
    in_specs = [_seq_spec(x.shape[1], x.shape[2]) for x in seq_inputs]
    args = list(seq_inputs)
    for pool in pools:
        in_specs += _page_specs(pool.shape[2], pool.shape[3], layer, n_pages, descending)
        args += [pool] * _PPS
    return pl.pallas_call(
        kernel_fn,
        grid_spec=pltpu.PrefetchScalarGridSpec(
            num_scalar_prefetch=1, grid=(b, n_pages // _PPS),
            in_specs=in_specs, out_specs=out_spec, scratch_shapes=scratch),
        out_shape=out_shape,
        compiler_params=pltpu.CompilerParams(dimension_semantics=("parallel", "arbitrary")),
        name=name,
    )(page_table, *args)


def _pool_t(pool):
    lead = pool.shape[:2]
    flat = pool.reshape(lead + (PAGE_SIZE, -1))
    return jnp.swapaxes(flat, 2, 3)


def _suffix_sum(x):
    width = x.shape[1]
    lane = lax.broadcasted_iota(jnp.int32, x.shape, 1)
    k = 1
    while k < width:
        x = x + jnp.where(lane < width - k, pltpu.roll(x, width - k, axis=1), 0.0)
        k *= 2
    return x


def _head_rows(rows):
    pad = jnp.zeros((_SUB - len(rows), rows[0].shape[1]), _F32)
    return jnp.concatenate(rows + [pad], axis=0)


def _sb_decode_kernel(pt_ref, q_ref, *refs, scale, heads, rep):
    k_refs, v_refs = refs[:_PPS], refs[_PPS:2 * _PPS]
    o_ref, tail_sc, acc_sc = refs[2 * _PPS:]
    c = pl.program_id(1)
    d = q_ref.shape[1] // heads

    @pl.when(c == 0)
    def _():
        tail_sc[...] = jnp.zeros_like(tail_sc)
        acc_sc[...] = jnp.zeros_like(acc_sc)

    zs = []
    for slot in range(_PPS):
        kt = k_refs[slot][...]
        rows = []
        for hh in range(heads):
            g = hh // rep
            rows.append(jnp.sum(kt[g * d:(g + 1) * d, :] * q_ref[0, hh * d:(hh + 1) * d, :], axis=0, keepdims=True))
        zs.append(_head_rows(rows))
    z = jnp.concatenate(zs, axis=1) * scale
    incl = _suffix_sum(_softplus(z))
    w = jnp.exp(z - incl - tail_sc[...][:, 0:1])
    for slot in range(_PPS):
        vt = v_refs[slot][...]
        for hh in range(heads):
            g = hh // rep
            acc_sc[hh * d:(hh + 1) * d, :] += (vt[g * d:(g + 1) * d, :]
                                               * w[hh:hh + 1, slot * PAGE_SIZE:(slot + 1) * PAGE_SIZE])
    tail_sc[...] = tail_sc[...] + incl[:, 0:1]

    @pl.when(c == pl.num_programs(1) - 1)
    def _():
        o_ref[0] = jnp.broadcast_to(jnp.sum(acc_sc[...], axis=1, keepdims=True), acc_sc.shape)


def _sb_decode(q, pool_k, pool_v, layer, page_table):
    b, t, groups, rep, d = q.shape
    assert t == 1
    heads = groups * rep
    qb = jnp.broadcast_to(q.reshape(b, heads * d, 1), (b, heads * d, PAGE_SIZE)).astype(_F32)
    out = _paged_call(
        functools.partial(_sb_decode_kernel, scale=d ** -0.5, heads=heads, rep=rep), page_table, [qb],
        [_pool_t(pool_k), _pool_t(pool_v)], layer, True, jax.ShapeDtypeStruct((b, heads * d, PAGE_SIZE), _F32),
        pl.BlockSpec((1, heads * d, PAGE_SIZE), lambda i, c, pt: (i, 0, 0)),
        [pltpu.VMEM((_SUB, PAGE_SIZE), _F32), pltpu.VMEM((heads * d, PAGE_SIZE), _F32)], "sb_decode")
    return out[:, :, 0]


def _softmax_pages(logits, values, m, l, acc):
    mx = logits[0].max(axis=1, keepdims=True)
    for lg in logits[1:]:
        mx = jnp.maximum(mx, lg.max(axis=1, keepdims=True))
    m_new = jnp.maximum(m, mx)
    alpha = jnp.exp(m - m_new)
    l = alpha * l
    acc = alpha[:, :acc.shape[1]] * acc
    for lg, val in zip(logits, values):
        pr = jnp.exp(lg - m_new)
        l = l + jnp.sum(pr, axis=1, keepdims=True)
        acc = acc + jnp.dot(pr.astype(_BF16), val, preferred_element_type=_F32)
    return m_new, l, acc


def _mla_decode_kernel(pt_ref, ql_ref, qr_ref, lnew_ref, cnew_ref, *refs, scale):
    c_refs, r_refs = refs[:_PPS], refs[_PPS:2 * _PPS]
    o_ref, m_sc, l_sc, acc_sc = refs[2 * _PPS:]
    c = pl.program_id(1)

    @pl.when(c == 0)
    def _():
        m_sc[...] = lnew_ref[0]
        l_sc[...] = jnp.ones_like(l_sc)
        acc_sc[...] = jnp.broadcast_to(cnew_ref[0], acc_sc.shape)

    ql, qr = ql_ref[0], qr_ref[0]
    lat = [c_refs[s][...].astype(_BF16) for s in range(_PPS)]
    logits = [(lax.dot_general(ql, lat[s], _NT, preferred_element_type=_F32)
               + jnp.dot(qr, r_refs[s][...].astype(_BF16), preferred_element_type=_F32)) * scale
              for s in range(_PPS)]
    m, l, acc = _softmax_pages(logits, lat, m_sc[...], l_sc[...], acc_sc[...])
    m_sc[...], l_sc[...], acc_sc[...] = m, l, acc

    @pl.when(c == pl.num_programs(1) - 1)
    def _():
        o_ref[0] = acc / l


def _mla_decode(q_nope, q_rope, c_new, r_new, pool_c, pool_r, w_uk, w_uv, layer, page_table):
    b, t, heads, _ = q_nope.shape
    assert t == 1
    q_lat = jnp.einsum('bthd,chd->bhc', q_nope, w_uk)
    qr = q_rope[:, 0]
    l_new = (jnp.einsum('bhc,bc->bh', q_lat, c_new[:, 0]) + jnp.einsum('bhr,br->bh', qr, r_new[:, 0])) * MLA_SCALE
    l_new = jnp.broadcast_to(_pad_rows(l_new[:, :, None]), (b, _SUB, PAGE_SIZE)).astype(_F32)
    rank = pool_c.shape[-1]
    o_lat = _paged_call(
        functools.partial(_mla_decode_kernel, scale=MLA_SCALE), page_table,
        [_pad_rows(q_lat).astype(_BF16), _pad_rows(qr).astype(_BF16), l_new, c_new.astype(_F32)],
        [pool_c, _pool_t(pool_r)], layer, False, jax.ShapeDtypeStruct((b, _SUB, rank), _F32),
        pl.BlockSpec((1, _SUB, rank), lambda i, c, pt: (i, 0, 0)),
        [pltpu.VMEM((_SUB, PAGE_SIZE), _F32), pltpu.VMEM((_SUB, PAGE_SIZE), _F32), pltpu.VMEM((_SUB, rank), _F32)],
        "mla_decode")
    return jnp.einsum('bhc,chd->bhd', o_lat[:, :heads], w_uv).reshape(b, t, heads * w_uv.shape[-1])


def _idx_decode_kernel(pt_ref, qi_ref, w_ref, *refs):
    ki_refs, o_ref = refs[:_PPS], refs[_PPS]
    qi, w = qi_ref[0], w_ref[0]
    for s in range(_PPS):
        dots = jnp.dot(qi, ki_refs[s][...].astype(_BF16), preferred_element_type=_F32)
        o_ref[0, s:s + 1, :] = jnp.sum(w * jnp.maximum(dots, 0.0), axis=0, keepdims=True) * IDX_SCALE


def _dsa_decode_kernel(pt_ref, sc_ref, scnew_ref, q_ref, lnew_ref, vnew_ref, *refs, n_sel, heads):
    k_refs, v_refs = refs[:_PPS], refs[_PPS:2 * _PPS]
    o_ref, bias_sc, m_sc, l_sc, acc_sc = refs[2 * _PPS:]
    c = pl.program_id(1)
    n_rows = sc_ref.shape[1]

    @pl.when(c == 0)
    def _():
        key = _sortable_key(sc_ref[0])
        key_new = _sortable_key(scnew_ref[0])

        def count(pred_tile, pred_new):
            return jnp.sum(jnp.where(pred_tile, 1, 0)) + jnp.max(jnp.where(pred_new, 1, 0))

        def bit_step(i, carry):
            t, cnt_t = carry
            cand = t ^ lax.shift_left(jnp.int32(1), 31 - i)
            cnt = count(key >= cand, key_new >= cand)
            ok = cnt >= n_sel
            return jnp.where(ok, cand, t), jnp.where(ok, cnt, cnt_t)

        thr, cnt_thr = lax.fori_loop(0, 32, bit_step, (jnp.int32(_INT_MIN), jnp.int32(n_rows * PAGE_SIZE + 1)))
        bias_sc[...] = jnp.where(key >= thr, 0.0, _NEG_BIG)
        sel_new = key_new >= thr

        def init(sel_new):
            m_sc[...] = jnp.where(sel_new, lnew_ref[0], _NEG_BIG)
            l_sc[...] = jnp.where(sel_new, 1.0, 0.0) + jnp.zeros_like(l_sc)
            acc_sc[...] = jnp.where(sel_new, vnew_ref[0], 0.0)

        init(sel_new)

        @pl.when(cnt_thr > n_sel)
        def _():
            room = (n_sel - count(key > thr, key_new > thr)).astype(_F32)
            eq = key == thr
            eqf = jnp.where(eq, 1.0, 0.0)
            r_i = lax.broadcasted_iota(jnp.int32, (PAGE_SIZE, PAGE_SIZE), 0)
            c_i = lax.broadcasted_iota(jnp.int32, (PAGE_SIZE, PAGE_SIZE), 1)
            in_row = jnp.dot(eqf.astype(_BF16), jnp.where(r_i < c_i, 1.0, 0.0).astype(_BF16),
                             preferred_element_type=_F32)
            row_tot = jnp.broadcast_to(jnp.sum(eqf, axis=1, keepdims=True), eqf.shape).astype(_BF16)
            rr = lax.broadcasted_iota(jnp.int32, (n_rows, n_rows), 0)
            rc = lax.broadcasted_iota(jnp.int32, (n_rows, n_rows), 1)
            rows_before = jnp.dot(jnp.where(rc < rr, 1.0, 0.0).astype(_BF16), row_tot, preferred_element_type=_F32)
            keep = (key > thr) | (eq & (in_row + rows_before < room))
            bias_sc[...] = jnp.where(keep, 0.0, _NEG_BIG)
            init((key_new > thr) | ((key_new == thr) & (jnp.sum(eqf) < room)))

    d = k_refs[0].shape[0]
    lg = []
    for s in range(_PPS):
        kt = k_refs[s][...]
        rows = [jnp.sum(kt * q_ref[0, hh * d:(hh + 1) * d, :], axis=0, keepdims=True) for hh in range(heads)]
        lg.append(_head_rows(rows) * (d ** -0.5) + bias_sc[pl.ds(c * _PPS + s, 1), :])
    logit = jnp.concatenate(lg, axis=1)
    m_old = m_sc[...][:, 0:1]
    m_new = jnp.maximum(m_old, jnp.max(logit, axis=1, keepdims=True))
    alpha = jnp.exp(m_old - m_new)
    pr = jnp.exp(logit - m_new)
    l_new = alpha * l_sc[...][:, 0:1] + jnp.sum(pr, axis=1, keepdims=True)
    for hh in range(heads):
        acc_h = alpha[hh:hh + 1, :] * acc_sc[hh * d:(hh + 1) * d, :]
        for s in range(_PPS):
            acc_h = acc_h + v_refs[s][...] * pr[hh:hh + 1, s * PAGE_SIZE:(s + 1) * PAGE_SIZE]
        acc_sc[hh * d:(hh + 1) * d, :] = acc_h
    m_sc[...] = jnp.broadcast_to(m_new, m_sc.shape)
    l_sc[...] = jnp.broadcast_to(l_new, l_sc.shape)

    @pl.when(c == pl.num_programs(1) - 1)
    def _():
        for hh in range(heads):
            tot = jnp.sum(acc_sc[hh * d:(hh + 1) * d, :], axis=1, keepdims=True) / l_new[hh:hh + 1, :]
            o_ref[0, hh * d:(hh + 1) * d, :] = jnp.broadcast_to(tot, (d, PAGE_SIZE))


def _dsa_decode(q, k_new, v_new, q_idx, ki_new, w_idx, pool_k, pool_v, pool_ki, layer, page_table):
    b, t = q.shape[:2]
    assert t == 1
    heads, d = q.shape[3], q.shape[4]
    ih, e = q_idx.shape[2], q_idx.shape[3]
    n_pages = page_table.shape[1]
    n_sel = min(TOPK_MAX, (n_pages * PAGE_SIZE + t) // 4)
    assert ih == _SUB
    w8 = jnp.broadcast_to(w_idx[:, 0, :, None], (b, ih, PAGE_SIZE)).astype(_F32)
    scores = _paged_call(
        _idx_decode_kernel, page_table, [q_idx[:, 0].astype(_BF16), w8], [_pool_t(pool_ki)], layer, False,
        jax.ShapeDtypeStruct((b, n_pages, PAGE_SIZE), _F32),
        pl.BlockSpec((1, _PPS, PAGE_SIZE), lambda i, c, pt: (i, c, 0)), [], "idx_decode")
    rel_new = jax.nn.relu(jnp.einsum('bhe,be->bh', q_idx[:, 0], ki_new[:, 0]))
    sc_new = jnp.einsum('bh,bh->b', w_idx[:, 0], rel_new) * IDX_SCALE
    sc_new = jnp.broadcast_to(sc_new[:, None, None], (b, 1, PAGE_SIZE)).astype(_F32)
    qh = q.reshape(b, heads, d)
    l_new = jnp.einsum('bhd,bd->bh', qh, k_new.reshape(b, d)) * d ** -0.5
    l_new = jnp.broadcast_to(_pad_rows(l_new[:, :, None]), (b, _SUB, PAGE_SIZE)).astype(_F32)
    qb = jnp.broadcast_to(qh.reshape(b, heads * d, 1), (b, heads * d, PAGE_SIZE)).astype(_F32)
    v_rep = jnp.tile(v_new.reshape(b, d), (1, heads))
    v_lane0 = jnp.zeros((b, heads * d, PAGE_SIZE), _F32).at[:, :, 0].set(v_rep)
    out = _paged_call(
        functools.partial(_dsa_decode_kernel, n_sel=n_sel, heads=heads), page_table,
        [scores, sc_new, qb, l_new, v_lane0],
        [_pool_t(pool_k), _pool_t(pool_v)], layer, False, jax.ShapeDtypeStruct((b, heads * d, PAGE_SIZE), _F32),
        pl.BlockSpec((1, heads * d, PAGE_SIZE), lambda i, c, pt: (i, 0, 0)),
        [pltpu.VMEM((n_pages, PAGE_SIZE), _F32), pltpu.VMEM((_SUB, PAGE_SIZE), _F32),
         pltpu.VMEM((_SUB, PAGE_SIZE), _F32), pltpu.VMEM((heads * d, PAGE_SIZE), _F32)], "dsa_decode")
    return out[:, :, 0].reshape(b, t, heads * d)


def _mixer_sample(h, g_mix, pos, p, l, caches, page_table, past_len):
    (ca_k, ca_v, ca_ki, cc_k, cc_v, cd_ckv, cd_kr, sb_conv, sb_ssm) = caches
    hn = h
    b, T = h.shape[:2]
    L = past_len + T
    k_pos = jnp.arange(L)
    u = _project(h, g_mix, pos, p)
    a_out = _dsa_decode(u['a_q'], u['a_k'], u['a_v'], u['a_qi'], u['a_ki'], u['a_w'], ca_k, ca_v, ca_ki, l, page_table)
    b_out, conv_new, ssm_new = _ssd_mixer(u['b_z'], u['b_xbc'], u['b_dt'], sb_conv[l], sb_ssm[l], p)
    c_out = _sb_decode(u['c_q'], cc_k, cc_v, l, page_table).reshape(b, T, C_HEADS * HEAD_DIM)
    d_out = _mla_decode(u['d_qn'], u['d_qr'], u['d_ckv'], u['d_kr'], cd_ckv, cd_kr, p['d_w_uk'], p['d_w_uv'],
                        l, page_table)
    mix = jnp.concatenate([a_out, b_out, c_out, d_out], axis=-1)
    new = (u['a_k'], u['a_v'], u['a_ki'], u['c_k'], u['c_v'], u['d_ckv'], u['d_kr'],
           conv_new.astype(sb_conv.dtype), ssm_new.astype(sb_ssm.dtype))
    return mix, new


def _mem_kv(mem, p):
    b, mt, dm = mem.shape
    kv = _norm_matmul(mem.reshape(b * mt, dm), p['g_mem_kv'], jnp.concatenate([p['w_mk'], p['w_mv']], axis=1))
    width = MEM_HEADS * MEM_HD
    return kv[:, :width].reshape(b, mt, MEM_HEADS, MEM_HD), kv[:, width:].reshape(b, mt, MEM_HEADS, MEM_HD)


def _post_mix_prompt(h, mix, mem_k, mem_v, p):
    b, L, dm = h.shape
    h2 = _mix_mem(h.reshape(b * L, dm), mix.reshape(b * L, dm), mem_k, mem_v, p, rows_per_batch=L)
    return _ffn(h2, p['g_ffn'], p['w_gate'], p['w_up'], p['w_down']).reshape(b, L, dm)


def _post_mix_sample(h, mix, mem_k, mem_v, p):
    b, L, dm = h.shape
    h1 = _matmul_residual(mix.reshape(b * L, dm), p['w_out'], h.reshape(b * L, dm))
    q = _norm_matmul(h1, p['g_mem_q'], p['w_mq']).reshape(b, L, MEM_HEADS, MEM_HD)
    sc = jnp.einsum('blhd,bmhd->bhlm', q, mem_k.astype(q.dtype)).astype(jnp.float32) * MEM_HD ** -0.5
    pr = jax.nn.softmax(sc, axis=-1).astype(q.dtype)
    o = jnp.einsum('bhlm,bmhd->blhd', pr, mem_v.astype(q.dtype)).reshape(b * L, MEM_HEADS * MEM_HD)
    h2 = _matmul_residual(o, p['w_mo'], h1)
    return _ffn(h2, p['g_ffn'], p['w_gate'], p['w_up'], p['w_down']).reshape(b, L, dm)


def _final_norm_kernel(x_ref, g_ref, o_ref):
    x = x_ref[...]
    o_ref[...] = x * lax.rsqrt(jnp.mean(x * x, axis=-1, keepdims=True) + NORM_EPS) * g_ref[...]


def _final_norm(x, g):
    shp = x.shape
    x2 = x.reshape(-1, shp[-1])
    n = x2.shape[0]
    tm = min(n, 1024)
    out = pl.pallas_call(
        _final_norm_kernel,
        grid=(n // tm,),
        in_specs=[pl.BlockSpec((tm, shp[-1]), lambda i: (i, 0)), pl.BlockSpec((1, shp[-1]), lambda i: (0, 0))],
        out_specs=pl.BlockSpec((tm, shp[-1]), lambda i: (i, 0)),
        out_shape=jax.ShapeDtypeStruct(x2.shape, x2.dtype),
    )(x2, g.reshape(1, -1))
    return out.reshape(shp)


def kernel(x_prompt, x_sample, mem_prompt, cache_a_k, cache_a_v, cache_a_kidx, cache_c_k, cache_c_v, cache_d_ckv, cache_d_krope, state_b_conv, state_b_ssm, cache_mem_k, cache_mem_v, page_table, g_mix, w_in, b_conv_w, b_conv_b, b_dt_bias, b_a_log, b_d, b_norm_g, d_q_norm_g, d_kv_norm_g, d_w_uq, d_w_uk, d_w_uv, w_out, g_mem_q, g_mem_kv, w_mq, w_mk, w_mv, w_mo, g_ffn, w_gate, w_up, w_down, g_final):
    s_prompt = x_prompt.shape[1]
    t_new = x_sample.shape[1]
    past_len = page_table.shape[1] * PAGE_SIZE
    pos_p = jnp.arange(s_prompt)
    pos_s = past_len + jnp.arange(t_new)
    caches = (cache_a_k, cache_a_v, cache_a_kidx, cache_c_k, cache_c_v, cache_d_ckv, cache_d_krope,
              state_b_conv, state_b_ssm)
    hp, hs = x_prompt, x_sample
    new_p, new_s, mem_ks, mem_vs = [], [], [], []
    for l in range(DEPTH):
        p = {'w_in': w_in[l], 'b_conv_w': b_conv_w[l], 'b_conv_b': b_conv_b[l], 'b_dt_bias': b_dt_bias[l],
             'b_a_log': b_a_log[l], 'b_d': b_d[l], 'b_norm_g': b_norm_g[l],
             'd_q_norm_g': d_q_norm_g[l], 'd_kv_norm_g': d_kv_norm_g[l], 'd_w_uq': d_w_uq[l],
             'd_w_uk': d_w_uk[l], 'd_w_uv': d_w_uv[l], 'w_out': w_out[l],
             'g_mem_q': g_mem_q[l], 'g_mem_kv': g_mem_kv[l], 'w_mq': w_mq[l], 'w_mk': w_mk[l],
             'w_mv': w_mv[l], 'w_mo': w_mo[l], 'g_ffn': g_ffn[l], 'w_gate': w_gate[l],
             'w_up': w_up[l], 'w_down': w_down[l]}
        mix_p, st_p = _mixer_prompt(hp, g_mix[l], pos_p, p)
        mk, mv = _mem_kv(mem_prompt, p)
        hp = _post_mix_prompt(hp, mix_p, mk, mv, p)
        new_p.append(st_p)
        mem_ks.append(mk)
        mem_vs.append(mv)
        mix_s, st_s = _mixer_sample(hs, g_mix[l], pos_s, p, l, caches, page_table, past_len)
        hs = _post_mix_sample(hs, mix_s, cache_mem_k[l], cache_mem_v[l], p)
        new_s.append(st_s)
    y_prompt = _final_norm(hp, g_final)
    y_sample = _final_norm(hs, g_final)
    (p_a_k, p_a_v, p_a_kidx, p_c_k, p_c_v, p_d_ckv, p_d_krope, p_b_conv, p_b_ssm) = [jnp.stack(t) for t in zip(*new_p)]
    (s_a_k, s_a_v, s_a_kidx, s_c_k, s_c_v, s_d_ckv, s_d_krope, s_b_conv, s_b_ssm) = [jnp.stack(t) for t in zip(*new_s)]
    p_mem_k = jnp.stack(mem_ks)
    p_mem_v = jnp.stack(mem_vs)
    return (y_prompt, y_sample,
            p_a_k, p_a_v, p_a_kidx, p_c_k, p_c_v, p_d_ckv, p_d_krope, p_b_conv, p_b_ssm, p_mem_k, p_mem_v,
            s_a_k, s_a_v, s_a_kidx, s_c_k, s_c_v, s_d_ckv, s_d_krope, s_b_conv, s_b_ssm)
```

```python
import functools
import math
import jax
import jax.numpy as jnp
from jax import lax
import numpy as np
from jax.experimental import pallas as pl
from jax.experimental.pallas import tpu as pltpu


D_MODEL = 1024
BATCH = 32
SEQ = 2048
DEPTH = 2
DEC_BATCH = 128
DEC_SEQ = 1
PAST_LEN = 16384
PAGE_SIZE = 128

N_MIXERS = 4
GROUP_WIDTH = D_MODEL // N_MIXERS
MIX_WIDTH = N_MIXERS * GROUP_WIDTH
HEAD_DIM = 64
ROPE_THETA = 10000.0
NORM_EPS = 1e-6
Q_BLOCK = 128

A_HEADS = GROUP_WIDTH // HEAD_DIM
A_KV_HEADS = 1
IDX_HEADS = 8
IDX_DIM = 32
TOPK_MAX = 256
IDX_SCALE = (IDX_HEADS * IDX_DIM) ** -0.5

SSM_P = HEAD_DIM
SSM_HEADS = GROUP_WIDTH // SSM_P
SSM_INNER = SSM_HEADS * SSM_P
SSM_GROUPS = 2
D_STATE = 128
CONV_W = 4
CONV_CH = SSM_INNER + 2 * SSM_GROUPS * D_STATE
SSD_CHUNK = 128

C_HEADS = GROUP_WIDTH // HEAD_DIM
C_KV_HEADS = 2

D_HEADS = GROUP_WIDTH // HEAD_DIM
D_NOPE = 64
D_ROPE = 32
D_V = GROUP_WIDTH // D_HEADS
Q_RANK = 256
KV_RANK = 128
MLA_SCALE = (D_NOPE + D_ROPE) ** -0.5

MEM_TOKENS = 256
MEM_HEADS = 4
MEM_HD = 64

D_FF = -(-8 * D_MODEL // (3 * 256)) * 256

IN_SIZES = (A_HEADS * HEAD_DIM, A_KV_HEADS * HEAD_DIM, A_KV_HEADS * HEAD_DIM, IDX_HEADS * IDX_DIM, IDX_DIM, IDX_HEADS,
            SSM_INNER, CONV_CH, SSM_HEADS,
            C_HEADS * HEAD_DIM, C_KV_HEADS * HEAD_DIM, C_KV_HEADS * HEAD_DIM,
            Q_RANK, KV_RANK, D_ROPE)
IN_TOTAL = sum(IN_SIZES)


def _in_offsets():
    return [int(o) for o in np.cumsum(IN_SIZES)[:-1]]


def _rmsnorm(x, g):
    xf = x.astype(jnp.float32)
    y = xf * lax.rsqrt(jnp.mean(xf * xf, axis=-1, keepdims=True) + NORM_EPS)
    return (y * g.astype(jnp.float32)).astype(x.dtype)


def _rope(x, pos):
    half = x.shape[-1] // 2
    inv = 1.0 / (ROPE_THETA ** (jnp.arange(half, dtype=jnp.float32) / half))
    ang = pos.astype(jnp.float32)[:, None] * inv[None, :]
    ang = ang.reshape((ang.shape[0],) + (1,) * (x.ndim - 3) + (half,))
    cos, sin = jnp.cos(ang), jnp.sin(ang)
    xf = x.astype(jnp.float32)
    x1, x2 = xf[..., :half], xf[..., half:]
    return jnp.concatenate([x1 * cos - x2 * sin, x2 * cos + x1 * sin], axis=-1).astype(x.dtype)


_LANES = 128
_VMEM_LIMIT = 48 * 1024 * 1024


def _row_tile(n, cap):
    tm = min(n, cap)
    assert n % tm == 0
    return tm


def _rms(x, g):
    return x * lax.rsqrt(jnp.mean(x * x, axis=-1, keepdims=True) + NORM_EPS) * g


def _norm_matmul_kernel(x_ref, g_ref, w_ref, o_ref):
    o_ref[...] = jnp.dot(_rms(x_ref[...], g_ref[...]).astype(_BF16), w_ref[...], preferred_element_type=_F32)


def _norm_matmul(x, g, w, tm_cap=256):
    n, k = x.shape
    m = w.shape[1]
    mp = -(-m // _LANES) * _LANES
    wb = jnp.pad(w.astype(_BF16), ((0, 0), (0, mp - m)))
    tm = _row_tile(n, tm_cap)
    out = pl.pallas_call(
        _norm_matmul_kernel,
        grid=(n // tm,),
        in_specs=[pl.BlockSpec((tm, k), lambda i: (i, 0)), pl.BlockSpec((1, k), lambda i: (0, 0)),
                  pl.BlockSpec((k, mp), lambda i: (0, 0))],
        out_specs=pl.BlockSpec((tm, mp), lambda i: (i, 0)),
        out_shape=jax.ShapeDtypeStruct((n, mp), _F32),
        compiler_params=pltpu.CompilerParams(dimension_semantics=("parallel",), vmem_limit_bytes=_VMEM_LIMIT),
        name="norm_matmul",
    )(x, g.reshape(1, k).astype(_F32), wb)
    return out[:, :m] if mp != m else out


def _matmul_residual_kernel(x_ref, w_ref, r_ref, o_ref):
    o_ref[...] = r_ref[...] + jnp.dot(x_ref[...].astype(_BF16), w_ref[...], preferred_element_type=_F32)


def _matmul_residual(x, w, res, tm_cap=256):
    n, k = x.shape
    m = w.shape[1]
    tm = _row_tile(n, tm_cap)
    return pl.pallas_call(
        _matmul_residual_kernel,
        grid=(n // tm,),
        in_specs=[pl.BlockSpec((tm, k), lambda i: (i, 0)), pl.BlockSpec((k, m), lambda i: (0, 0)),
                  pl.BlockSpec((tm, m), lambda i: (i, 0))],
        out_specs=pl.BlockSpec((tm, m), lambda i: (i, 0)),
        out_shape=jax.ShapeDtypeStruct((n, m), _F32),
        compiler_params=pltpu.CompilerParams(dimension_semantics=("parallel",), vmem_limit_bytes=_VMEM_LIMIT),
        name="matmul_residual",
    )(x, w.astype(_BF16), res)


_FF_SPLIT = 2


def _ffn_kernel(h_ref, g_ref, wg_ref, wu_ref, wd_ref, o_ref, hn_sc, acc_sc):
    k = pl.program_id(1)

    @pl.when(k == 0)
    def _():
        h = h_ref[...]
        hn_sc[...] = _rms(h, g_ref[...]).astype(_BF16)
        acc_sc[...] = h

    hn = hn_sc[...]
    gate = jnp.dot(hn, wg_ref[...], preferred_element_type=_F32)
    up = jnp.dot(hn, wu_ref[...], preferred_element_type=_F32)
    act = (gate * jax.nn.sigmoid(gate) * up).astype(_BF16)
    acc_sc[...] += jnp.dot(act, wd_ref[...], preferred_element_type=_F32)

    @pl.when(k == pl.num_programs(1) - 1)
    def _():
        o_ref[...] = acc_sc[...]


def _ffn(h, g, w_gate, w_up, w_down, tm_cap=512):
    n, dm = h.shape
    ff = w_gate.shape[1]
    tf = ff // _FF_SPLIT
    assert tf * _FF_SPLIT == ff and tf % _LANES == 0
    tm = _row_tile(n, tm_cap)
    return pl.pallas_call(
        _ffn_kernel,
        grid=(n // tm, _FF_SPLIT),
        in_specs=[pl.BlockSpec((tm, dm), lambda i, k: (i, 0)), pl.BlockSpec((1, dm), lambda i, k: (0, 0)),
                  pl.BlockSpec((dm, tf), lambda i, k: (0, k)), pl.BlockSpec((dm, tf), lambda i, k: (0, k)),
                  pl.BlockSpec((tf, dm), lambda i, k: (k, 0))],
        out_specs=pl.BlockSpec((tm, dm), lambda i, k: (i, 0)),
        out_shape=jax.ShapeDtypeStruct((n, dm), _F32),
        scratch_shapes=[pltpu.VMEM((tm, dm), _BF16), pltpu.VMEM((tm, dm), _F32)],
        compiler_params=pltpu.CompilerParams(dimension_semantics=("parallel", "arbitrary"),
                                             vmem_limit_bytes=_VMEM_LIMIT),
        name="ffn",
    )(h, g.reshape(1, dm).astype(_F32), w_gate.astype(_BF16), w_up.astype(_BF16), w_down.astype(_BF16))


def _mix_mem_kernel(h_ref, mix_ref, wout_ref, gq_ref, wmq_ref, mk_ref, mvt_ref, wmo_ref, o_ref):
    heads, hd = mk_ref.shape[1], mk_ref.shape[3]
    h1 = h_ref[...] + jnp.dot(mix_ref[...].astype(_BF16), wout_ref[...], preferred_element_type=_F32)
    q = jnp.dot(_rms(h1, gq_ref[...]).astype(_BF16), wmq_ref[...], preferred_element_type=_F32)
    outs = []
    for hh in range(heads):
        qh = q[:, hh * hd:(hh + 1) * hd].astype(_BF16)
        sc = lax.dot_general(mk_ref[0, hh], qh, (((1,), (1,)), ((), ())), preferred_element_type=_F32) * (hd ** -0.5)
        p = jnp.exp(sc - jnp.max(sc, axis=0, keepdims=True))
        o_t = jnp.dot(mvt_ref[0, hh], p.astype(_BF16), preferred_element_type=_F32)
        outs.append(o_t / jnp.sum(p, axis=0, keepdims=True))
    o = jnp.concatenate(outs, axis=0).T
    o_ref[...] = h1 + jnp.dot(o.astype(_BF16), wmo_ref[...], preferred_element_type=_F32)


def _mix_mem(h, mix, mem_k, mem_v, p, rows_per_batch, tm_cap=256):
    n, dm = h.shape
    b, mt, heads, hd = mem_k.shape
    tm = _row_tile(rows_per_batch, tm_cap)
    per = rows_per_batch // tm
    mk = jnp.transpose(mem_k, (0, 2, 1, 3)).astype(_BF16)
    mvt = jnp.transpose(mem_v, (0, 2, 3, 1)).astype(_BF16)
    const = lambda i: (0, 0)
    return pl.pallas_call(
        _mix_mem_kernel,
        grid=(n // tm,),
        in_specs=[pl.BlockSpec((tm, dm), lambda i: (i, 0)), pl.BlockSpec((tm, dm), lambda i: (i, 0)),
                  pl.BlockSpec((dm, dm), const), pl.BlockSpec((1, dm), const),
                  pl.BlockSpec((dm, heads * hd), const),
                  pl.BlockSpec((1, heads, mt, hd), lambda i: (i // per, 0, 0, 0)),
                  pl.BlockSpec((1, heads, hd, mt), lambda i: (i // per, 0, 0, 0)),
                  pl.BlockSpec((heads * hd, dm), const)],
        out_specs=pl.BlockSpec((tm, dm), lambda i: (i, 0)),
        out_shape=jax.ShapeDtypeStruct((n, dm), _F32),
        compiler_params=pltpu.CompilerParams(dimension_semantics=("parallel",), vmem_limit_bytes=_VMEM_LIMIT),
        name="mix_mem",
    )(h, mix, p['w_out'].astype(_BF16), p['g_mem_q'].reshape(1, dm).astype(_F32), p['w_mq'].astype(_BF16),
      mk, mvt, p['w_mo'].astype(_BF16))


def _gather_pages(pool, l, page_table):
    g = pool[l, page_table]
    return g.reshape((g.shape[0], g.shape[1] * g.shape[2]) + g.shape[3:])


def _gather_rows(pool, l, page_table, new, idx, past_len):
    page = jnp.clip(idx // PAGE_SIZE, 0, page_table.shape[1] - 1)
    phys = jnp.take_along_axis(page_table, page, axis=1)
    rows = pool[l, phys, idx % PAGE_SIZE]
    extra = (1,) * (new.ndim - 2)
    j = jnp.clip(idx - past_len, 0, new.shape[1] - 1)
    fresh = jnp.take_along_axis(new, j.reshape(j.shape + extra), axis=1)
    return jnp.where((idx < past_len).reshape(idx.shape + extra), rows.astype(new.dtype), fresh)


def _norm_matmul_multi_kernel(x_ref, g_ref, w_ref, *o_refs):
    y = jnp.dot(_rms(x_ref[...], g_ref[...]).astype(_BF16), w_ref[...], preferred_element_type=_F32)
    off = 0
    for o_ref in o_refs:
        o_ref[...] = y[:, off:off + o_ref.shape[1]]
        off += o_ref.shape[1]


def _norm_matmul_multi(x, g, w_groups, tm_cap=256):
    n, k = x.shape
    pads = [-(-w.shape[1] // _LANES) * _LANES for w in w_groups]
    wb = jnp.concatenate([jnp.pad(w.astype(_BF16), ((0, 0), (0, mp - w.shape[1]))) for w, mp in zip(w_groups, pads)],
                         axis=1)
    tm = _row_tile(n, tm_cap)
    return pl.pallas_call(
        _norm_matmul_multi_kernel,
        grid=(n // tm,),
        in_specs=[pl.BlockSpec((tm, k), lambda i: (i, 0)), pl.BlockSpec((1, k), lambda i: (0, 0)),
                  pl.BlockSpec((k, sum(pads)), lambda i: (0, 0))],
        out_specs=[pl.BlockSpec((tm, mp), lambda i: (i, 0)) for mp in pads],
        out_shape=[jax.ShapeDtypeStruct((n, mp), _F32) for mp in pads],
        compiler_params=pltpu.CompilerParams(dimension_semantics=("parallel",), vmem_limit_bytes=_VMEM_LIMIT),
        name="norm_matmul_multi",
    )(x, g.reshape(1, k).astype(_F32), wb)


def _project(h, g_mix, pos, p):
    b, L, dm = h.shape
    cols = jnp.split(p['w_in'], _in_offsets(), axis=1)
    cat = lambda ids: jnp.concatenate([cols[i] for i in ids], axis=1)
    outs = _norm_matmul_multi(h.reshape(b * L, dm), g_mix,
                              [cols[0], cols[3], cols[6], cols[7], cols[9], cols[10], cols[11], cols[12], cols[13],
                               cat([1, 2]), cat([4, 14, 5, 8])])
    a_q, a_qi, b_z, b_xbc, c_q, c_k, c_v, d_cq, d_ckv, kv_a, small = [o.reshape(b, L, -1) for o in outs]
    a_k, a_v = kv_a[..., :HEAD_DIM], kv_a[..., HEAD_DIM:2 * HEAD_DIM]
    o1, o2, o3 = IDX_DIM, IDX_DIM + D_ROPE, IDX_DIM + D_ROPE + IDX_HEADS
    a_ki, d_kr, a_w, b_dt = small[..., :o1], small[..., o1:o2], small[..., o2:o3], small[..., o3:o3 + SSM_HEADS]
    q_d = _norm_matmul(d_cq.reshape(b * L, Q_RANK), p['d_q_norm_g'], p['d_w_uq']).reshape(
        b, L, D_HEADS, D_NOPE + D_ROPE)
    return {
        'd_ckv_raw': d_ckv,
        'a_q': _rope(a_q.reshape(b, L, A_KV_HEADS, A_HEADS // A_KV_HEADS, HEAD_DIM), pos),
        'a_k': _rope(a_k.reshape(b, L, A_KV_HEADS, HEAD_DIM), pos),
        'a_v': a_v.reshape(b, L, A_KV_HEADS, HEAD_DIM),
        'a_qi': _rope(a_qi.reshape(b, L, IDX_HEADS, IDX_DIM), pos),
        'a_ki': _rope(a_ki, pos),
        'a_w': a_w,
        'b_z': b_z, 'b_xbc': b_xbc, 'b_dt': b_dt,
        'c_q': c_q.reshape(b, L, C_KV_HEADS, C_HEADS // C_KV_HEADS, HEAD_DIM),
        'c_k': c_k.reshape(b, L, C_KV_HEADS, HEAD_DIM),
        'c_v': c_v.reshape(b, L, C_KV_HEADS, HEAD_DIM),
        'd_qn': q_d[..., :D_NOPE],
        'd_qr': _rope(q_d[..., D_NOPE:], pos),
        'd_ckv': _rmsnorm(d_ckv, p['d_kv_norm_g']),
        'd_kr': _rope(d_kr, pos),
    }


def _indexer_scores(q_idx, w_idx, k_idx, q_pos, k_pos):
    rel = jax.nn.relu(jnp.einsum('bqhe,bse->bqhs', q_idx, k_idx).astype(jnp.float32))
    sc = jnp.einsum('bqh,bqhs->bqs', w_idx.astype(jnp.float32), rel) * IDX_SCALE
    return jnp.where(k_pos[None, None, :] <= q_pos[None, :, None], sc, -jnp.inf)


def _sparse_attend(q, k_sel, v_sel, valid):
    sc = jnp.einsum('btgrd,btkgd->btgrk', q, k_sel).astype(jnp.float32) * HEAD_DIM ** -0.5
    sc = jnp.where(valid[:, :, None, None, :], sc, -jnp.inf)
    pr = jax.nn.softmax(sc, axis=-1).astype(v_sel.dtype)
    return jnp.einsum('btgrk,btkgd->btgrd', pr, v_sel)


_BF16 = jnp.bfloat16
_F32 = jnp.float32
_INT_MIN = -2 ** 31
_NEG_BIG = -1e30
_TQ = 256
_TK = 128


def _sortable_key(x):
    u = lax.bitcast_convert_type(x, jnp.int32)
    return jnp.where(u < 0, -(u & 0x7FFFFFFF), u)


def _loop_pairs(n_pairs, body, init):
    return lax.fori_loop(0, n_pairs, lambda i, cr: body(2 * i + 1, body(2 * i, cr)), init)


def _kth_largest_key(key_sc, nchunks, n_sel):
    groups = _TK // _SUB

    def counts_ge(cands):
        def body(c, cnts):
            kk = key_sc[pl.ds(pl.multiple_of(c * _TK, _TK), _TK), :]
            return tuple(cnt + jnp.where(kk >= cand, 1, 0).reshape(groups, _SUB, _TQ).sum(axis=0)
                         for cnt, cand in zip(cnts, cands))
        zero = jnp.zeros((_SUB, _TQ), jnp.int32)
        cnts = lax.fori_loop(0, nchunks, body, (zero,) * len(cands))
        return [jnp.sum(cnt, axis=0, keepdims=True) for cnt in cnts]

    def two_bits(i, carry):
        t, cnt_t = carry
        hi = lax.shift_left(jnp.int32(1), 31 - 2 * i)
        lo = lax.shift_left(jnp.int32(1), 30 - 2 * i)
        cands = [t ^ lo, t ^ hi, t ^ hi ^ lo]
        for cand, cnt in zip(cands, counts_ge(cands)):
            ok = cnt >= n_sel
            t, cnt_t = jnp.where(ok, cand, t), jnp.where(ok, cnt, cnt_t)
        return t, cnt_t

    t0 = jnp.full((1, _TQ), _INT_MIN, jnp.int32)
    c0 = jnp.zeros((1, _TQ), jnp.int32) + nchunks * _TK
    return lax.fori_loop(0, 16, two_bits, (t0, c0))


def _drop_late_ties(key_sc, nchunks, thr, n_sel):
    def count_gt(c, cnt):
        kk = key_sc[pl.ds(pl.multiple_of(c * _TK, _TK), _TK), :]
        return cnt + jnp.where(kk > thr, 1, 0)
    n_gt = jnp.sum(lax.fori_loop(0, nchunks, count_gt, jnp.zeros((_TK, _TQ), jnp.int32)), axis=0, keepdims=True)
    room = (n_sel - n_gt).astype(_F32)
    row = lax.broadcasted_iota(jnp.int32, (_TK, _TK), 0)
    col = lax.broadcasted_iota(jnp.int32, (_TK, _TK), 1)
    before = jnp.where(col < row, 1.0, 0.0).astype(_BF16)

    def body(c, seen):
        sl = pl.ds(pl.multiple_of(c * _TK, _TK), _TK)
        kk = key_sc[sl, :]
        eq = kk == thr
        eqf = jnp.where(eq, 1.0, 0.0)
        rank = jnp.dot(before, eqf.astype(_BF16), preferred_element_type=_F32) + seen
        key_sc[sl, :] = jnp.where(eq & (rank >= room), _INT_MIN, kk)
        return seen + jnp.sum(eqf, axis=0, keepdims=True)

    lax.fori_loop(0, nchunks, body, jnp.zeros((1, _TQ), _F32))


def _dsa_prompt_kernel(q_ref, qi_ref, wt_ref, ki_ref, k_ref, vt_ref, o_ref, key_sc, *, n_sel):
    j = pl.program_id(1)
    nchunks = (j + 1) * (_TQ // _TK)
    heads, d = q_ref.shape[1], q_ref.shape[3]
    ih, e = qi_ref.shape[1], qi_ref.shape[3]
    qidx = qi_ref[0].reshape(ih * _TQ, e)
    wt = wt_ref[0]
    t_pos = j * _TQ + lax.broadcasted_iota(jnp.int32, (_TK, _TQ), 1)
    s_loc = lax.broadcasted_iota(jnp.int32, (_TK, _TQ), 0)

    def score_chunk(c, _):
        sl = pl.ds(pl.multiple_of(c * _TK, _TK), _TK)
        dots = lax.dot_general(ki_ref[0, sl, :], qidx, (((1,), (1,)), ((), ())), preferred_element_type=_F32)
        acc = jnp.zeros((_TK, _TQ), _F32)
        for h in range(ih):
            acc = acc + wt[h:h + 1, :] * jnp.maximum(dots[:, h * _TQ:(h + 1) * _TQ], 0.0)
        sc = jnp.where(c * _TK + s_loc <= t_pos, acc * IDX_SCALE, -jnp.inf)
        key_sc[sl, :] = _sortable_key(sc)
        return 0

    _loop_pairs(nchunks // 2, score_chunk, 0)
    thr, cnt_thr = _kth_largest_key(key_sc, nchunks, n_sel)

    @pl.when(jnp.max(cnt_thr) > n_sel)
    def _():
        _drop_late_ties(key_sc, nchunks, thr, n_sel)

    q_all = q_ref[0].reshape(heads * _TQ, d)
    t_pos_w = j * _TQ + lax.broadcasted_iota(jnp.int32, (_TK, _TQ), 1)

    def att_chunk(c, carry):
        m, l, acc = carry
        sl = pl.ds(pl.multiple_of(c * _TK, _TK), _TK)
        sel = (key_sc[sl, :] >= thr) & (c * _TK + s_loc <= t_pos_w)
        bias1 = jnp.where(sel, 0.0, _NEG_BIG)
        bias = jnp.concatenate([bias1] * heads, axis=1)
        logit = lax.dot_general(k_ref[0, sl, :], q_all, (((1,), (1,)), ((), ())),
                                preferred_element_type=_F32) * (d ** -0.5) + bias
        m_new = jnp.maximum(m, jnp.max(logit, axis=0, keepdims=True))
        p = jnp.exp(logit - m_new)
        alpha = jnp.exp(m - m_new)
        l = alpha * l + jnp.sum(p, axis=0, keepdims=True)
        acc = alpha * acc + jnp.dot(vt_ref[0, :, sl], p.astype(_BF16), preferred_element_type=_F32)
        return m_new, l, acc

    m0 = jnp.full((1, heads * _TQ), _NEG_BIG, _F32)
    l0 = jnp.zeros((1, heads * _TQ), _F32)
    a0 = jnp.zeros((d, heads * _TQ), _F32)
    _, l, acc = _loop_pairs(nchunks // 2, att_chunk, (m0, l0, a0))
    out_t = acc / l
    stacked = jnp.concatenate([out_t[:, h * _TQ:(h + 1) * _TQ] for h in range(heads)], axis=0)
    o_ref[0] = stacked.T


def _dsa_prompt(q, k, v, q_idx, k_idx, w_idx):
    b, L = q.shape[:2]
    heads, d = q.shape[3], q.shape[4]
    ih, e = q_idx.shape[2], q_idx.shape[3]
    n_sel = min(TOPK_MAX, L // 4)
    qh = jnp.transpose(q.reshape(b, L, heads, d), (0, 2, 1, 3)).astype(_BF16)
    qih = jnp.transpose(q_idx, (0, 2, 1, 3)).astype(_BF16)
    wt = jnp.transpose(w_idx, (0, 2, 1)).astype(_F32)
    kk = k.reshape(b, L, d).astype(_BF16)
    vt = jnp.transpose(v.reshape(b, L, d), (0, 2, 1)).astype(_BF16)
    return pl.pallas_call(
        functools.partial(_dsa_prompt_kernel, n_sel=n_sel),
        grid=(b, L // _TQ),
        in_specs=[
            pl.BlockSpec((1, heads, _TQ, d), lambda i, j: (i, 0, j, 0)),
            pl.BlockSpec((1, ih, _TQ, e), lambda i, j: (i, 0, j, 0)),
            pl.BlockSpec((1, ih, _TQ), lambda i, j: (i, 0, j)),
            pl.BlockSpec((1, L, e), lambda i, j: (i, 0, 0)),
            pl.BlockSpec((1, L, d), lambda i, j: (i, 0, 0)),
            pl.BlockSpec((1, d, L), lambda i, j: (i, 0, 0)),
        ],
        out_specs=pl.BlockSpec((1, _TQ, heads * d), lambda i, j: (i, j, 0)),
        out_shape=jax.ShapeDtypeStruct((b, L, heads * d), _F32),
        scratch_shapes=[pltpu.VMEM((L, _TQ), jnp.int32)],
        compiler_params=pltpu.CompilerParams(dimension_semantics=("parallel", "arbitrary")),
        name="dsa_prompt",
    )(qh, qih, wt, k_idx.astype(_BF16), kk, vt)


def _ssd_scan(x, dt, a, bm, cm, s0):
    b, l, h, p = x.shape
    g, n = bm.shape[2], bm.shape[3]
    r = h // g
    q = SSD_CHUNK if l % SSD_CHUNK == 0 else l
    c = l // q
    f32 = jnp.float32
    xdt = (x.astype(f32) * dt[..., None]).reshape(b, c, q, g, r, p)
    acs = jnp.cumsum((dt * a).reshape(b, c, q, g, r), axis=2)
    bc = bm.astype(f32).reshape(b, c, q, g, n)
    cc = cm.astype(f32).reshape(b, c, q, g, n)
    acs_t = jnp.moveaxis(acs, 2, -1)
    tril = jnp.tril(jnp.ones((q, q), dtype=bool))
    seg = jnp.exp(jnp.where(tril, acs_t[..., :, None] - acs_t[..., None, :], -jnp.inf))
    cb = jnp.einsum('bcign,bcjgn->bcgij', cc, bc)
    y_diag = jnp.einsum('bcgrij,bcjgrp->bcigrp', cb[:, :, :, None] * seg, xdt)
    decay_end = jnp.exp(acs[:, :, -1:] - acs)
    states = jnp.einsum('bcjgn,bcjgrp->bcgrpn', bc, xdt * decay_end[..., None])
    chunk_decay = jnp.exp(acs[:, :, -1])

    def step(s, inp):
        st, dc = inp
        return dc[..., None, None] * s + st, s

    s_fin, s_in = lax.scan(step, s0.astype(f32).reshape(b, g, r, p, n),
                           (jnp.moveaxis(states, 1, 0), jnp.moveaxis(chunk_decay, 1, 0)))
    s_in = jnp.moveaxis(s_in, 0, 1)
    y_off = jnp.einsum('bcign,bcgrpn->bcigrp', cc, s_in) * jnp.exp(acs)[..., None]
    return (y_diag + y_off).reshape(b, l, h, p), s_fin.reshape(b, h, p, n)


def _ssd_mixer(z, xbc, dt_raw, conv_prev, ssm_prev, p):
    b, T = xbc.shape[:2]
    f32 = jnp.float32
    xin = jnp.concatenate([conv_prev.astype(xbc.dtype), xbc], axis=1)
    cw = p['b_conv_w']
    conv = p['b_conv_b'] + xin[:, 0:T] * cw[0]
    for w in range(1, CONV_W):
        conv = conv + xin[:, w:w + T] * cw[w]
    xbc_c = jax.nn.silu(conv)
    xs, bm, cm = jnp.split(xbc_c, [SSM_INNER, SSM_INNER + SSM_GROUPS * D_STATE], axis=-1)
    xs = xs.reshape(b, T, SSM_HEADS, SSM_P)
    bm = bm.reshape(b, T, SSM_GROUPS, D_STATE)
    cm = cm.reshape(b, T, SSM_GROUPS, D_STATE)
    dt = jax.nn.softplus(dt_raw.astype(f32) + p['b_dt_bias'].astype(f32))
    a = -jnp.exp(p['b_a_log'].astype(f32))
    y, s_fin = _ssd_scan(xs, dt, a, bm, cm, ssm_prev)
    y = y + p['b_d'].astype(f32)[:, None] * xs.astype(f32)
    y = y.reshape(b, T, SSM_INNER) * jax.nn.silu(z.astype(f32))
    out = _rmsnorm(y, p['b_norm_g']).astype(z.dtype)
    return out, xin[:, -(CONV_W - 1):], s_fin


def _split_dot(a, b_mat, a_is_exact):
    x = b_mat if a_is_exact else a
    hi = x.astype(_BF16)
    lo = (x - hi.astype(_F32)).astype(_BF16)
    if a_is_exact:
        return jnp.dot(a, hi, preferred_element_type=_F32) + jnp.dot(a, lo, preferred_element_type=_F32)
    return jnp.dot(hi, b_mat, preferred_element_type=_F32) + jnp.dot(lo, b_mat, preferred_element_type=_F32)


def _ssd_prompt_kernel(xbc_ref, z_ref, dtc_ref, dtr_ref, cw_ref, cb_ref, hc_ref, hr_ref, ng_ref,
                       y_ref, conv_ref, st_ref, buf_sc, st_sc):
    c = pl.program_id(1)
    q = xbc_ref.shape[1]
    heads = SSM_HEADS
    inner = z_ref.shape[2]
    hp = inner // heads
    n = (xbc_ref.shape[2] - inner) // (2 * SSM_GROUPS)
    rep = heads // SSM_GROUPS
    lead = _SUB - (CONV_W - 1)

    @pl.when(c == 0)
    def _():
        buf_sc[0:_SUB, :] = jnp.zeros((_SUB, buf_sc.shape[1]), _F32)
        st_sc[...] = jnp.zeros_like(st_sc)

    xbc = xbc_ref[0]
    buf_sc[_SUB:_SUB + q, :] = xbc
    conv = cb_ref[...] + buf_sc[lead:lead + q, :] * cw_ref[0:1, :]
    for w in range(1, CONV_W):
        conv = conv + buf_sc[lead + w:lead + w + q, :] * cw_ref[w:w + 1, :]
    act = conv * jax.nn.sigmoid(conv)
    tail = buf_sc[q + lead:q + _SUB, :]
    buf_sc[lead:_SUB, :] = tail

    dt_c = _softplus(dtc_ref[0] + hc_ref[0:1, :])
    dt_r = _softplus(dtr_ref[0] + hr_ref[:, 0:1])
    da_c = dt_c * -jnp.exp(hc_ref[1:2, :])
    da_r = dt_r * -jnp.exp(hr_ref[:, 1:2])
    ri = lax.broadcasted_iota(jnp.int32, (q, q), 0)
    ci = lax.broadcasted_iota(jnp.int32, (q, q), 1)
    causal = ci <= ri
    acs_c = _split_dot(jnp.where(causal, 1.0, 0.0).astype(_BF16), da_c, True)
    acs_r = _split_dot(da_r, jnp.where(ri <= ci, 1.0, 0.0).astype(_BF16), False)
    acs_end = acs_c[q - 1:q, :]
    decay_end = jnp.exp(acs_end - acs_c)
    grow = jnp.exp(acs_c)

    ys = []
    for hh in range(heads):
        g = hh // rep
        xs = act[:, hh * hp:(hh + 1) * hp]
        bm = act[:, inner + g * n:inner + (g + 1) * n]
        cm = act[:, inner + SSM_GROUPS * n + g * n:inner + SSM_GROUPS * n + (g + 1) * n]
        xdt = xs * dt_c[:, hh:hh + 1]
        cbm = lax.dot_general(cm.astype(_BF16), bm.astype(_BF16), _NT, preferred_element_type=_F32)
        seg = jnp.exp(jnp.where(causal, acs_c[:, hh:hh + 1] - acs_r[hh:hh + 1, :], -jnp.inf))
        y = jnp.dot((cbm * seg).astype(_BF16), xdt.astype(_BF16), preferred_element_type=_F32)
        s_in = st_sc[hh]
        y = y + jnp.dot(cm.astype(_BF16), s_in.astype(_BF16), preferred_element_type=_F32) * grow[:, hh:hh + 1]
        new = jnp.dot(bm.T.astype(_BF16), (xdt * decay_end[:, hh:hh + 1]).astype(_BF16), preferred_element_type=_F32)
        st_sc[hh] = jnp.exp(acs_end[:, hh:hh + 1]) * s_in + new
        ys.append(y + hr_ref[hh:hh + 1, 2:3] * xs)
    y = jnp.concatenate(ys, axis=1)
    zz = z_ref[0]
    y_ref[0] = _rms(y * (zz * jax.nn.sigmoid(zz)), ng_ref[...])

    @pl.when(c == pl.num_programs(1) - 1)
    def _():
        conv_ref[0] = tail
        st_ref[0] = st_sc[...]


def _ssd_prompt(z, xbc, dt_raw, p):
    b, L, inner = z.shape
    heads = dt_raw.shape[-1]
    ch = xbc.shape[-1]
    q = SSD_CHUNK
    assert L % q == 0
    assert heads == SSM_HEADS and heads <= _SUB
    hc = jnp.stack([p['b_dt_bias'], p['b_a_log'], p['b_d']]).astype(_F32)
    hc_l = jnp.pad(hc, ((0, 0), (0, _LANES - heads)))
    hr_s = jnp.pad(hc.T, ((0, _SUB - heads), (0, _LANES - 3)))
    dt_l = jnp.pad(dt_raw, ((0, 0), (0, 0), (0, _LANES - heads)))
    dt_s = jnp.pad(jnp.swapaxes(dt_raw, 1, 2), ((0, 0), (0, _SUB - heads), (0, 0)))
    y, conv_new, st = pl.pallas_call(
        _ssd_prompt_kernel,
        grid=(b, L // q),
        in_specs=[pl.BlockSpec((1, q, ch), lambda i, c: (i, c, 0)), pl.BlockSpec((1, q, inner), lambda i, c: (i, c, 0)),
                  pl.BlockSpec((1, q, _LANES), lambda i, c: (i, c, 0)), pl.BlockSpec((1, _SUB, q), lambda i, c: (i, 0, c)),
                  pl.BlockSpec((CONV_W, ch), lambda i, c: (0, 0)), pl.BlockSpec((1, ch), lambda i, c: (0, 0)),
                  pl.BlockSpec((3, _LANES), lambda i, c: (0, 0)), pl.BlockSpec((_SUB, _LANES), lambda i, c: (0, 0)),
                  pl.BlockSpec((1, inner), lambda i, c: (0, 0))],
        out_specs=[pl.BlockSpec((1, q, inner), lambda i, c: (i, c, 0)),
                   pl.BlockSpec((1, CONV_W - 1, ch), lambda i, c: (i, 0, 0)),
                   pl.BlockSpec((1, heads, D_STATE, inner // heads), lambda i, c: (i, 0, 0, 0))],
        out_shape=[jax.ShapeDtypeStruct((b, L, inner), _F32), jax.ShapeDtypeStruct((b, CONV_W - 1, ch), _F32),
                   jax.ShapeDtypeStruct((b, heads, D_STATE, inner // heads), _F32)],
        scratch_shapes=[pltpu.VMEM((q + _SUB, ch), _F32), pltpu.VMEM((heads, D_STATE, inner // heads), _F32)],
        compiler_params=pltpu.CompilerParams(dimension_semantics=("parallel", "arbitrary")),
        name="ssd_prompt",
    )(xbc, z, dt_l, dt_s, p['b_conv_w'].astype(_F32), p['b_conv_b'].reshape(1, ch).astype(_F32),
      hc_l, hr_s, p['b_norm_g'].reshape(1, inner).astype(_F32))
    return y, conv_new, jnp.swapaxes(st, 2, 3)


def _stick_breaking(q, k, v, q_pos, k_pos):
    z = jnp.einsum('btgrd,bsgd->bgrts', q, k).astype(jnp.float32) * HEAD_DIM ** -0.5
    m = k_pos[None, :] < q_pos[:, None]
    log_1m = jnp.where(m, jax.nn.log_sigmoid(-z), 0.0)
    cum = jnp.cumsum(log_1m, axis=-1)
    log_w = jax.nn.log_sigmoid(z) + cum[..., -1:] - cum
    w = jnp.where(m, jnp.exp(log_w), 0.0).astype(v.dtype)
    return jnp.einsum('bgrts,bsgd->btgrd', w, v)


def _heads_to_rows(out_t, heads):
    stacked = jnp.concatenate([out_t[:, h * _TQ:(h + 1) * _TQ] for h in range(heads)], axis=0)
    return stacked.T


def _softplus(z):
    return jnp.maximum(z, 0.0) + jnp.log(1.0 + jnp.exp(-jnp.abs(z)))


def _sb_prompt_kernel(q_ref, k_ref, vt_ref, o_ref, *, rep):
    j = pl.program_id(1)
    heads, d = q_ref.shape[1], q_ref.shape[3]
    groups = k_ref.shape[1]
    gw = rep * _TQ
    width = heads * _TQ
    qg = [q_ref[0, g * rep:(g + 1) * rep].reshape(gw, d) for g in range(groups)]
    row = lax.broadcasted_iota(jnp.int32, (_TK, _TK), 0)
    col = lax.broadcasted_iota(jnp.int32, (_TK, _TK), 1)
    after = jnp.where(col > row, 1.0, 0.0).astype(_BF16)
    s_loc = lax.broadcasted_iota(jnp.int32, (_TK, width), 0)
    t_loc = lax.broadcasted_iota(jnp.int32, (_TK, width), 1) & (_TQ - 1)
    diag = _TQ // _TK

    def chunk(c, carry, diag_off=None):
        masked = diag_off is not None
        if masked:
            strictly_before = s_loc + diag_off < t_loc
        tail, accs = carry
        sl = pl.ds(pl.multiple_of(c * _TK, _TK), _TK)
        z = jnp.concatenate(
            [lax.dot_general(k_ref[0, g, sl, :], qg[g], (((1,), (1,)), ((), ())), preferred_element_type=_F32)
             for g in range(groups)], axis=1) * (d ** -0.5)
        sp = _softplus(z)
        if masked:
            sp = jnp.where(strictly_before, sp, 0.0)
        hi = sp.astype(_BF16)
        lo = (sp - hi.astype(_F32)).astype(_BF16)
        later = (jnp.dot(after, hi, preferred_element_type=_F32)
                 + jnp.dot(after, lo, preferred_element_type=_F32))
        w = jnp.exp(z - sp - later - tail)
        if masked:
            w = jnp.where(strictly_before, w, 0.0)
        wb = w.astype(_BF16)
        accs = tuple(accs[g] + jnp.dot(vt_ref[0, g, :, sl], wb[:, g * gw:(g + 1) * gw],
                                       preferred_element_type=_F32) for g in range(groups))
        return tail + jnp.sum(sp, axis=0, keepdims=True), accs

    init = (jnp.zeros((1, width), _F32), tuple(jnp.zeros((d, gw), _F32) for _ in range(groups)))
    carry = init
    for i in reversed(range(diag)):
        carry = chunk(j * diag + i, carry, diag_off=i * _TK)
    assert diag % 2 == 0
    _, accs = _loop_pairs(j * (diag // 2), lambda i, cr: chunk(j * diag - 1 - i, cr), carry)
    o_ref[0] = _heads_to_rows(jnp.concatenate(accs, axis=1), heads)


def _sb_prompt(q, k, v):
    b, L, groups, rep, d = q.shape
    heads = groups * rep
    qh = jnp.transpose(q.reshape(b, L, heads, d), (0, 2, 1, 3)).astype(_BF16)
    kg = jnp.transpose(k, (0, 2, 1, 3)).astype(_BF16)
    vt = jnp.transpose(v, (0, 2, 3, 1)).astype(_BF16)
    return pl.pallas_call(
        functools.partial(_sb_prompt_kernel, rep=rep),
        grid=(b, L // _TQ),
        in_specs=[
            pl.BlockSpec((1, heads, _TQ, d), lambda i, j: (i, 0, j, 0)),
            pl.BlockSpec((1, groups, L, d), lambda i, j: (i, 0, 0, 0)),
            pl.BlockSpec((1, groups, d, L), lambda i, j: (i, 0, 0, 0)),
        ],
        out_specs=pl.BlockSpec((1, _TQ, heads * d), lambda i, j: (i, j, 0)),
        out_shape=jax.ShapeDtypeStruct((b, L, heads * d), _F32),
        compiler_params=pltpu.CompilerParams(dimension_semantics=("parallel", "arbitrary")),
        name="sb_prompt",
    )(qh, kg, vt)


def _causal_mha_kernel(q_ref, k_ref, vt_ref, o_ref, *, scale):
    j = pl.program_id(1)
    heads = q_ref.shape[1]
    dv = vt_ref.shape[2]
    width = heads * _TQ
    s_loc = lax.broadcasted_iota(jnp.int32, (_TK, width), 0)
    t_loc = lax.broadcasted_iota(jnp.int32, (_TK, width), 1) & (_TQ - 1)
    diag = _TQ // _TK

    def chunk(c, carry, diag_off=None):
        masked = diag_off is not None
        if masked:
            visible = s_loc + diag_off <= t_loc
        m, l, accs = carry
        sl = pl.ds(pl.multiple_of(c * _TK, _TK), _TK)
        logit = jnp.concatenate(
            [lax.dot_general(k_ref[0, h, sl, :], q_ref[0, h], (((1,), (1,)), ((), ())), preferred_element_type=_F32)
             for h in range(heads)], axis=1) * scale
        if masked:
            logit = jnp.where(visible, logit, _NEG_BIG)
        m_new = jnp.maximum(m, jnp.max(logit, axis=0, keepdims=True))
        p = jnp.exp(logit - m_new)
        alpha = jnp.exp(m - m_new)
        l = alpha * l + jnp.sum(p, axis=0, keepdims=True)
        pb = p.astype(_BF16)
        accs = tuple(alpha[:, h * _TQ:(h + 1) * _TQ] * accs[h]
                     + jnp.dot(vt_ref[0, h, :, sl], pb[:, h * _TQ:(h + 1) * _TQ], preferred_element_type=_F32)
                     for h in range(heads))
        return m_new, l, accs

    init = (jnp.full((1, width), _NEG_BIG, _F32), jnp.zeros((1, width), _F32),
            tuple(jnp.zeros((dv, _TQ), _F32) for _ in range(heads)))
    carry = init
    for i in range(diag):
        carry = chunk(j * diag + i, carry, diag_off=i * _TK)
    assert diag % 2 == 0
    _, l, accs = _loop_pairs(j * (diag // 2), lambda i, cr: chunk(i, cr), carry)
    o_ref[0] = _heads_to_rows(jnp.concatenate(accs, axis=1) / l, heads)


def _mla_prompt(q_nope, q_rope, k_nope, k_rope, v):
    b, L, heads, _ = q_nope.shape
    dv = v.shape[-1]
    q = jnp.transpose(jnp.concatenate([q_nope, q_rope], axis=-1), (0, 2, 1, 3)).astype(_BF16)
    kr = jnp.broadcast_to(k_rope[:, :, None, :], (b, L, heads, k_rope.shape[-1]))
    k = jnp.transpose(jnp.concatenate([k_nope, kr], axis=-1), (0, 2, 1, 3)).astype(_BF16)
    vt = jnp.transpose(v, (0, 2, 3, 1)).astype(_BF16)
    dq = q.shape[-1]
    return pl.pallas_call(
        functools.partial(_causal_mha_kernel, scale=MLA_SCALE),
        grid=(b, L // _TQ),
        in_specs=[
            pl.BlockSpec((1, heads, _TQ, dq), lambda i, j: (i, 0, j, 0)),
            pl.BlockSpec((1, heads, L, dq), lambda i, j: (i, 0, 0, 0)),
            pl.BlockSpec((1, heads, dv, L), lambda i, j: (i, 0, 0, 0)),
        ],
        out_specs=pl.BlockSpec((1, _TQ, heads * dv), lambda i, j: (i, j, 0)),
        out_shape=jax.ShapeDtypeStruct((b, L, heads * dv), _F32),
        compiler_params=pltpu.CompilerParams(dimension_semantics=("parallel", "arbitrary")),
        name="mla_prompt",
    )(q, k, vt)


def _mla_latent(q_nope, q_rope, c_all, r_all, w_uk, w_uv, q_pos, k_pos):
    b, T = q_nope.shape[:2]
    q_lat = jnp.einsum('bthd,chd->bthc', q_nope, w_uk)
    sc = (jnp.einsum('bthc,bsc->bhts', q_lat, c_all)
          + jnp.einsum('bthr,bsr->bhts', q_rope, r_all)).astype(jnp.float32) * MLA_SCALE
    sc = jnp.where(k_pos[None, :] <= q_pos[:, None], sc, -jnp.inf)
    pr = jax.nn.softmax(sc, axis=-1).astype(c_all.dtype)
    o_lat = jnp.einsum('bhts,bsc->bthc', pr, c_all)
    return jnp.einsum('bthc,chd->bthd', o_lat, w_uv).reshape(b, T, D_HEADS * D_V)


def _mixer_prompt(h, g_mix, pos, p):
    b, L = h.shape[:2]
    u = _project(h, g_mix, pos, p)
    a_out = _dsa_prompt(u['a_q'], u['a_k'], u['a_v'], u['a_qi'], u['a_ki'], u['a_w'])
    b_out, conv_new, ssm_new = _ssd_prompt(u['b_z'], u['b_xbc'], u['b_dt'], p)
    c_out = _sb_prompt(u['c_q'], u['c_k'], u['c_v'])
    w_ukv = jnp.concatenate([p['d_w_uk'].reshape(KV_RANK, D_HEADS * D_NOPE),
                             p['d_w_uv'].reshape(KV_RANK, D_HEADS * D_V)], axis=1)
    kv_up = _norm_matmul(u['d_ckv_raw'].reshape(b * L, KV_RANK), p['d_kv_norm_g'], w_ukv)
    k_nope = kv_up[:, :D_HEADS * D_NOPE].reshape(b, L, D_HEADS, D_NOPE)
    v_d = kv_up[:, D_HEADS * D_NOPE:].reshape(b, L, D_HEADS, D_V)
    d_out = _mla_prompt(u['d_qn'], u['d_qr'], k_nope, u['d_kr'], v_d)
    mix = jnp.concatenate([a_out, b_out, c_out, d_out], axis=-1)
    new = (u['a_k'], u['a_v'], u['a_ki'], u['c_k'], u['c_v'], u['d_ckv'], u['d_kr'],
           conv_new, ssm_new.astype(h.dtype))
    return mix, new


_PPS = 32
_SUB = 8


def _page_specs(rows, width, layer, n_pages, descending):
    specs = []
    for slot in range(_PPS):
        if descending:
            idx = lambda b, c, pt, slot=slot: (layer, pt[b, n_pages - (c + 1) * _PPS + slot], 0, 0)
        else:
            idx = lambda b, c, pt, slot=slot: (layer, pt[b, c * _PPS + slot], 0, 0)
        specs.append(pl.BlockSpec((None, None, rows, width), idx))
    return specs


def _seq_spec(rows, width):
    return pl.BlockSpec((1, rows, width), lambda b, c, pt: (b, 0, 0))


def _pad_rows(x):
    return jnp.pad(x, ((0, 0), (0, _SUB - x.shape[1]), (0, 0)))


_NT = (((1,), (1,)), ((), ()))


def _paged_call(kernel_fn, page_table, seq_inputs, pools, layer, descending, out_shape, out_spec, scratch, name):
    b, n_pages = page_table.shape
    assert n_pages % _PPS == 0
    in_specs = [_seq_spec(x.shape[1], x.shape[2]) for x in seq_inputs]
    args = list(seq_inputs)
    for pool in pools:
        in_specs += _page_specs(pool.shape[2], pool.shape[3], layer, n_pages, descending)
        args += [pool] * _PPS
    return pl.pallas_call(
        kernel_fn,
        grid_spec=pltpu.PrefetchScalarGridSpec(
            num_scalar_prefetch=1, grid=(b, n_pages // _PPS),
            in_specs=in_specs, out_specs=out_spec, scratch_shapes=scratch),
        out_shape=out_shape,
        compiler_params=pltpu.CompilerParams(dimension_semantics=("parallel", "arbitrary")),
        name=name,
    )(page_table, *args)


def _pool_t(pool):
    lead = pool.shape[:2]
    flat = pool.reshape(lead + (PAGE_SIZE, -1))
    return jnp.swapaxes(flat, 2, 3)


def _suffix_sum(x):
    width = x.shape[1]
    lane = lax.broadcasted_iota(jnp.int32, x.shape, 1)
    k = 1
    while k < width:
        x = x + jnp.where(lane < width - k, pltpu.roll(x, width - k, axis=1), 0.0)
        k *= 2
    return x


def _head_rows(rows):
    pad = jnp.zeros((_SUB - len(rows), rows[0].shape[1]), _F32)
    return jnp.concatenate(rows + [pad], axis=0)


def _sb_decode_kernel(pt_ref, q_ref, *refs, scale, heads, rep):
    k_refs, v_refs = refs[:_PPS], refs[_PPS:2 * _PPS]
    o_ref, tail_sc, acc_sc = refs[2 * _PPS:]
    c = pl.program_id(1)
    d = q_ref.shape[1] // heads

    @pl.when(c == 0)
    def _():
        tail_sc[...] = jnp.zeros_like(tail_sc)
        acc_sc[...] = jnp.zeros_like(acc_sc)

    zs = []
    for slot in range(_PPS):
        kt = k_refs[slot][...]
        rows = []
        for hh in range(heads):
            g = hh // rep
            rows.append(jnp.sum(kt[g * d:(g + 1) * d, :] * q_ref[0, hh * d:(hh + 1) * d, :], axis=0, keepdims=True))
        zs.append(_head_rows(rows))
    z = jnp.concatenate(zs, axis=1) * scale
    incl = _suffix_sum(_softplus(z))
    w = jnp.exp(z - incl - tail_sc[...][:, 0:1])
    for slot in range(_PPS):
        vt = v_refs[slot][...]
        for hh in range(heads):
            g = hh // rep
            acc_sc[hh * d:(hh + 1) * d, :] += (vt[g * d:(g + 1) * d, :]
                                               * w[hh:hh + 1, slot * PAGE_SIZE:(slot + 1) * PAGE_SIZE])
    tail_sc[...] = tail_sc[...] + incl[:, 0:1]

    @pl.when(c == pl.num_programs(1) - 1)
    def _():
        o_ref[0] = jnp.broadcast_to(jnp.sum(acc_sc[...], axis=1, keepdims=True), acc_sc.shape)


def _sb_decode(q, pool_k, pool_v, layer, page_table):
    b, t, groups, rep, d = q.shape
    assert t == 1
    heads = groups * rep
    qb = jnp.broadcast_to(q.reshape(b, heads * d, 1), (b, heads * d, PAGE_SIZE)).astype(_F32)
    out = _paged_call(
        functools.partial(_sb_decode_kernel, scale=d ** -0.5, heads=heads, rep=rep), page_table, [qb],
        [_pool_t(pool_k), _pool_t(pool_v)], layer, True, jax.ShapeDtypeStruct((b, heads * d, PAGE_SIZE), _F32),
        pl.BlockSpec((1, heads * d, PAGE_SIZE), lambda i, c, pt: (i, 0, 0)),
        [pltpu.VMEM((_SUB, PAGE_SIZE), _F32), pltpu.VMEM((heads * d, PAGE_SIZE), _F32)], "sb_decode")
    return out[:, :, 0]


def _softmax_pages(logits, values, m, l, acc):
    mx = logits[0].max(axis=1, keepdims=True)
    for lg in logits[1:]:
        mx = jnp.maximum(mx, lg.max(axis=1, keepdims=True))
    m_new = jnp.maximum(m, mx)
    alpha = jnp.exp(m - m_new)
    l = alpha * l
    acc = alpha[:, :acc.shape[1]] * acc
    for lg, val in zip(logits, values):
        pr = jnp.exp(lg - m_new)
        l = l + jnp.sum(pr, axis=1, keepdims=True)
        acc = acc + jnp.dot(pr.astype(_BF16), val, preferred_element_type=_F32)
    return m_new, l, acc


def _mla_decode_kernel(pt_ref, ql_ref, qr_ref, lnew_ref, cnew_ref, *refs, scale):
    c_refs, r_refs = refs[:_PPS], refs[_PPS:2 * _PPS]
    o_ref, m_sc, l_sc, acc_sc = refs[2 * _PPS:]
    c = pl.program_id(1)

    @pl.when(c == 0)
    def _():
        m_sc[...] = lnew_ref[0]
        l_sc[...] = jnp.ones_like(l_sc)
        acc_sc[...] = jnp.broadcast_to(cnew_ref[0], acc_sc.shape)

    ql, qr = ql_ref[0], qr_ref[0]
    lat = [c_refs[s][...].astype(_BF16) for s in range(_PPS)]
    logits = [(lax.dot_general(ql, lat[s], _NT, preferred_element_type=_F32)
               + jnp.dot(qr, r_refs[s][...].astype(_BF16), preferred_element_type=_F32)) * scale
              for s in range(_PPS)]
    m, l, acc = _softmax_pages(logits, lat, m_sc[...], l_sc[...], acc_sc[...])
    m_sc[...], l_sc[...], acc_sc[...] = m, l, acc

    @pl.when(c == pl.num_programs(1) - 1)
    def _():
        o_ref[0] = acc / l


def _mla_decode(q_nope, q_rope, c_new, r_new, pool_c, pool_r, w_uk, w_uv, layer, page_table):
    b, t, heads, _ = q_nope.shape
    assert t == 1
    q_lat = jnp.einsum('bthd,chd->bhc', q_nope, w_uk)
    qr = q_rope[:, 0]
    l_new = (jnp.einsum('bhc,bc->bh', q_lat, c_new[:, 0]) + jnp.einsum('bhr,br->bh', qr, r_new[:, 0])) * MLA_SCALE
    l_new = jnp.broadcast_to(_pad_rows(l_new[:, :, None]), (b, _SUB, PAGE_SIZE)).astype(_F32)
    rank = pool_c.shape[-1]
    o_lat = _paged_call(
        functools.partial(_mla_decode_kernel, scale=MLA_SCALE), page_table,
        [_pad_rows(q_lat).astype(_BF16), _pad_rows(qr).astype(_BF16), l_new, c_new.astype(_F32)],
        [pool_c, _pool_t(pool_r)], layer, False, jax.ShapeDtypeStruct((b, _SUB, rank), _F32),
        pl.BlockSpec((1, _SUB, rank), lambda i, c, pt: (i, 0, 0)),
        [pltpu.VMEM((_SUB, PAGE_SIZE), _F32), pltpu.VMEM((_SUB, PAGE_SIZE), _F32), pltpu.VMEM((_SUB, rank), _F32)],
        "mla_decode")
    return jnp.einsum('bhc,chd->bhd', o_lat[:, :heads], w_uv).reshape(b, t, heads * w_uv.shape[-1])


def _idx_decode_kernel(pt_ref, qi_ref, w_ref, *refs):
    ki_refs, o_ref = refs[:_PPS], refs[_PPS]
    qi, w = qi_ref[0], w_ref[0]
    for s in range(_PPS):
        dots = jnp.dot(qi, ki_refs[s][...].astype(_BF16), preferred_element_type=_F32)
        o_ref[0, s:s + 1, :] = jnp.sum(w * jnp.maximum(dots, 0.0), axis=0, keepdims=True) * IDX_SCALE


def _dsa_decode_kernel(pt_ref, sc_ref, scnew_ref, q_ref, lnew_ref, vnew_ref, *refs, n_sel, heads):
    k_refs, v_refs = refs[:_PPS], refs[_PPS:2 * _PPS]
    o_ref, bias_sc, m_sc, l_sc, acc_sc = refs[2 * _PPS:]
    c = pl.program_id(1)
    n_rows = sc_ref.shape[1]

    @pl.when(c == 0)
    def _():
        key = _sortable_key(sc_ref[0])
        key_new = _sortable_key(scnew_ref[0])

        def count(pred_tile, pred_new):
            return jnp.sum(jnp.where(pred_tile, 1, 0)) + jnp.max(jnp.where(pred_new, 1, 0))

        def bit_step(i, carry):
            t, cnt_t = carry
            cand = t ^ lax.shift_left(jnp.int32(1), 31 - i)
            cnt = count(key >= cand, key_new >= cand)
            ok = cnt >= n_sel
            return jnp.where(ok, cand, t), jnp.where(ok, cnt, cnt_t)

        thr, cnt_thr = lax.fori_loop(0, 32, bit_step, (jnp.int32(_INT_MIN), jnp.int32(n_rows * PAGE_SIZE + 1)))
        bias_sc[...] = jnp.where(key >= thr, 0.0, _NEG_BIG)
        sel_new = key_new >= thr

        def init(sel_new):
            m_sc[...] = jnp.where(sel_new, lnew_ref[0], _NEG_BIG)
            l_sc[...] = jnp.where(sel_new, 1.0, 0.0) + jnp.zeros_like(l_sc)
            acc_sc[...] = jnp.where(sel_new, vnew_ref[0], 0.0)

        init(sel_new)

        @pl.when(cnt_thr > n_sel)
        def _():
            room = (n_sel - count(key > thr, key_new > thr)).astype(_F32)
            eq = key == thr
            eqf = jnp.where(eq, 1.0, 0.0)
            r_i = lax.broadcasted_iota(jnp.int32, (PAGE_SIZE, PAGE_SIZE), 0)
            c_i = lax.broadcasted_iota(jnp.int32, (PAGE_SIZE, PAGE_SIZE), 1)
            in_row = jnp.dot(eqf.astype(_BF16), jnp.where(r_i < c_i, 1.0, 0.0).astype(_BF16),
                             preferred_element_type=_F32)
            row_tot = jnp.broadcast_to(jnp.sum(eqf, axis=1, keepdims=True), eqf.shape).astype(_BF16)
            rr = lax.broadcasted_iota(jnp.int32, (n_rows, n_rows), 0)
            rc = lax.broadcasted_iota(jnp.int32, (n_rows, n_rows), 1)
            rows_before = jnp.dot(jnp.where(rc < rr, 1.0, 0.0).astype(_BF16), row_tot, preferred_element_type=_F32)
            keep = (key > thr) | (eq & (in_row + rows_before < room))
            bias_sc[...] = jnp.where(keep, 0.0, _NEG_BIG)
            init((key_new > thr) | ((key_new == thr) & (jnp.sum(eqf) < room)))

    d = k_refs[0].shape[0]
    lg = []
    for s in range(_PPS):
        kt = k_refs[s][...]
        rows = [jnp.sum(kt * q_ref[0, hh * d:(hh + 1) * d, :], axis=0, keepdims=True) for hh in range(heads)]
        lg.append(_head_rows(rows) * (d ** -0.5) + bias_sc[pl.ds(c * _PPS + s, 1), :])
    logit = jnp.concatenate(lg, axis=1)
    m_old = m_sc[...][:, 0:1]
    m_new = jnp.maximum(m_old, jnp.max(logit, axis=1, keepdims=True))
    alpha = jnp.exp(m_old - m_new)
    pr = jnp.exp(logit - m_new)
    l_new = alpha * l_sc[...][:, 0:1] + jnp.sum(pr, axis=1, keepdims=True)
    for hh in range(heads):
        acc_h = alpha[hh:hh + 1, :] * acc_sc[hh * d:(hh + 1) * d, :]
        for s in range(_PPS):
            acc_h = acc_h + v_refs[s][...] * pr[hh:hh + 1, s * PAGE_SIZE:(s + 1) * PAGE_SIZE]
        acc_sc[hh * d:(hh + 1) * d, :] = acc_h
    m_sc[...] = jnp.broadcast_to(m_new, m_sc.shape)
    l_sc[...] = jnp.broadcast_to(l_new, l_sc.shape)

    @pl.when(c == pl.num_programs(1) - 1)
    def _():
        for hh in range(heads):
            tot = jnp.sum(acc_sc[hh * d:(hh + 1) * d, :], axis=1, keepdims=True) / l_new[hh:hh + 1, :]
            o_ref[0, hh * d:(hh + 1) * d, :] = jnp.broadcast_to(tot, (d, PAGE_SIZE))


def _dsa_decode(q, k_new, v_new, q_idx, ki_new, w_idx, pool_k, pool_v, pool_ki, layer, page_table):
    b, t = q.shape[:2]
    assert t == 1
    heads, d = q.shape[3], q.shape[4]
    ih, e = q_idx.shape[2], q_idx.shape[3]
    n_pages = page_table.shape[1]
    n_sel = min(TOPK_MAX, (n_pages * PAGE_SIZE + t) // 4)
    assert ih == _SUB
    w8 = jnp.broadcast_to(w_idx[:, 0, :, None], (b, ih, PAGE_SIZE)).astype(_F32)
    scores = _paged_call(
        _idx_decode_kernel, page_table, [q_idx[:, 0].astype(_BF16), w8], [_pool_t(pool_ki)], layer, False,
        jax.ShapeDtypeStruct((b, n_pages, PAGE_SIZE), _F32),
        pl.BlockSpec((1, _PPS, PAGE_SIZE), lambda i, c, pt: (i, c, 0)), [], "idx_decode")
    rel_new = jax.nn.relu(jnp.einsum('bhe,be->bh', q_idx[:, 0], ki_new[:, 0]))
    sc_new = jnp.einsum('bh,bh->b', w_idx[:, 0], rel_new) * IDX_SCALE
    sc_new = jnp.broadcast_to(sc_new[:, None, None], (b, 1, PAGE_SIZE)).astype(_F32)
    qh = q.reshape(b, heads, d)
    l_new = jnp.einsum('bhd,bd->bh', qh, k_new.reshape(b, d)) * d ** -0.5
    l_new = jnp.broadcast_to(_pad_rows(l_new[:, :, None]), (b, _SUB, PAGE_SIZE)).astype(_F32)
    qb = jnp.broadcast_to(qh.reshape(b, heads * d, 1), (b, heads * d, PAGE_SIZE)).astype(_F32)
    v_rep = jnp.tile(v_new.reshape(b, d), (1, heads))
    v_lane0 = jnp.zeros((b, heads * d, PAGE_SIZE), _F32).at[:, :, 0].set(v_rep)
    out = _paged_call(
        functools.partial(_dsa_decode_kernel, n_sel=n_sel, heads=heads), page_table,
        [scores, sc_new, qb, l_new, v_lane0],
        [_pool_t(pool_k), _pool_t(pool_v)], layer, False, jax.ShapeDtypeStruct((b, heads * d, PAGE_SIZE), _F32),
        pl.BlockSpec((1, heads * d, PAGE_SIZE), lambda i, c, pt: (i, 0, 0)),
        [pltpu.VMEM((n_pages, PAGE_SIZE), _F32), pltpu.VMEM((_SUB, PAGE_SIZE), _F32),
         pltpu.VMEM((_SUB, PAGE_SIZE), _F32), pltpu.VMEM((heads * d, PAGE_SIZE), _F32)], "dsa_decode")
    return out[:, :, 0].reshape(b, t, heads * d)


def _mixer_sample(h, g_mix, pos, p, l, caches, page_table, past_len):
    (ca_k, ca_v, ca_ki, cc_k, cc_v, cd_ckv, cd_kr, sb_conv, sb_ssm) = caches
    hn = h
    b, T = h.shape[:2]
    L = past_len + T
    k_pos = jnp.arange(L)
    u = _project(h, g_mix, pos, p)
    a_out = _dsa_decode(u['a_q'], u['a_k'], u['a_v'], u['a_qi'], u['a_ki'], u['a_w'], ca_k, ca_v, ca_ki, l, page_table)
    b_out, conv_new, ssm_new = _ssd_mixer(u['b_z'], u['b_xbc'], u['b_dt'], sb_conv[l], sb_ssm[l], p)
    c_out = _sb_decode(u['c_q'], cc_k, cc_v, l, page_table).reshape(b, T, C_HEADS * HEAD_DIM)
    d_out = _mla_decode(u['d_qn'], u['d_qr'], u['d_ckv'], u['d_kr'], cd_ckv, cd_kr, p['d_w_uk'], p['d_w_uv'],
                        l, page_table)
    mix = jnp.concatenate([a_out, b_out, c_out, d_out], axis=-1)
    new = (u['a_k'], u['a_v'], u['a_ki'], u['c_k'], u['c_v'], u['d_ckv'], u['d_kr'],
           conv_new.astype(sb_conv.dtype), ssm_new.astype(sb_ssm.dtype))
    return mix, new


def _mem_kv(mem, p):
    b, mt, dm = mem.shape
    kv = _norm_matmul(mem.reshape(b * mt, dm), p['g_mem_kv'], jnp.concatenate([p['w_mk'], p['w_mv']], axis=1))
    width = MEM_HEADS * MEM_HD
    return kv[:, :width].reshape(b, mt, MEM_HEADS, MEM_HD), kv[:, width:].reshape(b, mt, MEM_HEADS, MEM_HD)


def _post_mix_prompt(h, mix, mem_k, mem_v, p):
    b, L, dm = h.shape
    h2 = _mix_mem(h.reshape(b * L, dm), mix.reshape(b * L, dm), mem_k, mem_v, p, rows_per_batch=L)
    return _ffn(h2, p['g_ffn'], p['w_gate'], p['w_up'], p['w_down']).reshape(b, L, dm)


def _post_mix_sample(h, mix, mem_k, mem_v, p):
    b, L, dm = h.shape
    h1 = _matmul_residual(mix.reshape(b * L, dm), p['w_out'], h.reshape(b * L, dm))
    q = _norm_matmul(h1, p['g_mem_q'], p['w_mq']).reshape(b, L, MEM_HEADS, MEM_HD)
    sc = jnp.einsum('blhd,bmhd->bhlm', q, mem_k.astype(q.dtype)).astype(jnp.float32) * MEM_HD ** -0.5
    pr = jax.nn.softmax(sc, axis=-1).astype(q.dtype)
    o = jnp.einsum('bhlm,bmhd->blhd', pr, mem_v.astype(q.dtype)).reshape(b * L, MEM_HEADS * MEM_HD)
    h2 = _matmul_residual(o, p['w_mo'], h1)
    return _ffn(h2, p['g_ffn'], p['w_gate'], p['w_up'], p['w_down']).reshape(b, L, dm)


def _final_norm_kernel(x_ref, g_ref, o_ref):
    x = x_ref[...]
    o_ref[...] = x * lax.rsqrt(jnp.mean(x * x, axis=-1, keepdims=True) + NORM_EPS) * g_ref[...]


def _final_norm(x, g):
    shp = x.shape
    x2 = x.reshape(-1, shp[-1])
    n = x2.shape[0]
    tm = min(n, 1024)
    out = pl.pallas_call(
        _final_norm_kernel,
        grid=(n // tm,),
        in_specs=[pl.BlockSpec((tm, shp[-1]), lambda i: (i, 0)), pl.BlockSpec((1, shp[-1]), lambda i: (0, 0))],
        out_specs=pl.BlockSpec((tm, shp[-1]), lambda i: (i, 0)),
        out_shape=jax.ShapeDtypeStruct(x2.shape, x2.dtype),
    )(x2, g.reshape(1, -1))
    return out.reshape(shp)


def kernel(x_prompt, x_sample, mem_prompt, cache_a_k, cache_a_v, cache_a_kidx, cache_c_k, cache_c_v, cache_d_ckv, cache_d_krope, state_b_conv, state_b_ssm, cache_mem_k, cache_mem_v, page_table, g_mix, w_in, b_conv_w, b_conv_b, b_dt_bias, b_a_log, b_d, b_norm_g, d_q_norm_g, d_kv_norm_g, d_w_uq, d_w_uk, d_w_uv, w_out, g_mem_q, g_mem_kv, w_mq, w_mk, w_mv, w_mo, g_ffn, w_gate, w_up, w_down, g_final):
    s_prompt = x_prompt.shape[1]
    t_new = x_sample.shape[1]
    past_len = page_table.shape[1] * PAGE_SIZE
    pos_p = jnp.arange(s_prompt)
    pos_s = past_len + jnp.arange(t_new)
    caches = (cache_a_k, cache_a_v, cache_a_kidx, cache_c_k, cache_c_v, cache_d_ckv, cache_d_krope,
              state_b_conv, state_b_ssm)
    hp, hs = x_prompt, x_sample
    new_p, new_s, mem_ks, mem_vs = [], [], [], []
    for l in range(DEPTH):
        p = {'w_in': w_in[l], 'b_conv_w': b_conv_w[l], 'b_conv_b': b_conv_b[l], 'b_dt_bias': b_dt_bias[l],
             'b_a_log': b_a_log[l], 'b_d': b_d[l], 'b_norm_g': b_norm_g[l],
             'd_q_norm_g': d_q_norm_g[l], 'd_kv_norm_g': d_kv_norm_g[l], 'd_w_uq': d_w_uq[l],
             'd_w_uk': d_w_uk[l], 'd_w_uv': d_w_uv[l], 'w_out': w_out[l],
             'g_mem_q': g_mem_q[l], 'g_mem_kv': g_mem_kv[l], 'w_mq': w_mq[l], 'w_mk': w_mk[l],
             'w_mv': w_mv[l], 'w_mo': w_mo[l], 'g_ffn': g_ffn[l], 'w_gate': w_gate[l],
             'w_up': w_up[l], 'w_down': w_down[l]}
        mix_p, st_p = _mixer_prompt(hp, g_mix[l], pos_p, p)
        mk, mv = _mem_kv(mem_prompt, p)
        hp = _post_mix_prompt(hp, mix_p, mk, mv, p)
        new_p.append(st_p)
        mem_ks.append(mk)
        mem_vs.append(mv)
        mix_s, st_s = _mixer_sample(hs, g_mix[l], pos_s, p, l, caches, page_table, past_len)
        hs = _post_mix_sample(hs, mix_s, cache_mem_k[l], cache_mem_v[l], p)
        new_s.append(st_s)
    y_prompt = _final_norm(hp, g_final)
    y_sample = _final_norm(hs, g_final)
    (p_a_k, p_a_v, p_a_kidx, p_c_k, p_c_v, p_d_ckv, p_d_krope, p_b_conv, p_b_ssm) = [jnp.stack(t) for t in zip(*new_p)]
    (s_a_k, s_a_v, s_a_kidx, s_c_k, s_c_v, s_d_ckv, s_d_krope, s_b_conv, s_b_ssm) = [jnp.stack(t) for t in zip(*new_s)]
    p_mem_k = jnp.stack(mem_ks)
    p_mem_v = jnp.stack(mem_vs)
    return (y_prompt, y_sample,
            p_a_k, p_a_v, p_a_kidx, p_c_k, p_c_v, p_d_ckv, p_d_krope, p_b_conv, p_b_ssm, p_mem_k, p_mem_v,
            s_a_k, s_a_v, s_a_kidx, s_c_k, s_c_v, s_d_ckv, s_d_krope, s_b_conv, s_b_ssm)
```
